```python
import math
import jax
import jax.numpy as jnp
from jax import lax
import numpy as np

D_MODEL = 2048
BATCH = 2
SEQ = 4096
DEPTH = 4

N_MIXERS = 3
N_ATTN = (DEPTH + 2) // N_MIXERS
N_CONV = (DEPTH + 1) // N_MIXERS
N_LRU = DEPTH // N_MIXERS

ATTN_HEAD_DIM = 128
ATTN_HEADS = D_MODEL // (2 * ATTN_HEAD_DIM)
Q_BLOCK = 128
NUM_BUCKETS = 32
MAX_DISTANCE = 128
MASK_VALUE = -1e30

CONV_WIDTH = 31

LRU_WIDTH = D_MODEL
LRU_BLOCKS = 8
LRU_BLOCK_SIZE = LRU_WIDTH // LRU_BLOCKS
LRU_CONV_WIDTH = 4
LRU_C = 8.0

D_FF = 4 * D_MODEL

kernel_name = "hybrid_diffattn_conformer_rglru_trunk"


def rms_norm(x, g, eps=1e-6):
    xf = x.astype(jnp.float32)
    y = xf * lax.rsqrt(jnp.mean(xf * xf, axis=-1, keepdims=True) + eps)
    return (y * g.astype(jnp.float32)).astype(x.dtype)


def layer_norm(x, g, b, eps=1e-5):
    xf = x.astype(jnp.float32)
    mu = jnp.mean(xf, axis=-1, keepdims=True)
    xc = xf - mu
    y = xc * lax.rsqrt(jnp.mean(xc * xc, axis=-1, keepdims=True) + eps)
    return (y * g.astype(jnp.float32) + b.astype(jnp.float32)).astype(x.dtype)


def causal_depthwise_conv(u, w, b):
    width, channels = w.shape
    y = lax.conv_general_dilated(
        u, w[:, None, :].astype(u.dtype), window_strides=(1,),
        padding=[(width - 1, 0)], dimension_numbers=("NWC", "WIO", "NWC"),
        feature_group_count=channels)
    return y + b.astype(u.dtype)


def relative_buckets(n):
    n = jnp.maximum(n, 0)
    max_exact = NUM_BUCKETS // 2
    nf = jnp.maximum(n, 1).astype(jnp.float32)
    large = max_exact + (jnp.log(nf / max_exact) / math.log(MAX_DISTANCE / max_exact)
                         * (NUM_BUCKETS - max_exact)).astype(jnp.int32)
    large = jnp.minimum(large, NUM_BUCKETS - 1)
    return jnp.where(n < max_exact, n, large)


def diff_attention(h, w_qkv, lq1, lk1, lq2, lk2, subln_g, w_o, rel_bias, lambda_init):
    B, S, _ = h.shape
    H, DH = ATTN_HEADS, ATTN_HEAD_DIM
    qkv = h @ w_qkv
    q, k, v = jnp.split(qkv, 3, axis=-1)
    q = q.reshape(B, S, H, 2, DH) * (DH ** -0.5)
    k = k.reshape(B, S, H, 2, DH)
    v = v.reshape(B, S, H, 2 * DH)
    lam = (jnp.exp(jnp.sum(lq1.astype(jnp.float32) * lk1.astype(jnp.float32)))
           - jnp.exp(jnp.sum(lq2.astype(jnp.float32) * lk2.astype(jnp.float32)))
           + lambda_init)
    n_blocks = S // Q_BLOCK
    qb = q.reshape(B, n_blocks, Q_BLOCK, H, 2, DH).transpose(1, 0, 2, 3, 4, 5)
    k_pos = jnp.arange(S, dtype=jnp.int32)

    def query_block(args):
        qblk, blk = args
        q_pos = blk * Q_BLOCK + jnp.arange(Q_BLOCK, dtype=jnp.int32)
        rel = q_pos[:, None] - k_pos[None, :]
        bias = rel_bias[relative_buckets(rel)].astype(jnp.float32).transpose(2, 0, 1)
        logits = jnp.einsum("bqhcd,bkhcd->cbhqk", qblk, k).astype(jnp.float32) + bias
        logits = jnp.where(rel >= 0, logits, MASK_VALUE)
        p = jax.nn.softmax(logits, axis=-1)
        attn = (p[0] - lam * p[1]).astype(v.dtype)
        return jnp.einsum("bhqk,bkhe->bqhe", attn, v)

    out = lax.map(query_block, (qb, jnp.arange(n_blocks, dtype=jnp.int32)))
    out = out.transpose(1, 0, 2, 3, 4).reshape(B, S, H, 2 * DH)
    out = rms_norm(out, subln_g, eps=1e-5) * (1.0 - lambda_init)
    return out.reshape(B, S, H * 2 * DH) @ w_o


def conformer_conv(h, w_in, b_in, dw_w, dw_b, ln_g, ln_b, w_out, b_out):
    u = h @ w_in + b_in
    a, g = jnp.split(u, 2, axis=-1)
    u = a * jax.nn.sigmoid(g)
    u = causal_depthwise_conv(u, dw_w, dw_b)
    u = jax.nn.silu(layer_norm(u, ln_g, ln_b))
    return u @ w_out + b_out


def _lru_combine(c1, c2):
    a1, b1 = c1
    a2, b2 = c2
    return a1 * a2, a2 * b1 + b2


def rglru_block(h, w_in, conv_w, conv_b, w_a, b_a, w_i, b_i, lam, w_out):
    B, S, _ = h.shape
    uy = h @ w_in
    u, y = jnp.split(uy, 2, axis=-1)
    y = jax.nn.gelu(y)
    u = causal_depthwise_conv(u, conv_w, conv_b)
    ub = u.reshape(B, S, LRU_BLOCKS, LRU_BLOCK_SIZE)
    r = jax.nn.sigmoid(jnp.einsum("bsgi,gij->bsgj", ub, w_a).reshape(B, S, LRU_WIDTH) + b_a)
    i = jax.nn.sigmoid(jnp.einsum("bsgi,gij->bsgj", ub, w_i).reshape(B, S, LRU_WIDTH) + b_i)
    log_a = -LRU_C * r.astype(jnp.float32) * jax.nn.softplus(-lam.astype(jnp.float32))
    a = jnp.exp(log_a)
    mult = jnp.sqrt(-jnp.expm1(2.0 * log_a))
    bx = mult * (i * u).astype(jnp.float32)
    _, hs = lax.associative_scan(_lru_combine, (a, bx), axis=1)
    return (hs.astype(h.dtype) * y) @ w_out


def sq_relu_mlp(h, w1, w2):
    z = jax.nn.relu(h @ w1)
    return (z * z) @ w2


def setup_inputs(seed: int = 0) -> dict:
    key = jax.random.key(seed)
    ks = iter(jax.random.split(key, 40))
    D = D_MODEL
    f32 = jnp.float32

    def nrm(shape, scale):
        return jax.random.normal(next(ks), shape, f32) * scale

    def gain(shape):
        return 1.0 + nrm(shape, 0.02)

    a0 = jax.random.uniform(next(ks), (N_LRU, LRU_WIDTH), f32, 0.9, 0.999)
    return {
        "x": nrm((BATCH, SEQ, D), 1.0),
        "rel_bias": nrm((NUM_BUCKETS, ATTN_HEADS), 0.5),
        "mixer_norm_g": gain((DEPTH, D)),
        "attn_w_qkv": nrm((N_ATTN, D, 3 * D), D ** -0.5),
        "attn_lq1": nrm((N_ATTN, ATTN_HEAD_DIM), 0.1),
        "attn_lk1": nrm((N_ATTN, ATTN_HEAD_DIM), 0.1),
        "attn_lq2": nrm((N_ATTN, ATTN_HEAD_DIM), 0.1),
        "attn_lk2": nrm((N_ATTN, ATTN_HEAD_DIM), 0.1),
        "attn_subln_g": gain((N_ATTN, 2 * ATTN_HEAD_DIM)),
        "attn_w_o": nrm((N_ATTN, D, D), D ** -0.5),
        "conv_w_in": nrm((N_CONV, D, 2 * D), D ** -0.5),
        "conv_b_in": nrm((N_CONV, 2 * D), 0.02),
        "conv_dw_w": nrm((N_CONV, CONV_WIDTH, D), CONV_WIDTH ** -0.5),
        "conv_dw_b": nrm((N_CONV, D), 0.02),
        "conv_ln_g": gain((N_CONV, D)),
        "conv_ln_b": nrm((N_CONV, D), 0.02),
        "conv_w_out": nrm((N_CONV, D, D), D ** -0.5),
        "conv_b_out": nrm((N_CONV, D), 0.02),
        "lru_w_in": nrm((N_LRU, D, 2 * LRU_WIDTH), D ** -0.5),
        "lru_conv_w": nrm((N_LRU, LRU_CONV_WIDTH, LRU_WIDTH), LRU_CONV_WIDTH ** -0.5),
        "lru_conv_b": nrm((N_LRU, LRU_WIDTH), 0.02),
        "lru_w_a": nrm((N_LRU, LRU_BLOCKS, LRU_BLOCK_SIZE, LRU_BLOCK_SIZE), LRU_BLOCK_SIZE ** -0.5),
        "lru_b_a": nrm((N_LRU, LRU_WIDTH), 0.02),
        "lru_w_i": nrm((N_LRU, LRU_BLOCKS, LRU_BLOCK_SIZE, LRU_BLOCK_SIZE), LRU_BLOCK_SIZE ** -0.5),
        "lru_b_i": nrm((N_LRU, LRU_WIDTH), 0.02),
        "lru_lambda": jnp.log(a0) - jnp.log1p(-a0),
        "lru_w_out": nrm((N_LRU, LRU_WIDTH, D), LRU_WIDTH ** -0.5),
        "mlp_norm_g": gain((DEPTH, D)),
        "mlp_w1": nrm((DEPTH, D, D_FF), D ** -0.5),
        "mlp_w2": nrm((DEPTH, D_FF, D), D_FF ** -0.5),
        "final_norm_g": gain((D,)),
    }


def reference(x, rel_bias, mixer_norm_g, attn_w_qkv, attn_lq1, attn_lk1, attn_lq2, attn_lk2,
              attn_subln_g, attn_w_o, conv_w_in, conv_b_in, conv_dw_w, conv_dw_b, conv_ln_g,
              conv_ln_b, conv_w_out, conv_b_out, lru_w_in, lru_conv_w, lru_conv_b, lru_w_a,
              lru_b_a, lru_w_i, lru_b_i, lru_lambda, lru_w_out, mlp_norm_g, mlp_w1, mlp_w2,
              final_norm_g):
    h = x
    ia = ic = il = 0
    for layer in range(DEPTH):
        kind = layer % N_MIXERS
        n = rms_norm(h, mixer_norm_g[layer])
        if kind == 0:
            lambda_init = 0.8 - 0.6 * math.exp(-0.3 * layer)
            mix = diff_attention(n, attn_w_qkv[ia], attn_lq1[ia], attn_lk1[ia], attn_lq2[ia],
                                 attn_lk2[ia], attn_subln_g[ia], attn_w_o[ia], rel_bias,
                                 lambda_init)
            ia += 1
        elif kind == 1:
            mix = conformer_conv(n, conv_w_in[ic], conv_b_in[ic], conv_dw_w[ic], conv_dw_b[ic],
                                 conv_ln_g[ic], conv_ln_b[ic], conv_w_out[ic], conv_b_out[ic])
            ic += 1
        else:
            mix = rglru_block(n, lru_w_in[il], lru_conv_w[il], lru_conv_b[il], lru_w_a[il],
                              lru_b_a[il], lru_w_i[il], lru_b_i[il], lru_lambda[il],
                              lru_w_out[il])
            il += 1
        h = h + mix
        h = h + sq_relu_mlp(rms_norm(h, mlp_norm_g[layer]), mlp_w1[layer], mlp_w2[layer])
    return rms_norm(h, final_norm_g)
```

```python
import functools
import math

import numpy as np
import jax
import jax.numpy as jnp
from jax import lax
from jax.experimental import pallas as pl
from jax.experimental.pallas import tpu as pltpu

F32 = jnp.float32
BF16 = jnp.bfloat16

D_MODEL = 2048
DEPTH = 4
N_MIXERS = 3
HEAD_DIM = 128
HEAD_W = 2 * HEAD_DIM
N_HEADS = D_MODEL // HEAD_W
NUM_BUCKETS = 32
MAX_DISTANCE = 128
MASK_VALUE = -1e30
CONV_WIDTH = 31
LRU_BLOCKS = 8
LRU_BLOCK_SIZE = D_MODEL // LRU_BLOCKS
LRU_CONV_WIDTH = 4
LRU_C = 8.0
D_FF = 4 * D_MODEL

VMEM_LIMIT = 56 * 1024 * 1024
NORM_ROWS = 128


def _params(*sem):
    return pltpu.CompilerParams(dimension_semantics=sem, vmem_limit_bytes=VMEM_LIMIT)


def _rms_rows_to(dst_ref, x_ref, g_ref, rows, eps):
    def body(c, carry):
        r0 = pl.multiple_of(c * NORM_ROWS, NORM_ROWS)
        x = x_ref[pl.ds(r0, NORM_ROWS), :]
        ms = jnp.mean(x * x, axis=-1, keepdims=True)
        dst_ref[pl.ds(r0, NORM_ROWS), :] = (x * lax.rsqrt(ms + eps) * g_ref[...]).astype(BF16)
        return carry
    lax.fori_loop(0, rows // NORM_ROWS, body, 0)


def _sigmoid(x):
    return 1.0 / (1.0 + jnp.exp(-x))


def _gelu_tanh(x):
    c = math.sqrt(2.0 / math.pi)
    return 0.5 * x * (1.0 + jnp.tanh(c * (x + 0.044715 * (x * x * x))))


def _norm_proj_kernel(x_ref, g_ref, w_ref, s_ref, o_ref, xn_ref, *, tm):
    @pl.when(pl.program_id(1) == 0)
    def _():
        _rms_rows_to(xn_ref, x_ref, g_ref, tm, 1e-6)
    acc = jnp.dot(xn_ref[...], w_ref[...], preferred_element_type=F32)
    o_ref[...] = (acc * s_ref[...]).astype(o_ref.dtype)


def _norm_proj(x, g, w, col_scale, *, tm=1024, tn=512):
    m, k = x.shape
    n = w.shape[1]
    return pl.pallas_call(
        functools.partial(_norm_proj_kernel, tm=tm),
        grid=(m // tm, n // tn),
        in_specs=[
            pl.BlockSpec((tm, k), lambda i, j: (i, 0)),
            pl.BlockSpec((1, k), lambda i, j: (0, 0)),
            pl.BlockSpec((k, tn), lambda i, j: (0, j)),
            pl.BlockSpec((1, tn), lambda i, j: (0, j)),
        ],
        out_specs=pl.BlockSpec((tm, tn), lambda i, j: (i, j)),
        out_shape=jax.ShapeDtypeStruct((m, n), BF16),
        scratch_shapes=[pltpu.VMEM((tm, k), BF16)],
        compiler_params=_params("parallel", "arbitrary"),
        name="norm_proj",
    )(x, g, w, col_scale)


def _norm_glu_kernel(x_ref, g_ref, wa_ref, wg_ref, ba_ref, bg_ref, o_ref, xn_ref, *, tm):
    @pl.when(pl.program_id(1) == 0)
    def _():
        _rms_rows_to(xn_ref, x_ref, g_ref, tm, 1e-6)
    xn = xn_ref[...]
    a = jnp.dot(xn, wa_ref[...], preferred_element_type=F32) + ba_ref[...]
    gate = jnp.dot(xn, wg_ref[...], preferred_element_type=F32) + bg_ref[...]
    o_ref[...] = a * _sigmoid(gate)


def _norm_glu(x, g, w, b, *, tm=1024, tn=256):
    m, k = x.shape
    half = w.shape[1] // 2
    nj = half // tn
    return pl.pallas_call(
        functools.partial(_norm_glu_kernel, tm=tm),
        grid=(m // tm, nj),
        in_specs=[
            pl.BlockSpec((tm, k), lambda i, j: (i, 0)),
            pl.BlockSpec((1, k), lambda i, j: (0, 0)),
            pl.BlockSpec((k, tn), lambda i, j: (0, j)),
            pl.BlockSpec((k, tn), lambda i, j: (0, j + nj)),
            pl.BlockSpec((1, tn), lambda i, j: (0, j)),
            pl.BlockSpec((1, tn), lambda i, j: (0, j + nj)),
        ],
        out_specs=pl.BlockSpec((tm, tn), lambda i, j: (i, j)),
        out_shape=jax.ShapeDtypeStruct((m, half), F32),
        scratch_shapes=[pltpu.VMEM((tm, k), BF16)],
        compiler_params=_params("parallel", "arbitrary"),
        name="norm_glu",
    )(x, g, w, w, b, b)


def _norm_lru_in_kernel(x_ref, g_ref, wu_ref, wy_ref, u_ref, y_ref, xn_ref, *, tm):
    @pl.when(pl.program_id(1) == 0)
    def _():
        _rms_rows_to(xn_ref, x_ref, g_ref, tm, 1e-6)
    xn = xn_ref[...]
    u_ref[...] = jnp.dot(xn, wu_ref[...], preferred_element_type=F32)
    y_ref[...] = _gelu_tanh(jnp.dot(xn, wy_ref[...], preferred_element_type=F32))


def _norm_lru_in(x, g, w, *, tm=1024, tn=256):
    m, k = x.shape
    half = w.shape[1] // 2
    nj = half // tn
    return pl.pallas_call(
        functools.partial(_norm_lru_in_kernel, tm=tm),
        grid=(m // tm, nj),
        in_specs=[
            pl.BlockSpec((tm, k), lambda i, j: (i, 0)),
            pl.BlockSpec((1, k), lambda i, j: (0, 0)),
            pl.BlockSpec((k, tn), lambda i, j: (0, j)),
            pl.BlockSpec((k, tn), lambda i, j: (0, j + nj)),
        ],
        out_specs=[pl.BlockSpec((tm, tn), lambda i, j: (i, j)),
                   pl.BlockSpec((tm, tn), lambda i, j: (i, j))],
        out_shape=[jax.ShapeDtypeStruct((m, half), F32), jax.ShapeDtypeStruct((m, half), F32)],
        scratch_shapes=[pltpu.VMEM((tm, k), BF16)],
        compiler_params=_params("parallel", "arbitrary"),
        name="norm_lru_in",
    )(x, g, w, w)


def _proj_res_kernel(a_ref, w_ref, b_ref, r_ref, o_ref):
    acc = jnp.dot(a_ref[...], w_ref[...], preferred_element_type=F32)
    o_ref[...] = r_ref[...] + (acc + b_ref[...])


def _proj_res(a, w, b, res, *, tm=1024, tn=512):
    m, k = a.shape
    n = w.shape[1]
    return pl.pallas_call(
        _proj_res_kernel,
        grid=(m // tm, n // tn),
        in_specs=[
            pl.BlockSpec((tm, k), lambda i, j: (i, 0)),
            pl.BlockSpec((k, tn), lambda i, j: (0, j)),
            pl.BlockSpec((1, tn), lambda i, j: (0, j)),
            pl.BlockSpec((tm, tn), lambda i, j: (i, j)),
        ],
        out_specs=pl.BlockSpec((tm, tn), lambda i, j: (i, j)),
        out_shape=jax.ShapeDtypeStruct((m, n), F32),
        compiler_params=_params("parallel", "arbitrary"),
        name="proj_res",
    )(a, w, b, res)


def _mlp_kernel(x_ref, g_ref, w1_ref, w2_ref, o_ref, xn_ref, *, tm):
    f = pl.program_id(1)

    @pl.when(f == 0)
    def _():
        _rms_rows_to(xn_ref, x_ref, g_ref, tm, 1e-6)

    z = jnp.maximum(jnp.dot(xn_ref[...], w1_ref[...], preferred_element_type=F32), 0.0)
    part = jnp.dot((z * z).astype(BF16), w2_ref[...], preferred_element_type=F32)

    @pl.when(f == 0)
    def _():
        o_ref[...] = x_ref[...] + part

    @pl.when(f != 0)
    def _():
        o_ref[...] += part


def _mlp(x, g, w1, w2, *, tm=512, tf=1024):
    m, d = x.shape
    ff = w1.shape[1]
    return pl.pallas_call(
        functools.partial(_mlp_kernel, tm=tm),
        grid=(m // tm, ff // tf),
        in_specs=[
            pl.BlockSpec((tm, d), lambda i, f: (i, 0)),
            pl.BlockSpec((1, d), lambda i, f: (0, 0)),
            pl.BlockSpec((d, tf), lambda i, f: (0, f)),
            pl.BlockSpec((tf, d), lambda i, f: (f, 0)),
        ],
        out_specs=pl.BlockSpec((tm, d), lambda i, f: (i, 0)),
        out_shape=jax.ShapeDtypeStruct((m, d), F32),
        scratch_shapes=[pltpu.VMEM((tm, d), BF16)],
        compiler_params=_params("parallel", "arbitrary"),
        name="mlp",
    )(x, g, w1, w2)


def _final_norm_kernel(x_ref, g_ref, o_ref):
    x = x_ref[...]
    ms = jnp.mean(x * x, axis=-1, keepdims=True)
    o_ref[...] = x * lax.rsqrt(ms + 1e-6) * g_ref[...]


def _final_norm(x, g, *, tm=256):
    m, d = x.shape
    return pl.pallas_call(
        _final_norm_kernel,
        grid=(m // tm,),
        in_specs=[pl.BlockSpec((tm, d), lambda i: (i, 0)), pl.BlockSpec((1, d), lambda i: (0, 0))],
        out_specs=pl.BlockSpec((tm, d), lambda i: (i, 0)),
        out_shape=jax.ShapeDtypeStruct((m, d), F32),
        compiler_params=_params("parallel"),
        name="final_norm",
    )(x, g)


ATTN_T = 512


def _bucket_tiles(t):
    r = np.arange(t, dtype=np.int32)[:, None]
    c = np.arange(t, dtype=np.int32)[None, :]
    tiles = []
    for d in range(2):
        n = np.maximum(d * t + r - c, 0)
        max_exact = NUM_BUCKETS // 2
        nf = np.maximum(n, 1).astype(np.float32)
        large = max_exact + (np.log(nf / np.float32(max_exact)) / np.float32(math.log(MAX_DISTANCE / max_exact))
                             * np.float32(NUM_BUCKETS - max_exact)).astype(np.int32)
        large = np.minimum(large, NUM_BUCKETS - 1)
        tiles.append(np.where(n < max_exact, n, large).astype(np.int32))
    return np.stack(tiles)


def _bias_tiles_kernel(rb_ref, bucket_ref, o_ref):
    h = pl.program_id(0)
    bucket = bucket_ref[0]
    acc = jnp.zeros(bucket.shape, F32)
    for j in range(NUM_BUCKETS):
        acc = jnp.where(bucket == j, rb_ref[j, h], acc)
    o_ref[0, 0] = acc


def _bias_tiles(rel_bias, t):
    buckets = jnp.asarray(_bucket_tiles(t))
    return pl.pallas_call(
        _bias_tiles_kernel,
        grid=(N_HEADS, 2),
        in_specs=[
            pl.BlockSpec(memory_space=pltpu.SMEM),
            pl.BlockSpec((1, t, t), lambda h, d: (d, 0, 0)),
        ],
        out_specs=pl.BlockSpec((1, 1, t, t), lambda h, d: (h, d, 0, 0)),
        out_shape=jax.ShapeDtypeStruct((N_HEADS, 2, t, t), F32),
        compiler_params=_params("parallel", "parallel"),
        name="bias_tiles",
    )(rel_bias, buckets)


def _attn_kernel(rb_ref, lq1_ref, lk1_ref, lq2_ref, lk2_ref, sg_ref, q_ref, k_ref, v_ref, bias_ref,
                 o_ref, m_ref, l_ref, acc_ref, *, t, lambda_init):
    h = pl.program_id(1)
    qi = pl.program_id(2)

    m_ref[...] = jnp.full(m_ref.shape, MASK_VALUE, F32)
    l_ref[...] = jnp.zeros(l_ref.shape, F32)
    acc_ref[...] = jnp.zeros(acc_ref.shape, F32)

    def block(k0, bias, causal):
        v = v_ref[pl.ds(k0, t), :]
        for c in range(2):
            lanes = slice(c * HEAD_DIM, (c + 1) * HEAD_DIM)
            s = lax.dot_general(q_ref[:, lanes], k_ref[pl.ds(k0, t), lanes],
                                (((1,), (1,)), ((), ())), preferred_element_type=F32)
            s = s + bias
            if causal:
                row = lax.broadcasted_iota(jnp.int32, (t, t), 0)
                col = lax.broadcasted_iota(jnp.int32, (t, t), 1)
                s = jnp.where(row >= col, s, MASK_VALUE)
            m_prev = m_ref[c]
            m_new = jnp.maximum(m_prev, jnp.max(s, axis=1, keepdims=True))
            alpha = jnp.exp(m_prev - m_new)
            p = jnp.exp(s - m_new)
            l_ref[c] = alpha * l_ref[c] + jnp.sum(p, axis=1, keepdims=True)
            acc_ref[c] = alpha * acc_ref[c] + jnp.dot(p.astype(BF16), v, preferred_element_type=F32)
            m_ref[c] = m_new

    far_bias = rb_ref[NUM_BUCKETS - 1, h]

    def far_body(ki, carry):
        block(pl.multiple_of(ki * t, t), far_bias, False)
        return carry
    lax.fori_loop(0, jnp.maximum(qi - 1, 0), far_body, 0)

    @pl.when(qi >= 1)
    def _():
        block(pl.multiple_of((qi - 1) * t, t), bias_ref[0, 1], False)

    block(pl.multiple_of(qi * t, t), bias_ref[0, 0], True)

    lam = (jnp.exp(jnp.sum(lq1_ref[...] * lk1_ref[...], keepdims=True))
           - jnp.exp(jnp.sum(lq2_ref[...] * lk2_ref[...], keepdims=True)) + lambda_init)
    o = acc_ref[0] / l_ref[0] - lam * (acc_ref[1] / l_ref[1])
    ms = jnp.mean(o * o, axis=-1, keepdims=True)
    o = o * lax.rsqrt(ms + 1e-5) * sg_ref[...] * (1.0 - lambda_init)
    o_ref[...] = o.astype(o_ref.dtype)


def _diff_attention(qkv, bias_tiles, rel_bias, lq1, lk1, lq2, lk2, subln_g, *, batch, seq, lambda_init):
    t = ATTN_T
    nq = seq // t
    lam_spec = pl.BlockSpec((1, HEAD_DIM), lambda b, h, i: (0, 0))
    return pl.pallas_call(
        functools.partial(_attn_kernel, t=t, lambda_init=lambda_init),
        grid=(batch, N_HEADS, nq),
        in_specs=[
            pl.BlockSpec(memory_space=pltpu.SMEM),
            lam_spec, lam_spec, lam_spec, lam_spec,
            pl.BlockSpec((1, HEAD_W), lambda b, h, i: (0, 0)),
            pl.BlockSpec((t, HEAD_W), lambda b, h, i: (b * nq + i, h)),
            pl.BlockSpec((seq, HEAD_W), lambda b, h, i: (b, N_HEADS + h)),
            pl.BlockSpec((seq, HEAD_W), lambda b, h, i: (b, 2 * N_HEADS + h)),
            pl.BlockSpec((1, 2, t, t), lambda b, h, i: (h, 0, 0, 0)),
        ],
        out_specs=pl.BlockSpec((t, HEAD_W), lambda b, h, i: (b * nq + i, h)),
        out_shape=jax.ShapeDtypeStruct((batch * seq, D_MODEL), BF16),
        scratch_shapes=[pltpu.VMEM((2, t, 1), F32), pltpu.VMEM((2, t, 1), F32),
                        pltpu.VMEM((2, t, HEAD_W), F32)],
        compiler_params=_params("parallel", "parallel", "arbitrary"),
        name="diff_attn",
    )(rel_bias, lq1, lk1, lq2, lk2, subln_g, qkv, qkv, qkv, bias_tiles)


CONV_T = 256
CONV_HALO = 32
CONV_ROWS = 64
CONV_LANES = 256


def _conv_ln_kernel(halo_ref, u_ref, w_ref, b_ref, g_ref, beta_ref, o_ref, buf_ref, y_ref, *, tiles_per_seq):
    i = pl.program_id(0)
    first = (i % tiles_per_seq) == 0

    @pl.when(first)
    def _():
        buf_ref[0:CONV_HALO, :] = jnp.zeros((CONV_HALO, D_MODEL), F32)

    @pl.when(jnp.logical_not(first))
    def _():
        buf_ref[0:CONV_HALO, :] = halo_ref[...]

    buf_ref[CONV_HALO:, :] = u_ref[...]
    base = CONV_HALO - (CONV_WIDTH - 1)
    for r0 in range(0, CONV_T, CONV_ROWS):
        for c0 in range(0, D_MODEL, CONV_LANES):
            lanes = slice(c0, c0 + CONV_LANES)
            acc = jnp.zeros((CONV_ROWS, CONV_LANES), F32)
            for j in range(CONV_WIDTH):
                acc = acc + w_ref[j:j + 1, lanes] * buf_ref[base + r0 + j:base + r0 + j + CONV_ROWS, lanes]
            y_ref[r0:r0 + CONV_ROWS, lanes] = acc + b_ref[:, lanes]

    def ln_body(c, carry):
        r0 = pl.multiple_of(c * CONV_ROWS, CONV_ROWS)
        y = y_ref[pl.ds(r0, CONV_ROWS), :]
        mu = jnp.mean(y, axis=-1, keepdims=True)
        yc = y - mu
        var = jnp.mean(yc * yc, axis=-1, keepdims=True)
        z = yc * lax.rsqrt(var + 1e-5) * g_ref[...] + beta_ref[...]
        o_ref[pl.ds(r0, CONV_ROWS), :] = (z * _sigmoid(z)).astype(o_ref.dtype)
        return carry
    lax.fori_loop(0, CONV_T // CONV_ROWS, ln_body, 0)


def _conv_ln(u, dw_w, dw_b, ln_g, ln_b, *, seq):
    m, d = u.shape
    tiles_per_seq = seq // CONV_T
    ratio = CONV_T // CONV_HALO
    row = lambda i: (0, 0)
    return pl.pallas_call(
        functools.partial(_conv_ln_kernel, tiles_per_seq=tiles_per_seq),
        grid=(m // CONV_T,),
        in_specs=[
            pl.BlockSpec((CONV_HALO, d), lambda i: (jnp.maximum(i * ratio - 1, 0), 0)),
            pl.BlockSpec((CONV_T, d), lambda i: (i, 0)),
            pl.BlockSpec((CONV_WIDTH, d), row),
            pl.BlockSpec((1, d), row), pl.BlockSpec((1, d), row), pl.BlockSpec((1, d), row),
        ],
        out_specs=pl.BlockSpec((CONV_T, d), lambda i: (i, 0)),
        out_shape=jax.ShapeDtypeStruct((m, d), BF16),
        scratch_shapes=[pltpu.VMEM((CONV_HALO + CONV_T, d), F32), pltpu.VMEM((CONV_T, d), F32)],
        compiler_params=_params("parallel"),
        name="conv_ln",
    )(u, u, dw_w, dw_b, ln_g, ln_b)


LRU_T = 256
LRU_HALO = 8


def _lru_kernel(halo_ref, u_ref, y_ref, cw_ref, cb_ref, wa_ref, ba_ref, wi_ref, bi_ref, lam_ref,
                o_ref, buf_ref, carry_ref):
    ti = pl.program_id(2)
    first = ti == 0
    buf_ref[LRU_HALO:, :] = u_ref[...]

    @pl.when(first)
    def _():
        buf_ref[0:LRU_HALO, :] = jnp.zeros((LRU_HALO, LRU_BLOCK_SIZE), F32)
        carry_ref[...] = jnp.zeros(carry_ref.shape, F32)

    @pl.when(jnp.logical_not(first))
    def _():
        buf_ref[0:LRU_HALO, :] = halo_ref[...]

    base = LRU_HALO - (LRU_CONV_WIDTH - 1)
    uc = jnp.zeros((LRU_T, LRU_BLOCK_SIZE), F32)
    for j in range(LRU_CONV_WIDTH):
        uc = uc + cw_ref[j:j + 1, :] * buf_ref[base + j:base + j + LRU_T, :]
    uc = uc + cb_ref[...]

    ucb = uc.astype(BF16)
    r = _sigmoid(jnp.dot(ucb, wa_ref[0], preferred_element_type=F32) + ba_ref[...])
    gate_i = _sigmoid(jnp.dot(ucb, wi_ref[0], preferred_element_type=F32) + bi_ref[...])
    neg_lam = -lam_ref[...]
    softplus = jnp.maximum(neg_lam, 0.0) + jnp.log1p(jnp.exp(-jnp.abs(neg_lam)))
    log_a = -LRU_C * r * softplus
    a = jnp.exp(log_a)
    mult = jnp.sqrt(-jnp.tanh(log_a) * (1.0 + a * a))
    b = mult * (gate_i * uc)

    row = lax.broadcasted_iota(jnp.int32, (LRU_T, LRU_BLOCK_SIZE), 0)
    s = 1
    while s < LRU_T:
        valid = row >= s
        a_sh = jnp.where(valid, pltpu.roll(a, s, 0), 1.0)
        b_sh = jnp.where(valid, pltpu.roll(b, s, 0), 0.0)
        b = a * b_sh + b
        a = a * a_sh
        s *= 2
    hs = a * carry_ref[0:1, :] + b
    carry_ref[0:1, :] = hs[LRU_T - 1:LRU_T, :]
    o_ref[...] = (hs * y_ref[...]).astype(o_ref.dtype)


def _lru_core(u, y, conv_w, conv_b, w_a, b_a, w_i, b_i, lam, *, batch, seq):
    m, d = u.shape
    nt = seq // LRU_T
    ratio = LRU_T // LRU_HALO
    c = LRU_BLOCK_SIZE
    chan = lambda b, g, t: (0, g)
    tile = lambda b, g, t: (b * nt + t, g)
    gate_w = pl.BlockSpec((1, c, c), lambda b, g, t: (g, 0, 0))
    return pl.pallas_call(
        _lru_kernel,
        grid=(batch, LRU_BLOCKS, nt),
        in_specs=[
            pl.BlockSpec((LRU_HALO, c), lambda b, g, t: (jnp.maximum((b * nt + t) * ratio - 1, 0), g)),
            pl.BlockSpec((LRU_T, c), tile),
            pl.BlockSpec((LRU_T, c), tile),
            pl.BlockSpec((LRU_CONV_WIDTH, c), chan),
            pl.BlockSpec((1, c), chan),
            gate_w, pl.BlockSpec((1, c), chan),
            gate_w, pl.BlockSpec((1, c), chan),
            pl.BlockSpec((1, c), chan),
        ],
        out_specs=pl.BlockSpec((LRU_T, c), tile),
        out_shape=jax.ShapeDtypeStruct((m, d), BF16),
        scratch_shapes=[pltpu.VMEM((LRU_HALO + LRU_T, c), F32), pltpu.VMEM((8, c), F32)],
        compiler_params=_params("parallel", "parallel", "arbitrary"),
        name="lru_core",
    )(u, u, y, conv_w, conv_b, w_a, b_a, w_i, b_i, lam)


def kernel(x, rel_bias, mixer_norm_g, attn_w_qkv, attn_lq1, attn_lk1, attn_lq2, attn_lk2, attn_subln_g, attn_w_o, conv_w_in, conv_b_in, conv_dw_w, conv_dw_b, conv_ln_g, conv_ln_b, conv_w_out, conv_b_out, lru_w_in, lru_conv_w, lru_conv_b, lru_w_a, lru_b_a, lru_w_i, lru_b_i, lru_lambda, lru_w_out, mlp_norm_g, mlp_w1, mlp_w2, final_norm_g):
    batch, seq, d = x.shape
    h = x.reshape(batch * seq, d)
    row = lambda v: v.reshape(1, -1)
    zero_bias = jnp.zeros((1, d), F32)
    q_scale = jnp.concatenate([jnp.full((1, d), HEAD_DIM ** -0.5, F32), jnp.ones((1, 2 * d), F32)], axis=1)
    bias_tiles = _bias_tiles(rel_bias, ATTN_T)

    ia = ic = il = 0
    for layer in range(DEPTH):
        kind = layer % N_MIXERS
        g = row(mixer_norm_g[layer])
        if kind == 0:
            lambda_init = 0.8 - 0.6 * math.exp(-0.3 * layer)
            qkv = _norm_proj(h, g, attn_w_qkv[ia].astype(BF16), q_scale)
            att = _diff_attention(qkv, bias_tiles, rel_bias, row(attn_lq1[ia]), row(attn_lk1[ia]),
                                  row(attn_lq2[ia]), row(attn_lk2[ia]), row(attn_subln_g[ia]),
                                  batch=batch, seq=seq, lambda_init=lambda_init)
            h = _proj_res(att, attn_w_o[ia].astype(BF16), zero_bias, h)
            ia += 1
        elif kind == 1:
            u = _norm_glu(h, g, conv_w_in[ic].astype(BF16), row(conv_b_in[ic]))
            act = _conv_ln(u, conv_dw_w[ic], row(conv_dw_b[ic]), row(conv_ln_g[ic]), row(conv_ln_b[ic]), seq=seq)
            h = _proj_res(act, conv_w_out[ic].astype(BF16), row(conv_b_out[ic]), h)
            ic += 1
        else:
            u, y = _norm_lru_in(h, g, lru_w_in[il].astype(BF16))
            hy = _lru_core(u, y, lru_conv_w[il], row(lru_conv_b[il]), lru_w_a[il].astype(BF16), row(lru_b_a[il]),
                           lru_w_i[il].astype(BF16), row(lru_b_i[il]), row(lru_lambda[il]), batch=batch, seq=seq)
            h = _proj_res(hy, lru_w_out[il].astype(BF16), zero_bias, h)
            il += 1
        h = _mlp(h, row(mlp_norm_g[layer]), mlp_w1[layer].astype(BF16), mlp_w2[layer].astype(BF16))
    return _final_norm(h, row(final_norm_g)).reshape(batch, seq, d)
```

```python
import functools
import math

import numpy as np
import jax
import jax.numpy as jnp
from jax import lax
from jax.experimental import pallas as pl
from jax.experimental.pallas import tpu as pltpu

F32 = jnp.float32
BF16 = jnp.bfloat16

D_MODEL = 2048
DEPTH = 4
N_MIXERS = 3
HEAD_DIM = 128
HEAD_W = 2 * HEAD_DIM
N_HEADS = D_MODEL // HEAD_W
NUM_BUCKETS = 32
MAX_DISTANCE = 128
MASK_VALUE = -1e30
CONV_WIDTH = 31
LRU_BLOCKS = 8
LRU_BLOCK_SIZE = D_MODEL // LRU_BLOCKS
LRU_CONV_WIDTH = 4
LRU_C = 8.0
D_FF = 4 * D_MODEL

VMEM_LIMIT = 56 * 1024 * 1024
NORM_ROWS = 128


def _params(*sem):
    return pltpu.CompilerParams(dimension_semantics=sem, vmem_limit_bytes=VMEM_LIMIT)


def _rms_rows_to(dst_ref, x_ref, g_ref, rows, eps):
    def body(c, carry):
        r0 = pl.multiple_of(c * NORM_ROWS, NORM_ROWS)
        x = x_ref[pl.ds(r0, NORM_ROWS), :]
        ms = jnp.mean(x * x, axis=-1, keepdims=True)
        dst_ref[pl.ds(r0, NORM_ROWS), :] = (x * lax.rsqrt(ms + eps) * g_ref[...]).astype(BF16)
        return carry
    lax.fori_loop(0, rows // NORM_ROWS, body, 0)


def _sigmoid(x):
    return 1.0 / (1.0 + jnp.exp(-x))


def _gelu_tanh(x):
    c = math.sqrt(2.0 / math.pi)
    return 0.5 * x * (1.0 + jnp.tanh(c * (x + 0.044715 * (x * x * x))))


def _norm_proj_kernel(x_ref, g_ref, w_ref, s_ref, o_ref, xn_ref, *, tm):
    @pl.when(pl.program_id(1) == 0)
    def _():
        _rms_rows_to(xn_ref, x_ref, g_ref, tm, 1e-6)
    acc = jnp.dot(xn_ref[...], w_ref[...], preferred_element_type=F32)
    o_ref[...] = (acc * s_ref[...]).astype(o_ref.dtype)


def _norm_proj(x, g, w, col_scale, *, tm=1024, tn=512):
    m, k = x.shape
    n = w.shape[1]
    return pl.pallas_call(
        functools.partial(_norm_proj_kernel, tm=tm),
        grid=(m // tm, n // tn),
        in_specs=[
            pl.BlockSpec((tm, k), lambda i, j: (i, 0)),
            pl.BlockSpec((1, k), lambda i, j: (0, 0)),
            pl.BlockSpec((k, tn), lambda i, j: (0, j)),
            pl.BlockSpec((1, tn), lambda i, j: (0, j)),
        ],
        out_specs=pl.BlockSpec((tm, tn), lambda i, j: (i, j)),
        out_shape=jax.ShapeDtypeStruct((m, n), BF16),
        scratch_shapes=[pltpu.VMEM((tm, k), BF16)],
        compiler_params=_params("parallel", "arbitrary"),
        name="norm_proj",
    )(x, g, w, col_scale)


def _norm_glu_kernel(x_ref, g_ref, wa_ref, wg_ref, ba_ref, bg_ref, o_ref, xn_ref, *, tm):
    @pl.when(pl.program_id(1) == 0)
    def _():
        _rms_rows_to(xn_ref, x_ref, g_ref, tm, 1e-6)
    xn = xn_ref[...]
    a = jnp.dot(xn, wa_ref[...], preferred_element_type=F32) + ba_ref[...]
    gate = jnp.dot(xn, wg_ref[...], preferred_element_type=F32) + bg_ref[...]
    o_ref[...] = a * _sigmoid(gate)


def _norm_glu(x, g, w, b, *, tm=1024, tn=256):
    m, k = x.shape
    half = w.shape[1] // 2
    nj = half // tn
    return pl.pallas_call(
        functools.partial(_norm_glu_kernel, tm=tm),
        grid=(m // tm, nj),
        in_specs=[
            pl.BlockSpec((tm, k), lambda i, j: (i, 0)),
            pl.BlockSpec((1, k), lambda i, j: (0, 0)),
            pl.BlockSpec((k, tn), lambda i, j: (0, j)),
            pl.BlockSpec((k, tn), lambda i, j: (0, j + nj)),
            pl.BlockSpec((1, tn), lambda i, j: (0, j)),
            pl.BlockSpec((1, tn), lambda i, j: (0, j + nj)),
        ],
        out_specs=pl.BlockSpec((tm, tn), lambda i, j: (i, j)),
        out_shape=jax.ShapeDtypeStruct((m, half), F32),
        scratch_shapes=[pltpu.VMEM((tm, k), BF16)],
        compiler_params=_params("parallel", "arbitrary"),
        name="norm_glu",
    )(x, g, w, w, b, b)


def _norm_lru_in_kernel(x_ref, g_ref, wu_ref, wy_ref, u_ref, y_ref, xn_ref, *, tm):
    @pl.when(pl.program_id(1) == 0)
    def _():
        _rms_rows_to(xn_ref, x_ref, g_ref, tm, 1e-6)
    xn = xn_ref[...]
    u_ref[...] = jnp.dot(xn, wu_ref[...], preferred_element_type=F32)
    y_ref[...] = _gelu_tanh(jnp.dot(xn, wy_ref[...], preferred_element_type=F32))


def _norm_lru_in(x, g, w, *, tm=1024, tn=256):
    m, k = x.shape
    half = w.shape[1] // 2
    nj = half // tn
    return pl.pallas_call(
        functools.partial(_norm_lru_in_kernel, tm=tm),
        grid=(m // tm, nj),
        in_specs=[
            pl.BlockSpec((tm, k), lambda i, j: (i, 0)),
            pl.BlockSpec((1, k), lambda i, j: (0, 0)),
            pl.BlockSpec((k, tn), lambda i, j: (0, j)),
            pl.BlockSpec((k, tn), lambda i, j: (0, j + nj)),
        ],
        out_specs=[pl.BlockSpec((tm, tn), lambda i, j: (i, j)),
                   pl.BlockSpec((tm, tn), lambda i, j: (i, j))],
        out_shape=[jax.ShapeDtypeStruct((m, half), F32), jax.ShapeDtypeStruct((m, half), F32)],
        scratch_shapes=[pltpu.VMEM((tm, k), BF16)],
        compiler_params=_params("parallel", "arbitrary"),
        name="norm_lru_in",
    )(x, g, w, w)


def _proj_res_kernel(a_ref, w_ref, b_ref, r_ref, o_ref):
    acc = jnp.dot(a_ref[...], w_ref[...], preferred_element_type=F32)
    o_ref[...] = r_ref[...] + (acc + b_ref[...])


def _proj_res(a, w, b, res, *, tm=1024, tn=512):
    m, k = a.shape
    n = w.shape[1]
    return pl.pallas_call(
        _proj_res_kernel,
        grid=(m // tm, n // tn),
        in_specs=[
            pl.BlockSpec((tm, k), lambda i, j: (i, 0)),
            pl.BlockSpec((k, tn), lambda i, j: (0, j)),
            pl.BlockSpec((1, tn), lambda i, j: (0, j)),
            pl.BlockSpec((tm, tn), lambda i, j: (i, j)),
        ],
        out_specs=pl.BlockSpec((tm, tn), lambda i, j: (i, j)),
        out_shape=jax.ShapeDtypeStruct((m, n), F32),
        compiler_params=_params("parallel", "arbitrary"),
        name="proj_res",
    )(a, w, b, res)


def _mlp_kernel(x_ref, g_ref, w1_ref, w2_ref, o_ref, xn_ref, *, tm):
    f = pl.program_id(1)

    @pl.when(f == 0)
    def _():
        _rms_rows_to(xn_ref, x_ref, g_ref, tm, 1e-6)

    z = jnp.maximum(jnp.dot(xn_ref[...], w1_ref[...], preferred_element_type=F32), 0.0)
    part = jnp.dot((z * z).astype(BF16), w2_ref[...], preferred_element_type=F32)

    @pl.when(f == 0)
    def _():
        o_ref[...] = x_ref[...] + part

    @pl.when(f != 0)
    def _():
        o_ref[...] += part


def _mlp(x, g, w1, w2, *, tm=512, tf=1024):
    m, d = x.shape
    ff = w1.shape[1]
    return pl.pallas_call(
        functools.partial(_mlp_kernel, tm=tm),
        grid=(m // tm, ff // tf),
        in_specs=[
            pl.BlockSpec((tm, d), lambda i, f: (i, 0)),
            pl.BlockSpec((1, d), lambda i, f: (0, 0)),
            pl.BlockSpec((d, tf), lambda i, f: (0, f)),
            pl.BlockSpec((tf, d), lambda i, f: (f, 0)),
        ],
        out_specs=pl.BlockSpec((tm, d), lambda i, f: (i, 0)),
        out_shape=jax.ShapeDtypeStruct((m, d), F32),
        scratch_shapes=[pltpu.VMEM((tm, d), BF16)],
        compiler_params=_params("parallel", "arbitrary"),
        name="mlp",
    )(x, g, w1, w2)


def _final_norm_kernel(x_ref, g_ref, o_ref):
    x = x_ref[...]
    ms = jnp.mean(x * x, axis=-1, keepdims=True)
    o_ref[...] = x * lax.rsqrt(ms + 1e-6) * g_ref[...]


def _final_norm(x, g, *, tm=256):
    m, d = x.shape
    return pl.pallas_call(
        _final_norm_kernel,
        grid=(m // tm,),
        in_specs=[pl.BlockSpec((tm, d), lambda i: (i, 0)), pl.BlockSpec((1, d), lambda i: (0, 0))],
        out_specs=pl.BlockSpec((tm, d), lambda i: (i, 0)),
        out_shape=jax.ShapeDtypeStruct((m, d), F32),
        compiler_params=_params("parallel"),
        name="final_norm",
    )(x, g)


ATTN_T = 512
ATTN_STRIP = 32
ATTN_UNROLL = 4
LANES = 128
LOG2E = math.log2(math.e)


def _bucket_tiles(t):
    r = np.arange(t, dtype=np.int32)[:, None]
    c = np.arange(t, dtype=np.int32)[None, :]
    tiles = []
    for d in range(2):
        n = np.maximum(d * t + r - c, 0)
        max_exact = NUM_BUCKETS // 2
        nf = np.maximum(n, 1).astype(np.float32)
        large = max_exact + (np.log(nf / np.float32(max_exact)) / np.float32(math.log(MAX_DISTANCE / max_exact))
                             * np.float32(NUM_BUCKETS - max_exact)).astype(np.int32)
        large = np.minimum(large, NUM_BUCKETS - 1)
        bucket = np.where(n < max_exact, n, large).astype(np.int32)
        tiles.append(np.where(d * t + r - c >= 0, bucket, -1))
    return np.stack(tiles)


def _bias_tiles_kernel(rb_ref, bucket_ref, o_ref):
    h = pl.program_id(0)
    bucket = bucket_ref[0]
    acc = jnp.zeros(bucket.shape, F32)
    for j in range(NUM_BUCKETS):
        acc = jnp.where(bucket == j, rb_ref[j, h], acc)
    acc = (acc - rb_ref[NUM_BUCKETS - 1, h]) * LOG2E
    o_ref[0, 0] = jnp.where(bucket < 0, MASK_VALUE, acc)


def _bias_tiles(rel_bias, t):
    buckets = jnp.asarray(_bucket_tiles(t))
    return pl.pallas_call(
        _bias_tiles_kernel,
        grid=(N_HEADS, 2),
        in_specs=[
            pl.BlockSpec(memory_space=pltpu.SMEM),
            pl.BlockSpec((1, t, t), lambda h, d: (d, 0, 0)),
        ],
        out_specs=pl.BlockSpec((1, 1, t, t), lambda h, d: (h, d, 0, 0)),
        out_shape=jax.ShapeDtypeStruct((N_HEADS, 2, t, t), F32),
        compiler_params=_params("parallel", "parallel"),
        name="bias_tiles",
    )(rel_bias, buckets)


def _attn_kernel(lq1_ref, lk1_ref, lq2_ref, lk2_ref, sg_ref, q_ref, k_ref, v_ref, bias_ref,
                 o_ref, m_ref, l_ref, alpha_ref, part_ref, acc_ref, s_ref, p_ref, *, t, lambda_init):
    qi = pl.program_id(2)
    chunks = [slice(j * LANES, (j + 1) * LANES) for j in range(t // LANES)]

    m_ref[...] = jnp.full(m_ref.shape, MASK_VALUE, F32)
    l_ref[...] = jnp.zeros(l_ref.shape, F32)
    acc_ref[...] = jnp.zeros(acc_ref.shape, F32)

    def block(k0, bias_index):
        def logits(c, rows):
            s = s_ref[c, rows, :]
            if bias_index is not None:
                s = s + bias_ref[0, bias_index, rows, :]
            return s

        def strip_rows(r):
            return pl.ds(pl.multiple_of(r * ATTN_STRIP, ATTN_STRIP), ATTN_STRIP)

        for c in range(2):
            lanes = slice(c * HEAD_DIM, (c + 1) * HEAD_DIM)
            s_ref[c] = lax.dot_general(q_ref[:, lanes], k_ref[pl.ds(k0, t), lanes],
                                       (((1,), (1,)), ((), ())), preferred_element_type=F32)

        def max_strip(r, carry):
            rows = strip_rows(r)
            for c in range(2):
                s = logits(c, rows)
                part = s[:, chunks[0]]
                for ch in chunks[1:]:
                    part = jnp.maximum(part, s[:, ch])
                part_ref[c, rows, :] = part
            return carry
        lax.fori_loop(0, t // ATTN_STRIP, max_strip, 0, unroll=ATTN_UNROLL)

        for c in range(2):
            m_prev = m_ref[c]
            m_new = jnp.maximum(m_prev, jnp.max(part_ref[c], axis=1, keepdims=True))
            alpha_ref[c] = jnp.exp2(m_prev - m_new)
            m_ref[c] = m_new

        def exp_strip(r, carry):
            rows = strip_rows(r)
            for c in range(2):
                s = logits(c, rows)
                m_new = m_ref[c, rows, :]
                part = None
                for ch in chunks:
                    p = jnp.exp2(s[:, ch] - m_new)
                    part = p if part is None else part + p
                    p_ref[c, rows, ch] = p.astype(BF16)
                part_ref[c, rows, :] = part
            return carry
        lax.fori_loop(0, t // ATTN_STRIP, exp_strip, 0, unroll=ATTN_UNROLL)

        v = v_ref[pl.ds(k0, t), :]
        for c in range(2):
            alpha = alpha_ref[c]
            l_ref[c] = alpha * l_ref[c] + jnp.sum(part_ref[c], axis=1, keepdims=True)
            pv = jnp.dot(p_ref[c], v, preferred_element_type=F32)
            acc_ref[c] = jnp.concatenate([alpha, alpha], axis=1) * acc_ref[c] + pv

    def far_body(ki, carry):
        block(pl.multiple_of(ki * t, t), None)
        return carry
    lax.fori_loop(0, jnp.maximum(qi - 1, 0), far_body, 0)

    @pl.when(qi >= 1)
    def _():
        block(pl.multiple_of((qi - 1) * t, t), 1)

    block(pl.multiple_of(qi * t, t), 0)

    lam = (jnp.exp(jnp.sum(lq1_ref[...] * lk1_ref[...], keepdims=True))
           - jnp.exp(jnp.sum(lq2_ref[...] * lk2_ref[...], keepdims=True)) + lambda_init)
    l0 = l_ref[0]
    l1 = l_ref[1]
    o = (acc_ref[0] / jnp.concatenate([l0, l0], axis=1)
         - lam * (acc_ref[1] / jnp.concatenate([l1, l1], axis=1)))
    ms = jnp.mean(o * o, axis=-1, keepdims=True)
    o = o * lax.rsqrt(ms + 1e-5) * sg_ref[...] * (1.0 - lambda_init)
    o_ref[...] = o.astype(o_ref.dtype)


def _diff_attention(qkv, bias_tiles, lq1, lk1, lq2, lk2, subln_g, *, batch, seq, lambda_init):
    t = ATTN_T
    nq = seq // t
    lam_spec = pl.BlockSpec((1, HEAD_DIM), lambda b, h, i: (0, 0))
    return pl.pallas_call(
        functools.partial(_attn_kernel, t=t, lambda_init=lambda_init),
        grid=(batch, N_HEADS, nq),
        in_specs=[
            lam_spec, lam_spec, lam_spec, lam_spec,
            pl.BlockSpec((1, HEAD_W), lambda b, h, i: (0, 0)),
            pl.BlockSpec((t, HEAD_W), lambda b, h, i: (b * nq + i, h)),
            pl.BlockSpec((seq, HEAD_W), lambda b, h, i: (b, N_HEADS + h)),
            pl.BlockSpec((seq, HEAD_W), lambda b, h, i: (b, 2 * N_HEADS + h)),
            pl.BlockSpec((1, 2, t, t), lambda b, h, i: (h, 0, 0, 0)),
        ],
        out_specs=pl.BlockSpec((t, HEAD_W), lambda b, h, i: (b * nq + i, h)),
        out_shape=jax.ShapeDtypeStruct((batch * seq, D_MODEL), BF16),
        scratch_shapes=[pltpu.VMEM((2, t, LANES), F32), pltpu.VMEM((2, t, LANES), F32),
                        pltpu.VMEM((2, t, LANES), F32), pltpu.VMEM((2, t, LANES), F32),
                        pltpu.VMEM((2, t, HEAD_W), F32), pltpu.VMEM((2, t, t), F32),
                        pltpu.VMEM((2, t, t), BF16)],
        compiler_params=_params("parallel", "parallel", "arbitrary"),
        name="diff_attn",
    )(lq1, lk1, lq2, lk2, subln_g, qkv, qkv, qkv, bias_tiles)


CONV_T = 256
CONV_HALO = 32
CONV_ROWS = 64
CONV_LANES = 256


def _conv_ln_kernel(halo_ref, u_ref, w_ref, b_ref, g_ref, beta_ref, o_ref, buf_ref, y_ref, *, tiles_per_seq):
    i = pl.program_id(0)
    first = (i % tiles_per_seq) == 0

    @pl.when(first)
    def _():
        buf_ref[0:CONV_HALO, :] = jnp.zeros((CONV_HALO, D_MODEL), F32)

    @pl.when(jnp.logical_not(first))
    def _():
        buf_ref[0:CONV_HALO, :] = halo_ref[...]

    buf_ref[CONV_HALO:, :] = u_ref[...]
    base = CONV_HALO - (CONV_WIDTH - 1)
    for r0 in range(0, CONV_T, CONV_ROWS):
        for c0 in range(0, D_MODEL, CONV_LANES):
            lanes = slice(c0, c0 + CONV_LANES)
            acc = jnp.zeros((CONV_ROWS, CONV_LANES), F32)
            for j in range(CONV_WIDTH):
                acc = acc + w_ref[j:j + 1, lanes] * buf_ref[base + r0 + j:base + r0 + j + CONV_ROWS, lanes]
            y_ref[r0:r0 + CONV_ROWS, lanes] = acc + b_ref[:, lanes]

    def ln_body(c, carry):
        r0 = pl.multiple_of(c * CONV_ROWS, CONV_ROWS)
        y = y_ref[pl.ds(r0, CONV_ROWS), :]
        mu = jnp.mean(y, axis=-1, keepdims=True)
        yc = y - mu
        var = jnp.mean(yc * yc, axis=-1, keepdims=True)
        z = yc * lax.rsqrt(var + 1e-5) * g_ref[...] + beta_ref[...]
        o_ref[pl.ds(r0, CONV_ROWS), :] = (z * _sigmoid(z)).astype(o_ref.dtype)
        return carry
    lax.fori_loop(0, CONV_T // CONV_ROWS, ln_body, 0)


def _conv_ln(u, dw_w, dw_b, ln_g, ln_b, *, seq):
    m, d = u.shape
    tiles_per_seq = seq // CONV_T
    ratio = CONV_T // CONV_HALO
    row = lambda i: (0, 0)
    return pl.pallas_call(
        functools.partial(_conv_ln_kernel, tiles_per_seq=tiles_per_seq),
        grid=(m // CONV_T,),
        in_specs=[
            pl.BlockSpec((CONV_HALO, d), lambda i: (jnp.maximum(i * ratio - 1, 0), 0)),
            pl.BlockSpec((CONV_T, d), lambda i: (i, 0)),
            pl.BlockSpec((CONV_WIDTH, d), row),
            pl.BlockSpec((1, d), row), pl.BlockSpec((1, d), row), pl.BlockSpec((1, d), row),
        ],
        out_specs=pl.BlockSpec((CONV_T, d), lambda i: (i, 0)),
        out_shape=jax.ShapeDtypeStruct((m, d), BF16),
        scratch_shapes=[pltpu.VMEM((CONV_HALO + CONV_T, d), F32), pltpu.VMEM((CONV_T, d), F32)],
        compiler_params=_params("parallel"),
        name="conv_ln",
    )(u, u, dw_w, dw_b, ln_g, ln_b)


LRU_T = 256
LRU_HALO = 8


def _lru_kernel(halo_ref, u_ref, y_ref, cw_ref, cb_ref, wa_ref, ba_ref, wi_ref, bi_ref, lam_ref,
                o_ref, buf_ref, carry_ref):
    ti = pl.program_id(2)
    first = ti == 0
    buf_ref[LRU_HALO:, :] = u_ref[...]

    @pl.when(first)
    def _():
        buf_ref[0:LRU_HALO, :] = jnp.zeros((LRU_HALO, LRU_BLOCK_SIZE), F32)
        carry_ref[...] = jnp.zeros(carry_ref.shape, F32)

    @pl.when(jnp.logical_not(first))
    def _():
        buf_ref[0:LRU_HALO, :] = halo_ref[...]

    base = LRU_HALO - (LRU_CONV_WIDTH - 1)
    uc = jnp.zeros((LRU_T, LRU_BLOCK_SIZE), F32)
    for j in range(LRU_CONV_WIDTH):
        uc = uc + cw_ref[j:j + 1, :] * buf_ref[base + j:base + j + LRU_T, :]
    uc = uc + cb_ref[...]

    ucb = uc.astype(BF16)
    r = _sigmoid(jnp.dot(ucb, wa_ref[0], preferred_element_type=F32) + ba_ref[...])
    gate_i = _sigmoid(jnp.dot(ucb, wi_ref[0], preferred_element_type=F32) + bi_ref[...])
    neg_lam = -lam_ref[...]
    softplus = jnp.maximum(neg_lam, 0.0) + jnp.log1p(jnp.exp(-jnp.abs(neg_lam)))
    log_a = -LRU_C * r * softplus
    a = jnp.exp(log_a)
    mult = jnp.sqrt(-jnp.tanh(log_a) * (1.0 + a * a))
    b = mult * (gate_i * uc)

    row = lax.broadcasted_iota(jnp.int32, (LRU_T, LRU_BLOCK_SIZE), 0)
    s = 1
    while s < LRU_T:
        valid = row >= s
        a_sh = jnp.where(valid, pltpu.roll(a, s, 0), 1.0)
        b_sh = jnp.where(valid, pltpu.roll(b, s, 0), 0.0)
        b = a * b_sh + b
        a = a * a_sh
        s *= 2
    hs = a * carry_ref[0:1, :] + b
    carry_ref[0:1, :] = hs[LRU_T - 1:LRU_T, :]
    o_ref[...] = (hs * y_ref[...]).astype(o_ref.dtype)


def _lru_core(u, y, conv_w, conv_b, w_a, b_a, w_i, b_i, lam, *, batch, seq):
    m, d = u.shape
    nt = seq // LRU_T
    ratio = LRU_T // LRU_HALO
    c = LRU_BLOCK_SIZE
    chan = lambda b, g, t: (0, g)
    tile = lambda b, g, t: (b * nt + t, g)
    gate_w = pl.BlockSpec((1, c, c), lambda b, g, t: (g, 0, 0))
    return pl.pallas_call(
        _lru_kernel,
        grid=(batch, LRU_BLOCKS, nt),
        in_specs=[
            pl.BlockSpec((LRU_HALO, c), lambda b, g, t: (jnp.maximum((b * nt + t) * ratio - 1, 0), g)),
            pl.BlockSpec((LRU_T, c), tile),
            pl.BlockSpec((LRU_T, c), tile),
            pl.BlockSpec((LRU_CONV_WIDTH, c), chan),
            pl.BlockSpec((1, c), chan),
            gate_w, pl.BlockSpec((1, c), chan),
            gate_w, pl.BlockSpec((1, c), chan),
            pl.BlockSpec((1, c), chan),
        ],
        out_specs=pl.BlockSpec((LRU_T, c), tile),
        out_shape=jax.ShapeDtypeStruct((m, d), BF16),
        scratch_shapes=[pltpu.VMEM((LRU_HALO + LRU_T, c), F32), pltpu.VMEM((8, c), F32)],
        compiler_params=_params("parallel", "parallel", "arbitrary"),
        name="lru_core",
    )(u, u, y, conv_w, conv_b, w_a, b_a, w_i, b_i, lam)


def kernel(x, rel_bias, mixer_norm_g, attn_w_qkv, attn_lq1, attn_lk1, attn_lq2, attn_lk2, attn_subln_g, attn_w_o, conv_w_in, conv_b_in, conv_dw_w, conv_dw_b, conv_ln_g, conv_ln_b, conv_w_out, conv_b_out, lru_w_in, lru_conv_w, lru_conv_b, lru_w_a, lru_b_a, lru_w_i, lru_b_i, lru_lambda, lru_w_out, mlp_norm_g, mlp_w1, mlp_w2, final_norm_g):
    batch, seq, d = x.shape
    h = x.reshape(batch * seq, d)
    row = lambda v: v.reshape(1, -1)
    zero_bias = jnp.zeros((1, d), F32)
    q_scale = jnp.concatenate([jnp.full((1, d), HEAD_DIM ** -0.5 * LOG2E, F32), jnp.ones((1, 2 * d), F32)], axis=1)
    bias_tiles = _bias_tiles(rel_bias, ATTN_T)

    ia = ic = il = 0
    for layer in range(DEPTH):
        kind = layer % N_MIXERS
        g = row(mixer_norm_g[layer])
        if kind == 0:
            lambda_init = 0.8 - 0.6 * math.exp(-0.3 * layer)
            qkv = _norm_proj(h, g, attn_w_qkv[ia].astype(BF16), q_scale)
            att = _diff_attention(qkv, bias_tiles, row(attn_lq1[ia]), row(attn_lk1[ia]),
                                  row(attn_lq2[ia]), row(attn_lk2[ia]), row(attn_subln_g[ia]),
                                  batch=batch, seq=seq, lambda_init=lambda_init)
            h = _proj_res(att, attn_w_o[ia].astype(BF16), zero_bias, h)
            ia += 1
        elif kind == 1:
            u = _norm_glu(h, g, conv_w_in[ic].astype(BF16), row(conv_b_in[ic]))
            act = _conv_ln(u, conv_dw_w[ic], row(conv_dw_b[ic]), row(conv_ln_g[ic]), row(conv_ln_b[ic]), seq=seq)
            h = _proj_res(act, conv_w_out[ic].astype(BF16), row(conv_b_out[ic]), h)
            ic += 1
        else:
            u, y = _norm_lru_in(h, g, lru_w_in[il].astype(BF16))
            hy = _lru_core(u, y, lru_conv_w[il], row(lru_conv_b[il]), lru_w_a[il].astype(BF16), row(lru_b_a[il]),
                           lru_w_i[il].astype(BF16), row(lru_b_i[il]), row(lru_lambda[il]), batch=batch, seq=seq)
            h = _proj_res(hy, lru_w_out[il].astype(BF16), zero_bias, h)
            il += 1
        h = _mlp(h, row(mlp_norm_g[layer]), mlp_w1[layer].astype(BF16), mlp_w2[layer].astype(BF16))
    return _final_norm(h, row(final_norm_g)).reshape(batch, seq, d)
```

```python
import functools
import math

import numpy as np
import jax
import jax.numpy as jnp
from jax import lax
from jax.experimental import pallas as pl
from jax.experimental.pallas import tpu as pltpu

F32 = jnp.float32
BF16 = jnp.bfloat16

D_MODEL = 2048
DEPTH = 4
N_MIXERS = 3
HEAD_DIM = 128
HEAD_W = 2 * HEAD_DIM
N_HEADS = D_MODEL // HEAD_W
NUM_BUCKETS = 32
MAX_DISTANCE = 128
MASK_VALUE = -1e30
CONV_WIDTH = 31
LRU_BLOCKS = 8
LRU_BLOCK_SIZE = D_MODEL // LRU_BLOCKS
LRU_CONV_WIDTH = 4
LRU_C = 8.0
D_FF = 4 * D_MODEL

VMEM_LIMIT = 56 * 1024 * 1024
NORM_ROWS = 128


def _params(*sem):
    return pltpu.CompilerParams(dimension_semantics=sem, vmem_limit_bytes=VMEM_LIMIT)


def _rms_rows_to(dst_ref, x_ref, g_ref, rows, eps):
    def body(c, carry):
        r0 = pl.multiple_of(c * NORM_ROWS, NORM_ROWS)
        x = x_ref[pl.ds(r0, NORM_ROWS), :]
        ms = jnp.mean(x * x, axis=-1, keepdims=True)
        dst_ref[pl.ds(r0, NORM_ROWS), :] = (x * lax.rsqrt(ms + eps) * g_ref[...]).astype(BF16)
        return carry
    lax.fori_loop(0, rows // NORM_ROWS, body, 0)


def _sigmoid(x):
    return 1.0 / (1.0 + jnp.exp(-x))


def _gelu_tanh(x):
    c = math.sqrt(2.0 / math.pi)
    return 0.5 * x * (1.0 + jnp.tanh(c * (x + 0.044715 * (x * x * x))))


def _norm_proj_kernel(x_ref, g_ref, w_ref, s_ref, o_ref, xn_ref, *, tm):
    @pl.when(pl.program_id(1) == 0)
    def _():
        _rms_rows_to(xn_ref, x_ref, g_ref, tm, 1e-6)
    acc = jnp.dot(xn_ref[...], w_ref[...].astype(BF16), preferred_element_type=F32)
    o_ref[...] = (acc * s_ref[...]).astype(o_ref.dtype)


def _norm_proj(x, g, w, layer, col_scale, *, tm=1024, tn=512):
    m, k = x.shape
    n = w.shape[2]
    return pl.pallas_call(
        functools.partial(_norm_proj_kernel, tm=tm),
        grid=(m // tm, n // tn),
        in_specs=[
            pl.BlockSpec((tm, k), lambda i, j: (i, 0)),
            pl.BlockSpec((1, k), lambda i, j: (0, 0)),
            pl.BlockSpec((None, k, tn), lambda i, j: (layer, 0, j)),
            pl.BlockSpec((1, tn), lambda i, j: (0, j)),
        ],
        out_specs=pl.BlockSpec((tm, tn), lambda i, j: (i, j)),
        out_shape=jax.ShapeDtypeStruct((m, n), BF16),
        scratch_shapes=[pltpu.VMEM((tm, k), BF16)],
        compiler_params=_params("parallel", "arbitrary"),
        name="norm_proj",
    )(x, g, w, col_scale)


def _norm_glu_kernel(x_ref, g_ref, wa_ref, wg_ref, ba_ref, bg_ref, o_ref, xn_ref, *, tm):
    @pl.when(pl.program_id(1) == 0)
    def _():
        _rms_rows_to(xn_ref, x_ref, g_ref, tm, 1e-6)
    xn = xn_ref[...]
    a = jnp.dot(xn, wa_ref[...].astype(BF16), preferred_element_type=F32) + ba_ref[...]
    gate = jnp.dot(xn, wg_ref[...].astype(BF16), preferred_element_type=F32) + bg_ref[...]
    o_ref[...] = a * _sigmoid(gate)


def _norm_glu(x, g, w, layer, b, *, tm=1024, tn=512):
    m, k = x.shape
    half = w.shape[2] // 2
    nj = half // tn
    return pl.pallas_call(
        functools.partial(_norm_glu_kernel, tm=tm),
        grid=(m // tm, nj),
        in_specs=[
            pl.BlockSpec((tm, k), lambda i, j: (i, 0)),
            pl.BlockSpec((1, k), lambda i, j: (0, 0)),
            pl.BlockSpec((None, k, tn), lambda i, j: (layer, 0, j)),
            pl.BlockSpec((None, k, tn), lambda i, j: (layer, 0, j + nj)),
            pl.BlockSpec((1, tn), lambda i, j: (0, j)),
            pl.BlockSpec((1, tn), lambda i, j: (0, j + nj)),
        ],
        out_specs=pl.BlockSpec((tm, tn), lambda i, j: (i, j)),
        out_shape=jax.ShapeDtypeStruct((m, half), F32),
        scratch_shapes=[pltpu.VMEM((tm, k), BF16)],
        compiler_params=_params("parallel", "arbitrary"),
        name="norm_glu",
    )(x, g, w, w, b, b)


def _norm_lru_in_kernel(x_ref, g_ref, wu_ref, wy_ref, u_ref, y_ref, xn_ref, *, tm):
    @pl.when(pl.program_id(1) == 0)
    def _():
        _rms_rows_to(xn_ref, x_ref, g_ref, tm, 1e-6)
    xn = xn_ref[...]
    u_ref[...] = jnp.dot(xn, wu_ref[...].astype(BF16), preferred_element_type=F32)
    y_ref[...] = _gelu_tanh(jnp.dot(xn, wy_ref[...].astype(BF16), preferred_element_type=F32))


def _norm_lru_in(x, g, w, layer, *, tm=1024, tn=512):
    m, k = x.shape
    half = w.shape[2] // 2
    nj = half // tn
    return pl.pallas_call(
        functools.partial(_norm_lru_in_kernel, tm=tm),
        grid=(m // tm, nj),
        in_specs=[
            pl.BlockSpec((tm, k), lambda i, j: (i, 0)),
            pl.BlockSpec((1, k), lambda i, j: (0, 0)),
            pl.BlockSpec((None, k, tn), lambda i, j: (layer, 0, j)),
            pl.BlockSpec((None, k, tn), lambda i, j: (layer, 0, j + nj)),
        ],
        out_specs=[pl.BlockSpec((tm, tn), lambda i, j: (i, j)),
                   pl.BlockSpec((tm, tn), lambda i, j: (i, j))],
        out_shape=[jax.ShapeDtypeStruct((m, half), F32), jax.ShapeDtypeStruct((m, half), F32)],
        scratch_shapes=[pltpu.VMEM((tm, k), BF16)],
        compiler_params=_params("parallel", "arbitrary"),
        name="norm_lru_in",
    )(x, g, w, w)


def _proj_res_kernel(a_ref, w_ref, b_ref, r_ref, o_ref):
    acc = jnp.dot(a_ref[...], w_ref[...].astype(BF16), preferred_element_type=F32)
    o_ref[...] = r_ref[...] + (acc + b_ref[...])


def _proj_res(a, w, layer, b, res, *, tm=1024, tn=512):
    m, k = a.shape
    n = w.shape[2]
    return pl.pallas_call(
        _proj_res_kernel,
        grid=(m // tm, n // tn),
        in_specs=[
            pl.BlockSpec((tm, k), lambda i, j: (i, 0)),
            pl.BlockSpec((None, k, tn), lambda i, j: (layer, 0, j)),
            pl.BlockSpec((1, tn), lambda i, j: (0, j)),
            pl.BlockSpec((tm, tn), lambda i, j: (i, j)),
        ],
        out_specs=pl.BlockSpec((tm, tn), lambda i, j: (i, j)),
        out_shape=jax.ShapeDtypeStruct((m, n), F32),
        compiler_params=_params("parallel", "arbitrary"),
        name="proj_res",
    )(a, w, b, res)


def _mlp_kernel(x_ref, g_ref, w1_ref, w2_ref, o_ref, xn_ref, *, tm):
    f = pl.program_id(1)

    @pl.when(f == 0)
    def _():
        _rms_rows_to(xn_ref, x_ref, g_ref, tm, 1e-6)

    z = jnp.maximum(jnp.dot(xn_ref[...], w1_ref[...], preferred_element_type=F32), 0.0)
    part = jnp.dot((z * z).astype(BF16), w2_ref[...], preferred_element_type=F32)

    @pl.when(f == 0)
    def _():
        o_ref[...] = x_ref[...] + part

    @pl.when(f != 0)
    def _():
        o_ref[...] += part


def _mlp(x, g, w1, w2, layer, *, tm=512, tf=1024):
    m, d = x.shape
    ff = w1.shape[2]
    return pl.pallas_call(
        functools.partial(_mlp_kernel, tm=tm),
        grid=(m // tm, ff // tf),
        in_specs=[
            pl.BlockSpec((tm, d), lambda i, f: (i, 0)),
            pl.BlockSpec((1, d), lambda i, f: (0, 0)),
            pl.BlockSpec((None, d, tf), lambda i, f: (layer, 0, f)),
            pl.BlockSpec((None, tf, d), lambda i, f: (layer, f, 0)),
        ],
        out_specs=pl.BlockSpec((tm, d), lambda i, f: (i, 0)),
        out_shape=jax.ShapeDtypeStruct((m, d), F32),
        scratch_shapes=[pltpu.VMEM((tm, d), BF16)],
        compiler_params=_params("parallel", "arbitrary"),
        name="mlp",
    )(x, g, w1, w2)


def _final_norm_kernel(x_ref, g_ref, o_ref):
    x = x_ref[...]
    ms = jnp.mean(x * x, axis=-1, keepdims=True)
    o_ref[...] = x * lax.rsqrt(ms + 1e-6) * g_ref[...]


def _final_norm(x, g, *, tm=256):
    m, d = x.shape
    return pl.pallas_call(
        _final_norm_kernel,
        grid=(m // tm,),
        in_specs=[pl.BlockSpec((tm, d), lambda i: (i, 0)), pl.BlockSpec((1, d), lambda i: (0, 0))],
        out_specs=pl.BlockSpec((tm, d), lambda i: (i, 0)),
        out_shape=jax.ShapeDtypeStruct((m, d), F32),
        compiler_params=_params("parallel"),
        name="final_norm",
    )(x, g)


ATTN_T = 512
ATTN_STRIP = 32
ATTN_UNROLL = 4
LANES = 128
LOG2E = math.log2(math.e)


def _bucket_tiles(t):
    r = np.arange(t, dtype=np.int32)[:, None]
    c = np.arange(t, dtype=np.int32)[None, :]
    tiles = []
    for d in range(2):
        n = np.maximum(d * t + r - c, 0)
        max_exact = NUM_BUCKETS // 2
        nf = np.maximum(n, 1).astype(np.float32)
        large = max_exact + (np.log(nf / np.float32(max_exact)) / np.float32(math.log(MAX_DISTANCE / max_exact))
                             * np.float32(NUM_BUCKETS - max_exact)).astype(np.int32)
        large = np.minimum(large, NUM_BUCKETS - 1)
        bucket = np.where(n < max_exact, n, large).astype(np.int32)
        tiles.append(np.where(d * t + r - c >= 0, bucket, -1))
    return np.stack(tiles)


def _bias_tiles_kernel(rb_ref, bucket_ref, o_ref):
    h = pl.program_id(0)
    bucket = bucket_ref[0]
    acc = jnp.zeros(bucket.shape, F32)
    for j in range(NUM_BUCKETS):
        acc = jnp.where(bucket == j, rb_ref[j, h], acc)
    acc = (acc - rb_ref[NUM_BUCKETS - 1, h]) * LOG2E
    o_ref[0, 0] = jnp.where(bucket < 0, MASK_VALUE, acc)


def _bias_tiles(rel_bias, t):
    buckets = jnp.asarray(_bucket_tiles(t))
    return pl.pallas_call(
        _bias_tiles_kernel,
        grid=(N_HEADS, 2),
        in_specs=[
            pl.BlockSpec(memory_space=pltpu.SMEM),
            pl.BlockSpec((1, t, t), lambda h, d: (d, 0, 0)),
        ],
        out_specs=pl.BlockSpec((1, 1, t, t), lambda h, d: (h, d, 0, 0)),
        out_shape=jax.ShapeDtypeStruct((N_HEADS, 2, t, t), F32),
        compiler_params=_params("parallel", "parallel"),
        name="bias_tiles",
    )(rel_bias, buckets)


def _attn_kernel(lq1_ref, lk1_ref, lq2_ref, lk2_ref, sg_ref, q_ref, k_ref, v_ref, bias_ref,
                 o_ref, m_ref, l_ref, alpha_ref, part_ref, acc_ref, s_ref, p_ref, *, t, lambda_init):
    qi = pl.program_id(2)
    chunks = [slice(j * LANES, (j + 1) * LANES) for j in range(t // LANES)]

    m_ref[...] = jnp.full(m_ref.shape, MASK_VALUE, F32)
    l_ref[...] = jnp.zeros(l_ref.shape, F32)
    acc_ref[...] = jnp.zeros(acc_ref.shape, F32)
    alpha_ref[...] = jnp.zeros(alpha_ref.shape, F32)
    p_ref[...] = jnp.zeros(p_ref.shape, BF16)

    def qk_logits(k0):
        for c in range(2):
            lanes = slice(c * HEAD_DIM, (c + 1) * HEAD_DIM)
            s_ref[c] = lax.dot_general(q_ref[:, lanes], k_ref[pl.ds(k0, t), lanes],
                                       (((1,), (1,)), ((), ())), preferred_element_type=F32)

    def accumulate(k0):
        v = v_ref[pl.ds(k0, t), :]
        for c in range(2):
            alpha = alpha_ref[c]
            pv = jnp.dot(p_ref[c], v, preferred_element_type=F32)
            acc_ref[c] = jnp.concatenate([alpha, alpha], axis=1) * acc_ref[c] + pv

    def block(k0, k0_prev, k0_next, bias_index):
        def logits(c, rows):
            s = s_ref[c, rows, :]
            if bias_index is not None:
                s = s + bias_ref[0, bias_index, rows, :]
            return s

        def strip_rows(r):
            return pl.ds(pl.multiple_of(r * ATTN_STRIP, ATTN_STRIP), ATTN_STRIP)

        def max_strip(r, carry):
            rows = strip_rows(r)
            for c in range(2):
                s = logits(c, rows)
                part = s[:, chunks[0]]
                for ch in chunks[1:]:
                    part = jnp.maximum(part, s[:, ch])
                part_ref[c, rows, :] = part
            return carry
        lax.fori_loop(0, t // ATTN_STRIP, max_strip, 0, unroll=ATTN_UNROLL)

        accumulate(k0_prev)
        for c in range(2):
            m_prev = m_ref[c]
            m_new = jnp.maximum(m_prev, jnp.max(part_ref[c], axis=1, keepdims=True))
            alpha_ref[c] = jnp.exp2(m_prev - m_new)
            m_ref[c] = m_new

        def exp_strip(r, carry):
            rows = strip_rows(r)
            for c in range(2):
                s = logits(c, rows)
                m_new = m_ref[c, rows, :]
                part = None
                for ch in chunks:
                    p = jnp.exp2(s[:, ch] - m_new)
                    part = p if part is None else part + p
                    p_ref[c, rows, ch] = p.astype(BF16)
                part_ref[c, rows, :] = part
            return carry
        lax.fori_loop(0, t // ATTN_STRIP, exp_strip, 0, unroll=ATTN_UNROLL)

        if k0_next is not None:
            qk_logits(k0_next)
        for c in range(2):
            l_ref[c] = alpha_ref[c] * l_ref[c] + jnp.sum(part_ref[c], axis=1, keepdims=True)

    def start(ki):
        return pl.multiple_of(jnp.maximum(ki, 0) * t, t)

    qk_logits(0)

    def far_body(ki, carry):
        block(start(ki), start(ki - 1), start(ki + 1), None)
        return carry
    lax.fori_loop(0, jnp.maximum(qi - 1, 0), far_body, 0)

    @pl.when(qi >= 1)
    def _():
        block(start(qi - 1), start(qi - 2), start(qi), 1)

    block(start(qi), start(qi - 1), None, 0)
    accumulate(start(qi))

    lam = (jnp.exp(jnp.sum(lq1_ref[...] * lk1_ref[...], keepdims=True))
           - jnp.exp(jnp.sum(lq2_ref[...] * lk2_ref[...], keepdims=True)) + lambda_init)
    l0 = l_ref[0]
    l1 = l_ref[1]
    o = (acc_ref[0] / jnp.concatenate([l0, l0], axis=1)
         - lam * (acc_ref[1] / jnp.concatenate([l1, l1], axis=1)))
    ms = jnp.mean(o * o, axis=-1, keepdims=True)
    o = o * lax.rsqrt(ms + 1e-5) * sg_ref[...] * (1.0 - lambda_init)
    o_ref[...] = o.astype(o_ref.dtype)


def _diff_attention(qkv, bias_tiles, lq1, lk1, lq2, lk2, subln_g, *, batch, seq, lambda_init):
    t = ATTN_T
    nq = seq // t
    lam_spec = pl.BlockSpec((1, HEAD_DIM), lambda b, h, i: (0, 0))
    return pl.pallas_call(
        functools.partial(_attn_kernel, t=t, lambda_init=lambda_init),
        grid=(batch, N_HEADS, nq),
        in_specs=[
            lam_spec, lam_spec, lam_spec, lam_spec,
            pl.BlockSpec((1, HEAD_W), lambda b, h, i: (0, 0)),
            pl.BlockSpec((t, HEAD_W), lambda b, h, i: (b * nq + i, h)),
            pl.BlockSpec((seq, HEAD_W), lambda b, h, i: (b, N_HEADS + h)),
            pl.BlockSpec((seq, HEAD_W), lambda b, h, i: (b, 2 * N_HEADS + h)),
            pl.BlockSpec((1, 2, t, t), lambda b, h, i: (h, 0, 0, 0)),
        ],
        out_specs=pl.BlockSpec((t, HEAD_W), lambda b, h, i: (b * nq + i, h)),
        out_shape=jax.ShapeDtypeStruct((batch * seq, D_MODEL), BF16),
        scratch_shapes=[pltpu.VMEM((2, t, LANES), F32), pltpu.VMEM((2, t, LANES), F32),
                        pltpu.VMEM((2, t, LANES), F32), pltpu.VMEM((2, t, LANES), F32),
                        pltpu.VMEM((2, t, HEAD_W), F32), pltpu.VMEM((2, t, t), F32),
                        pltpu.VMEM((2, t, t), BF16)],
        compiler_params=_params("parallel", "parallel", "arbitrary"),
        name="diff_attn",
    )(lq1, lk1, lq2, lk2, subln_g, qkv, qkv, qkv, bias_tiles)


CONV_T = 256
CONV_HALO = 32
CONV_ROWS = 64
CONV_LANES = 256


def _conv_ln_kernel(halo_ref, u_ref, w_ref, b_ref, g_ref, beta_ref, o_ref, buf_ref, y_ref, *, tiles_per_seq):
    i = pl.program_id(0)
    first = (i % tiles_per_seq) == 0

    @pl.when(first)
    def _():
        buf_ref[0:CONV_HALO, :] = jnp.zeros((CONV_HALO, D_MODEL), F32)

    @pl.when(jnp.logical_not(first))
    def _():
        buf_ref[0:CONV_HALO, :] = halo_ref[...]

    buf_ref[CONV_HALO:, :] = u_ref[...]
    base = CONV_HALO - (CONV_WIDTH - 1)
    for r0 in range(0, CONV_T, CONV_ROWS):
        for c0 in range(0, D_MODEL, CONV_LANES):
            lanes = slice(c0, c0 + CONV_LANES)
            acc = jnp.zeros((CONV_ROWS, CONV_LANES), F32)
            for j in range(CONV_WIDTH):
                acc = acc + w_ref[j:j + 1, lanes] * buf_ref[base + r0 + j:base + r0 + j + CONV_ROWS, lanes]
            y_ref[r0:r0 + CONV_ROWS, lanes] = acc + b_ref[:, lanes]

    def ln_body(c, carry):
        r0 = pl.multiple_of(c * CONV_ROWS, CONV_ROWS)
        y = y_ref[pl.ds(r0, CONV_ROWS), :]
        mu = jnp.mean(y, axis=-1, keepdims=True)
        yc = y - mu
        var = jnp.mean(yc * yc, axis=-1, keepdims=True)
        z = yc * lax.rsqrt(var + 1e-5) * g_ref[...] + beta_ref[...]
        o_ref[pl.ds(r0, CONV_ROWS), :] = (z * _sigmoid(z)).astype(o_ref.dtype)
        return carry
    lax.fori_loop(0, CONV_T // CONV_ROWS, ln_body, 0)


def _conv_ln(u, dw_w, dw_b, ln_g, ln_b, *, seq):
    m, d = u.shape
    tiles_per_seq = seq // CONV_T
    ratio = CONV_T // CONV_HALO
    row = lambda i: (0, 0)
    return pl.pallas_call(
        functools.partial(_conv_ln_kernel, tiles_per_seq=tiles_per_seq),
        grid=(m // CONV_T,),
        in_specs=[
            pl.BlockSpec((CONV_HALO, d), lambda i: (jnp.maximum(i * ratio - 1, 0), 0)),
            pl.BlockSpec((CONV_T, d), lambda i: (i, 0)),
            pl.BlockSpec((CONV_WIDTH, d), row),
            pl.BlockSpec((1, d), row), pl.BlockSpec((1, d), row), pl.BlockSpec((1, d), row),
        ],
        out_specs=pl.BlockSpec((CONV_T, d), lambda i: (i, 0)),
        out_shape=jax.ShapeDtypeStruct((m, d), BF16),
        scratch_shapes=[pltpu.VMEM((CONV_HALO + CONV_T, d), F32), pltpu.VMEM((CONV_T, d), F32)],
        compiler_params=_params("parallel"),
        name="conv_ln",
    )(u, u, dw_w, dw_b, ln_g, ln_b)


LRU_T = 256
LRU_HALO = 8


def _lru_kernel(halo_ref, u_ref, y_ref, cw_ref, cb_ref, wa_ref, ba_ref, wi_ref, bi_ref, lam_ref,
                o_ref, buf_ref, carry_ref):
    ti = pl.program_id(2)
    first = ti == 0
    buf_ref[LRU_HALO:, :] = u_ref[...]

    @pl.when(first)
    def _():
        buf_ref[0:LRU_HALO, :] = jnp.zeros((LRU_HALO, LRU_BLOCK_SIZE), F32)
        carry_ref[...] = jnp.zeros(carry_ref.shape, F32)

    @pl.when(jnp.logical_not(first))
    def _():
        buf_ref[0:LRU_HALO, :] = halo_ref[...]

    base = LRU_HALO - (LRU_CONV_WIDTH - 1)
    uc = jnp.zeros((LRU_T, LRU_BLOCK_SIZE), F32)
    for j in range(LRU_CONV_WIDTH):
        uc = uc + cw_ref[j:j + 1, :] * buf_ref[base + j:base + j + LRU_T, :]
    uc = uc + cb_ref[...]

    ucb = uc.astype(BF16)
    r = _sigmoid(jnp.dot(ucb, wa_ref[0].astype(BF16), preferred_element_type=F32) + ba_ref[...])
    gate_i = _sigmoid(jnp.dot(ucb, wi_ref[0].astype(BF16), preferred_element_type=F32) + bi_ref[...])
    neg_lam = -lam_ref[...]
    softplus = jnp.maximum(neg_lam, 0.0) + jnp.log1p(jnp.exp(-jnp.abs(neg_lam)))
    log_a = -LRU_C * r * softplus
    a = jnp.exp(log_a)
    mult = jnp.sqrt(-jnp.tanh(log_a) * (1.0 + a * a))
    b = mult * (gate_i * uc)

    row = lax.broadcasted_iota(jnp.int32, (LRU_T, LRU_BLOCK_SIZE), 0)
    s = 1
    while s < LRU_T:
        valid = row >= s
        a_sh = jnp.where(valid, pltpu.roll(a, s, 0), 1.0)
        b_sh = jnp.where(valid, pltpu.roll(b, s, 0), 0.0)
        b = a * b_sh + b
        a = a * a_sh
        s *= 2
    hs = a * carry_ref[0:1, :] + b
    carry_ref[0:1, :] = hs[LRU_T - 1:LRU_T, :]
    o_ref[...] = (hs * y_ref[...]).astype(o_ref.dtype)


def _lru_core(u, y, conv_w, conv_b, w_a, b_a, w_i, b_i, lam, *, batch, seq):
    m, d = u.shape
    nt = seq // LRU_T
    ratio = LRU_T // LRU_HALO
    c = LRU_BLOCK_SIZE
    chan = lambda b, g, t: (0, g)
    tile = lambda b, g, t: (b * nt + t, g)
    gate_w = pl.BlockSpec((1, c, c), lambda b, g, t: (g, 0, 0))
    return pl.pallas_call(
        _lru_kernel,
        grid=(batch, LRU_BLOCKS, nt),
        in_specs=[
            pl.BlockSpec((LRU_HALO, c), lambda b, g, t: (jnp.maximum((b * nt + t) * ratio - 1, 0), g)),
            pl.BlockSpec((LRU_T, c), tile),
            pl.BlockSpec((LRU_T, c), tile),
            pl.BlockSpec((LRU_CONV_WIDTH, c), chan),
            pl.BlockSpec((1, c), chan),
            gate_w, pl.BlockSpec((1, c), chan),
            gate_w, pl.BlockSpec((1, c), chan),
            pl.BlockSpec((1, c), chan),
        ],
        out_specs=pl.BlockSpec((LRU_T, c), tile),
        out_shape=jax.ShapeDtypeStruct((m, d), BF16),
        scratch_shapes=[pltpu.VMEM((LRU_HALO + LRU_T, c), F32), pltpu.VMEM((8, c), F32)],
        compiler_params=_params("parallel", "parallel", "arbitrary"),
        name="lru_core",
    )(u, u, y, conv_w, conv_b, w_a, b_a, w_i, b_i, lam)


def kernel(x, rel_bias, mixer_norm_g, attn_w_qkv, attn_lq1, attn_lk1, attn_lq2, attn_lk2, attn_subln_g, attn_w_o, conv_w_in, conv_b_in, conv_dw_w, conv_dw_b, conv_ln_g, conv_ln_b, conv_w_out, conv_b_out, lru_w_in, lru_conv_w, lru_conv_b, lru_w_a, lru_b_a, lru_w_i, lru_b_i, lru_lambda, lru_w_out, mlp_norm_g, mlp_w1, mlp_w2, final_norm_g):
    batch, seq, d = x.shape
    h = x.reshape(batch * seq, d)
    row = lambda v: v.reshape(1, -1)
    zero_bias = jnp.zeros((1, d), F32)
    q_scale = jnp.concatenate([jnp.full((1, d), HEAD_DIM ** -0.5 * LOG2E, F32), jnp.ones((1, 2 * d), F32)], axis=1)
    bias_tiles = _bias_tiles(rel_bias, ATTN_T)
    mlp_w1_bf16 = mlp_w1.astype(BF16)
    mlp_w2_bf16 = mlp_w2.astype(BF16)

    ia = ic = il = 0
    for layer in range(DEPTH):
        kind = layer % N_MIXERS
        g = row(mixer_norm_g[layer])
        if kind == 0:
            lambda_init = 0.8 - 0.6 * math.exp(-0.3 * layer)
            qkv = _norm_proj(h, g, attn_w_qkv, ia, q_scale)
            att = _diff_attention(qkv, bias_tiles, row(attn_lq1[ia]), row(attn_lk1[ia]),
                                  row(attn_lq2[ia]), row(attn_lk2[ia]), row(attn_subln_g[ia]),
                                  batch=batch, seq=seq, lambda_init=lambda_init)
            h = _proj_res(att, attn_w_o, ia, zero_bias, h)
            ia += 1
        elif kind == 1:
            u = _norm_glu(h, g, conv_w_in, ic, row(conv_b_in[ic]))
            act = _conv_ln(u, conv_dw_w[ic], row(conv_dw_b[ic]), row(conv_ln_g[ic]), row(conv_ln_b[ic]), seq=seq)
            h = _proj_res(act, conv_w_out, ic, row(conv_b_out[ic]), h)
            ic += 1
        else:
            u, y = _norm_lru_in(h, g, lru_w_in, il)
            hy = _lru_core(u, y, lru_conv_w[il], row(lru_conv_b[il]), lru_w_a[il], row(lru_b_a[il]),
                           lru_w_i[il], row(lru_b_i[il]), row(lru_lambda[il]), batch=batch, seq=seq)
            h = _proj_res(hy, lru_w_out, il, zero_bias, h)
            il += 1
        h = _mlp(h, row(mlp_norm_g[layer]), mlp_w1_bf16, mlp_w2_bf16, layer)
    return _final_norm(h, row(final_norm_g)).reshape(batch, seq, d)
```

```python
import functools
import math

import numpy as np
import jax
import jax.numpy as jnp
from jax import lax
from jax.experimental import pallas as pl
from jax.experimental.pallas import tpu as pltpu

F32 = jnp.float32
BF16 = jnp.bfloat16

D_MODEL = 2048
DEPTH = 4
N_MIXERS = 3
HEAD_DIM = 128
HEAD_W = 2 * HEAD_DIM
N_HEADS = D_MODEL // HEAD_W
NUM_BUCKETS = 32
MAX_DISTANCE = 128
MASK_VALUE = -1e30
CONV_WIDTH = 31
LRU_BLOCKS = 8
LRU_BLOCK_SIZE = D_MODEL // LRU_BLOCKS
LRU_CONV_WIDTH = 4
LRU_C = 8.0
D_FF = 4 * D_MODEL

VMEM_LIMIT = 56 * 1024 * 1024
NORM_ROWS = 128
LANES = 128
SUBLANES = 8


def _params(*sem):
    return pltpu.CompilerParams(dimension_semantics=sem, vmem_limit_bytes=VMEM_LIMIT)


def _rms_rows_to(dst_ref, x_ref, g_ref, rows, eps):
    def body(c, carry):
        r0 = pl.multiple_of(c * NORM_ROWS, NORM_ROWS)
        x = x_ref[pl.ds(r0, NORM_ROWS), :]
        ms = jnp.mean(x * x, axis=-1, keepdims=True)
        dst_ref[pl.ds(r0, NORM_ROWS), :] = (x * lax.rsqrt(ms + eps) * g_ref[...]).astype(BF16)
        return carry
    lax.fori_loop(0, rows // NORM_ROWS, body, 0)


def _sigmoid(x):
    return 1.0 / (1.0 + jnp.exp(-x))


def _gelu_tanh(x):
    c = math.sqrt(2.0 / math.pi)
    return 0.5 * x * (1.0 + jnp.tanh(c * (x + 0.044715 * (x * x * x))))


def _norm_proj_kernel(x_ref, g_ref, w_ref, s_ref, o_ref, xn_ref, *, tm):
    @pl.when(pl.program_id(1) == 0)
    def _():
        _rms_rows_to(xn_ref, x_ref, g_ref, tm, 1e-6)
    acc = jnp.dot(xn_ref[...], w_ref[...].astype(BF16), preferred_element_type=F32)
    o_ref[...] = (acc * s_ref[...]).astype(o_ref.dtype)


def _norm_proj(x, g, w, layer, col_scale, *, tm=1024, tn=512):
    m, k = x.shape
    n = w.shape[2]
    return pl.pallas_call(
        functools.partial(_norm_proj_kernel, tm=tm),
        grid=(m // tm, n // tn),
        in_specs=[
            pl.BlockSpec((tm, k), lambda i, j: (i, 0)),
            pl.BlockSpec((1, k), lambda i, j: (0, 0)),
            pl.BlockSpec((None, k, tn), lambda i, j: (layer, 0, j)),
            pl.BlockSpec((1, tn), lambda i, j: (0, j)),
        ],
        out_specs=pl.BlockSpec((tm, tn), lambda i, j: (i, j)),
        out_shape=jax.ShapeDtypeStruct((m, n), BF16),
        scratch_shapes=[pltpu.VMEM((tm, k), BF16)],
        compiler_params=_params("parallel", "arbitrary"),
        name="norm_proj",
    )(x, g, w, col_scale)


def _norm_glu_kernel(x_ref, g_ref, wa_ref, wg_ref, ba_ref, bg_ref, o_ref, xn_ref, *, tm):
    @pl.when(pl.program_id(1) == 0)
    def _():
        _rms_rows_to(xn_ref, x_ref, g_ref, tm, 1e-6)
    xn = xn_ref[...]
    a = jnp.dot(xn, wa_ref[...].astype(BF16), preferred_element_type=F32) + ba_ref[...]
    gate = jnp.dot(xn, wg_ref[...].astype(BF16), preferred_element_type=F32) + bg_ref[...]
    o_ref[...] = a * _sigmoid(gate)


def _norm_glu(x, g, w, layer, b, *, tm=1024, tn=512):
    m, k = x.shape
    half = w.shape[2] // 2
    nj = half // tn
    return pl.pallas_call(
        functools.partial(_norm_glu_kernel, tm=tm),
        grid=(m // tm, nj),
        in_specs=[
            pl.BlockSpec((tm, k), lambda i, j: (i, 0)),
            pl.BlockSpec((1, k), lambda i, j: (0, 0)),
            pl.BlockSpec((None, k, tn), lambda i, j: (layer, 0, j)),
            pl.BlockSpec((None, k, tn), lambda i, j: (layer, 0, j + nj)),
            pl.BlockSpec((1, tn), lambda i, j: (0, j)),
            pl.BlockSpec((1, tn), lambda i, j: (0, j + nj)),
        ],
        out_specs=pl.BlockSpec((tm, tn), lambda i, j: (i, j)),
        out_shape=jax.ShapeDtypeStruct((m, half), F32),
        scratch_shapes=[pltpu.VMEM((tm, k), BF16)],
        compiler_params=_params("parallel", "arbitrary"),
        name="norm_glu",
    )(x, g, w, w, b, b)


def _norm_lru_in_kernel(x_ref, g_ref, wu_ref, wy_ref, u_ref, y_ref, xn_ref, *, tm):
    @pl.when(pl.program_id(1) == 0)
    def _():
        _rms_rows_to(xn_ref, x_ref, g_ref, tm, 1e-6)
    xn = xn_ref[...]
    u_ref[...] = jnp.dot(xn, wu_ref[...].astype(BF16), preferred_element_type=F32)
    y_ref[...] = _gelu_tanh(jnp.dot(xn, wy_ref[...].astype(BF16), preferred_element_type=F32))


def _norm_lru_in(x, g, w, layer, *, tm=1024, tn=512):
    m, k = x.shape
    half = w.shape[2] // 2
    nj = half // tn
    return pl.pallas_call(
        functools.partial(_norm_lru_in_kernel, tm=tm),
        grid=(m // tm, nj),
        in_specs=[
            pl.BlockSpec((tm, k), lambda i, j: (i, 0)),
            pl.BlockSpec((1, k), lambda i, j: (0, 0)),
            pl.BlockSpec((None, k, tn), lambda i, j: (layer, 0, j)),
            pl.BlockSpec((None, k, tn), lambda i, j: (layer, 0, j + nj)),
        ],
        out_specs=[pl.BlockSpec((tm, tn), lambda i, j: (i, j)),
                   pl.BlockSpec((tm, tn), lambda i, j: (i, j))],
        out_shape=[jax.ShapeDtypeStruct((m, half), F32), jax.ShapeDtypeStruct((m, half), F32)],
        scratch_shapes=[pltpu.VMEM((tm, k), BF16)],
        compiler_params=_params("parallel", "arbitrary"),
        name="norm_lru_in",
    )(x, g, w, w)


def _proj_res_kernel(a_ref, w_ref, b_ref, r_ref, o_ref):
    acc = jnp.dot(a_ref[...], w_ref[...].astype(BF16), preferred_element_type=F32)
    o_ref[...] = r_ref[...] + (acc + b_ref[...])


def _proj_res(a, w, layer, b, res, *, tm=1024, tn=512):
    m, k = a.shape
    n = w.shape[2]
    return pl.pallas_call(
        _proj_res_kernel,
        grid=(m // tm, n // tn),
        in_specs=[
            pl.BlockSpec((tm, k), lambda i, j: (i, 0)),
            pl.BlockSpec((None, k, tn), lambda i, j: (layer, 0, j)),
            pl.BlockSpec((1, tn), lambda i, j: (0, j)),
            pl.BlockSpec((tm, tn), lambda i, j: (i, j)),
        ],
        out_specs=pl.BlockSpec((tm, tn), lambda i, j: (i, j)),
        out_shape=jax.ShapeDtypeStruct((m, n), F32),
        compiler_params=_params("parallel", "arbitrary"),
        name="proj_res",
    )(a, w, b, res)


def _mlp_kernel(*refs, tm, final_norm):
    if final_norm:
        x_ref, g_ref, w1_ref, w2_ref, fg_ref, o_ref, xn_ref = refs
    else:
        x_ref, g_ref, w1_ref, w2_ref, o_ref, xn_ref = refs
    f = pl.program_id(1)

    @pl.when(f == 0)
    def _():
        _rms_rows_to(xn_ref, x_ref, g_ref, tm, 1e-6)

    z = jnp.maximum(jnp.dot(xn_ref[...], w1_ref[...], preferred_element_type=F32), 0.0)
    part = jnp.dot((z * z).astype(BF16), w2_ref[...], preferred_element_type=F32)

    @pl.when(f == 0)
    def _():
        o_ref[...] = x_ref[...] + part

    @pl.when(f != 0)
    def _():
        o_ref[...] += part

    if final_norm:
        @pl.when(f == pl.num_programs(1) - 1)
        def _():
            def body(c, carry):
                rows = pl.ds(pl.multiple_of(c * NORM_ROWS, NORM_ROWS), NORM_ROWS)
                y = o_ref[rows, :]
                ms = jnp.mean(y * y, axis=-1, keepdims=True)
                o_ref[rows, :] = y * lax.rsqrt(ms + 1e-6) * fg_ref[...]
                return carry
            lax.fori_loop(0, tm // NORM_ROWS, body, 0)


def _mlp(x, g, w1, w2, layer, final_g=None, *, tm=512, tf=1024):
    m, d = x.shape
    ff = w1.shape[2]
    final_norm = final_g is not None
    row_spec = pl.BlockSpec((1, d), lambda i, f: (0, 0))
    return pl.pallas_call(
        functools.partial(_mlp_kernel, tm=tm, final_norm=final_norm),
        grid=(m // tm, ff // tf),
        in_specs=[
            pl.BlockSpec((tm, d), lambda i, f: (i, 0)),
            row_spec,
            pl.BlockSpec((None, d, tf), lambda i, f: (layer, 0, f)),
            pl.BlockSpec((None, tf, d), lambda i, f: (layer, f, 0)),
        ] + ([row_spec] if final_norm else []),
        out_specs=pl.BlockSpec((tm, d), lambda i, f: (i, 0)),
        out_shape=jax.ShapeDtypeStruct((m, d), F32),
        scratch_shapes=[pltpu.VMEM((tm, d), BF16)],
        compiler_params=_params("parallel", "arbitrary"),
        name="mlp",
    )(x, g, w1, w2, *((final_g,) if final_norm else ()))


ATTN_T = 512
ATTN_STRIP = 32
ATTN_UNROLL = 4
LOG2E = math.log2(math.e)


def _bucket_tiles(t):
    r = np.arange(t, dtype=np.int32)[:, None]
    c = np.arange(t, dtype=np.int32)[None, :]
    tiles = []
    for d in range(2):
        n = np.maximum(d * t + r - c, 0)
        max_exact = NUM_BUCKETS // 2
        nf = np.maximum(n, 1).astype(np.float32)
        large = max_exact + (np.log(nf / np.float32(max_exact)) / np.float32(math.log(MAX_DISTANCE / max_exact))
                             * np.float32(NUM_BUCKETS - max_exact)).astype(np.int32)
        large = np.minimum(large, NUM_BUCKETS - 1)
        bucket = np.where(n < max_exact, n, large).astype(np.int32)
        tiles.append(np.where(d * t + r - c >= 0, bucket, -1))
    return np.stack(tiles)


def _bias_tiles_kernel(rb_ref, bucket_ref, o_ref):
    h = pl.program_id(0)
    bucket = bucket_ref[0]
    acc = jnp.zeros(bucket.shape, F32)
    for j in range(NUM_BUCKETS):
        acc = jnp.where(bucket == j, rb_ref[j, h], acc)
    acc = (acc - rb_ref[NUM_BUCKETS - 1, h]) * LOG2E
    o_ref[0, 0] = jnp.where(bucket < 0, MASK_VALUE, acc)


def _bias_tiles(rel_bias, t):
    buckets = jnp.asarray(_bucket_tiles(t))
    return pl.pallas_call(
        _bias_tiles_kernel,
        grid=(N_HEADS, 2),
        in_specs=[
            pl.BlockSpec(memory_space=pltpu.SMEM),
            pl.BlockSpec((1, t, t), lambda h, d: (d, 0, 0)),
        ],
        out_specs=pl.BlockSpec((1, 1, t, t), lambda h, d: (h, d, 0, 0)),
        out_shape=jax.ShapeDtypeStruct((N_HEADS, 2, t, t), F32),
        compiler_params=_params("parallel", "parallel"),
        name="bias_tiles",
    )(rel_bias, buckets)


def _attn_kernel(lq1_ref, lk1_ref, lq2_ref, lk2_ref, sg_ref, q_ref, k_ref, v_ref, bias_ref,
                 o_ref, m_ref, l_ref, alpha_ref, part_ref, acc_ref, s_ref, p_ref, *, t, lambda_init):
    qi = pl.program_id(2)
    chunks = [slice(j * LANES, (j + 1) * LANES) for j in range(t // LANES)]

    m_ref[...] = jnp.full(m_ref.shape, MASK_VALUE, F32)
    l_ref[...] = jnp.zeros(l_ref.shape, F32)
    acc_ref[...] = jnp.zeros(acc_ref.shape, F32)
    alpha_ref[...] = jnp.zeros(alpha_ref.shape, F32)
    p_ref[...] = jnp.zeros(p_ref.shape, BF16)

    def qk_logits(k0):
        for c in range(2):
            lanes = slice(c * HEAD_DIM, (c + 1) * HEAD_DIM)
            s_ref[c] = lax.dot_general(q_ref[:, lanes], k_ref[pl.ds(k0, t), lanes],
                                       (((1,), (1,)), ((), ())), preferred_element_type=F32)

    def accumulate(k0):
        v = v_ref[pl.ds(k0, t), :]
        for c in range(2):
            alpha = alpha_ref[c]
            pv = jnp.dot(p_ref[c], v, preferred_element_type=F32)
            acc_ref[c] = jnp.concatenate([alpha, alpha], axis=1) * acc_ref[c] + pv

    def block(k0, k0_prev, k0_next, bias_index):
        def logits(c, rows):
            s = s_ref[c, rows, :]
            if bias_index is not None:
                s = s + bias_ref[0, bias_index, rows, :]
            return s

        def strip_rows(r):
            return pl.ds(pl.multiple_of(r * ATTN_STRIP, ATTN_STRIP), ATTN_STRIP)

        def max_strip(r, carry):
            rows = strip_rows(r)
            for c in range(2):
                s = logits(c, rows)
                part = s[:, chunks[0]]
                for ch in chunks[1:]:
                    part = jnp.maximum(part, s[:, ch])
                part_ref[c, rows, :] = part
            return carry
        lax.fori_loop(0, t // ATTN_STRIP, max_strip, 0, unroll=ATTN_UNROLL)

        accumulate(k0_prev)
        for c in range(2):
            m_prev = m_ref[c]
            m_new = jnp.maximum(m_prev, jnp.max(part_ref[c], axis=1, keepdims=True))
            alpha_ref[c] = jnp.exp2(m_prev - m_new)
            m_ref[c] = m_new

        def exp_strip(r, carry):
            rows = strip_rows(r)
            for c in range(2):
                s = logits(c, rows)
                m_new = m_ref[c, rows, :]
                part = None
                for ch in chunks:
                    p = jnp.exp2(s[:, ch] - m_new)
                    part = p if part is None else part + p
                    p_ref[c, rows, ch] = p.astype(BF16)
                part_ref[c, rows, :] = part
            return carry
        lax.fori_loop(0, t // ATTN_STRIP, exp_strip, 0, unroll=ATTN_UNROLL)

        if k0_next is not None:
            qk_logits(k0_next)
        for c in range(2):
            l_ref[c] = alpha_ref[c] * l_ref[c] + jnp.sum(part_ref[c], axis=1, keepdims=True)

    def start(ki):
        return pl.multiple_of(jnp.maximum(ki, 0) * t, t)

    qk_logits(0)

    def far_body(ki, carry):
        block(start(ki), start(ki - 1), start(ki + 1), None)
        return carry
    lax.fori_loop(0, jnp.maximum(qi - 1, 0), far_body, 0)

    @pl.when(qi >= 1)
    def _():
        block(start(qi - 1), start(qi - 2), start(qi), 1)

    block(start(qi), start(qi - 1), None, 0)
    accumulate(start(qi))

    lam = (jnp.exp(jnp.sum(lq1_ref[...] * lk1_ref[...], keepdims=True))
           - jnp.exp(jnp.sum(lq2_ref[...] * lk2_ref[...], keepdims=True)) + lambda_init)
    l0 = l_ref[0]
    l1 = l_ref[1]
    o = (acc_ref[0] / jnp.concatenate([l0, l0], axis=1)
         - lam * (acc_ref[1] / jnp.concatenate([l1, l1], axis=1)))
    ms = jnp.mean(o * o, axis=-1, keepdims=True)
    o = o * lax.rsqrt(ms + 1e-5) * sg_ref[...] * (1.0 - lambda_init)
    o_ref[...] = o.astype(o_ref.dtype)


def _diff_attention(qkv, bias_tiles, lq1, lk1, lq2, lk2, subln_g, *, batch, seq, lambda_init):
    t = ATTN_T
    nq = seq // t
    lam_spec = pl.BlockSpec((1, HEAD_DIM), lambda b, h, i: (0, 0))
    return pl.pallas_call(
        functools.partial(_attn_kernel, t=t, lambda_init=lambda_init),
        grid=(batch, N_HEADS, nq),
        in_specs=[
            lam_spec, lam_spec, lam_spec, lam_spec,
            pl.BlockSpec((1, HEAD_W), lambda b, h, i: (0, 0)),
            pl.BlockSpec((t, HEAD_W), lambda b, h, i: (b * nq + i, h)),
            pl.BlockSpec((seq, HEAD_W), lambda b, h, i: (b, N_HEADS + h)),
            pl.BlockSpec((seq, HEAD_W), lambda b, h, i: (b, 2 * N_HEADS + h)),
            pl.BlockSpec((1, 2, t, t), lambda b, h, i: (h, 0, 0, 0)),
        ],
        out_specs=pl.BlockSpec((t, HEAD_W), lambda b, h, i: (b * nq + i, h)),
        out_shape=jax.ShapeDtypeStruct((batch * seq, D_MODEL), BF16),
        scratch_shapes=[pltpu.VMEM((2, t, LANES), F32), pltpu.VMEM((2, t, LANES), F32),
                        pltpu.VMEM((2, t, LANES), F32), pltpu.VMEM((2, t, LANES), F32),
                        pltpu.VMEM((2, t, HEAD_W), F32), pltpu.VMEM((2, t, t), F32),
                        pltpu.VMEM((2, t, t), BF16)],
        compiler_params=_params("parallel", "parallel", "arbitrary"),
        name="diff_attn",
    )(lq1, lk1, lq2, lk2, subln_g, qkv, qkv, qkv, bias_tiles)


CONV_T = 256
CONV_HALO = 32
CONV_ROWS = 64
CONV_LANES = 256


def _conv_ln_kernel(halo_ref, u_ref, w_ref, b_ref, g_ref, beta_ref, o_ref, sh_ref, y_ref, *, tiles_per_seq):
    i = pl.program_id(0)
    first = (i % tiles_per_seq) == 0

    @pl.when(first)
    def _():
        sh_ref[0, 0:CONV_HALO, :] = jnp.zeros((CONV_HALO, D_MODEL), F32)

    @pl.when(jnp.logical_not(first))
    def _():
        sh_ref[0, 0:CONV_HALO, :] = halo_ref[...]

    sh_ref[0, CONV_HALO:, :] = u_ref[...]
    window = CONV_HALO + CONV_T
    for s in range(1, SUBLANES):
        for c0 in range(0, D_MODEL, CONV_LANES):
            lanes = slice(c0, c0 + CONV_LANES)
            sh_ref[s, 0:window - SUBLANES, lanes] = sh_ref[0, s:s + window - SUBLANES, lanes]

    base = CONV_HALO - (CONV_WIDTH - 1)
    for r0 in range(0, CONV_T, CONV_ROWS):
        for c0 in range(0, D_MODEL, CONV_LANES):
            lanes = slice(c0, c0 + CONV_LANES)
            acc = jnp.zeros((CONV_ROWS, CONV_LANES), F32)
            for j in range(CONV_WIDTH):
                a, s = divmod(base + j, SUBLANES)
                x0 = r0 + a * SUBLANES
                acc = acc + w_ref[j:j + 1, lanes] * sh_ref[s, x0:x0 + CONV_ROWS, lanes]
            y_ref[r0:r0 + CONV_ROWS, lanes] = acc + b_ref[:, lanes]

    def ln_body(c, carry):
        r0 = pl.multiple_of(c * CONV_ROWS, CONV_ROWS)
        y = y_ref[pl.ds(r0, CONV_ROWS), :]
        mu = jnp.mean(y, axis=-1, keepdims=True)
        yc = y - mu
        var = jnp.mean(yc * yc, axis=-1, keepdims=True)
        z = yc * lax.rsqrt(var + 1e-5) * g_ref[...] + beta_ref[...]
        o_ref[pl.ds(r0, CONV_ROWS), :] = (z * _sigmoid(z)).astype(o_ref.dtype)
        return carry
    lax.fori_loop(0, CONV_T // CONV_ROWS, ln_body, 0)


def _conv_ln(u, dw_w, dw_b, ln_g, ln_b, *, seq):
    m, d = u.shape
    tiles_per_seq = seq // CONV_T
    ratio = CONV_T // CONV_HALO
    row = lambda i: (0, 0)
    return pl.pallas_call(
        functools.partial(_conv_ln_kernel, tiles_per_seq=tiles_per_seq),
        grid=(m // CONV_T,),
        in_specs=[
            pl.BlockSpec((CONV_HALO, d), lambda i: (jnp.maximum(i * ratio - 1, 0), 0)),
            pl.BlockSpec((CONV_T, d), lambda i: (i, 0)),
            pl.BlockSpec((CONV_WIDTH, d), row),
            pl.BlockSpec((1, d), row), pl.BlockSpec((1, d), row), pl.BlockSpec((1, d), row),
        ],
        out_specs=pl.BlockSpec((CONV_T, d), lambda i: (i, 0)),
        out_shape=jax.ShapeDtypeStruct((m, d), BF16),
        scratch_shapes=[pltpu.VMEM((SUBLANES, CONV_HALO + CONV_T, d), F32), pltpu.VMEM((CONV_T, d), F32)],
        compiler_params=_params("parallel"),
        name="conv_ln",
    )(u, u, dw_w, dw_b, ln_g, ln_b)


LRU_T = 256
LRU_HALO = 8


def _lru_kernel(halo_ref, u_ref, y_ref, cw_ref, cb_ref, wa_ref, ba_ref, wi_ref, bi_ref, lam_ref,
                o_ref, buf_ref, carry_ref):
    ti = pl.program_id(2)
    first = ti == 0
    buf_ref[LRU_HALO:, :] = u_ref[...]

    @pl.when(first)
    def _():
        buf_ref[0:LRU_HALO, :] = jnp.zeros((LRU_HALO, LRU_BLOCK_SIZE), F32)
        carry_ref[...] = jnp.zeros(carry_ref.shape, F32)

    @pl.when(jnp.logical_not(first))
    def _():
        buf_ref[0:LRU_HALO, :] = halo_ref[...]

    base = LRU_HALO - (LRU_CONV_WIDTH - 1)
    uc = jnp.zeros((LRU_T, LRU_BLOCK_SIZE), F32)
    for j in range(LRU_CONV_WIDTH):
        uc = uc + cw_ref[j:j + 1, :] * buf_ref[base + j:base + j + LRU_T, :]
    uc = uc + cb_ref[...]

    ucb = uc.astype(BF16)
    r = _sigmoid(jnp.dot(ucb, wa_ref[0].astype(BF16), preferred_element_type=F32) + ba_ref[...])
    gate_i = _sigmoid(jnp.dot(ucb, wi_ref[0].astype(BF16), preferred_element_type=F32) + bi_ref[...])
    neg_lam = -lam_ref[...]
    softplus = jnp.maximum(neg_lam, 0.0) + jnp.log1p(jnp.exp(-jnp.abs(neg_lam)))
    log_a = -LRU_C * r * softplus
    a = jnp.exp(log_a)
    mult = jnp.sqrt(-jnp.tanh(log_a) * (1.0 + a * a))
    b = mult * (gate_i * uc)

    row = lax.broadcasted_iota(jnp.int32, (LRU_T, LRU_BLOCK_SIZE), 0)
    s = 1
    while s < SUBLANES:
        valid = row >= s
        a_sh = jnp.where(valid, pltpu.roll(a, s, 0), 1.0)
        b_sh = jnp.where(valid, pltpu.roll(b, s, 0), 0.0)
        b = a * b_sh + b
        a = a * a_sh
        s *= 2
    while s < LRU_T:
        b = jnp.concatenate([b[:s], a[s:] * b[:-s] + b[s:]], axis=0)
        a = jnp.concatenate([a[:s], a[s:] * a[:-s]], axis=0)
        s *= 2
    hs = a * carry_ref[0:1, :] + b
    carry_ref[0:1, :] = hs[LRU_T - 1:LRU_T, :]
    o_ref[...] = (hs * y_ref[...]).astype(o_ref.dtype)


def _lru_core(u, y, conv_w, conv_b, w_a, b_a, w_i, b_i, lam, *, batch, seq):
    m, d = u.shape
    nt = seq // LRU_T
    ratio = LRU_T // LRU_HALO
    c = LRU_BLOCK_SIZE
    chan = lambda b, g, t: (0, g)
    tile = lambda b, g, t: (b * nt + t, g)
    gate_w = pl.BlockSpec((1, c, c), lambda b, g, t: (g, 0, 0))
    return pl.pallas_call(
        _lru_kernel,
        grid=(batch, LRU_BLOCKS, nt),
        in_specs=[
            pl.BlockSpec((LRU_HALO, c), lambda b, g, t: (jnp.maximum((b * nt + t) * ratio - 1, 0), g)),
            pl.BlockSpec((LRU_T, c), tile),
            pl.BlockSpec((LRU_T, c), tile),
            pl.BlockSpec((LRU_CONV_WIDTH, c), chan),
            pl.BlockSpec((1, c), chan),
            gate_w, pl.BlockSpec((1, c), chan),
            gate_w, pl.BlockSpec((1, c), chan),
            pl.BlockSpec((1, c), chan),
        ],
        out_specs=pl.BlockSpec((LRU_T, c), tile),
        out_shape=jax.ShapeDtypeStruct((m, d), BF16),
        scratch_shapes=[pltpu.VMEM((LRU_HALO + LRU_T, c), F32), pltpu.VMEM((8, c), F32)],
        compiler_params=_params("parallel", "parallel", "arbitrary"),
        name="lru_core",
    )(u, u, y, conv_w, conv_b, w_a, b_a, w_i, b_i, lam)


def kernel(x, rel_bias, mixer_norm_g, attn_w_qkv, attn_lq1, attn_lk1, attn_lq2, attn_lk2, attn_subln_g, attn_w_o, conv_w_in, conv_b_in, conv_dw_w, conv_dw_b, conv_ln_g, conv_ln_b, conv_w_out, conv_b_out, lru_w_in, lru_conv_w, lru_conv_b, lru_w_a, lru_b_a, lru_w_i, lru_b_i, lru_lambda, lru_w_out, mlp_norm_g, mlp_w1, mlp_w2, final_norm_g):
    batch, seq, d = x.shape
    h = x.reshape(batch * seq, d)
    row = lambda v: v.reshape(1, -1)
    zero_bias = jnp.zeros((1, d), F32)
    q_scale = jnp.concatenate([jnp.full((1, d), HEAD_DIM ** -0.5 * LOG2E, F32), jnp.ones((1, 2 * d), F32)], axis=1)
    bias_tiles = _bias_tiles(rel_bias, ATTN_T)
    mlp_w1_bf16 = mlp_w1.astype(BF16)
    mlp_w2_bf16 = mlp_w2.astype(BF16)

    ia = ic = il = 0
    for layer in range(DEPTH):
        kind = layer % N_MIXERS
        g = row(mixer_norm_g[layer])
        if kind == 0:
            lambda_init = 0.8 - 0.6 * math.exp(-0.3 * layer)
            qkv = _norm_proj(h, g, attn_w_qkv, ia, q_scale)
            att = _diff_attention(qkv, bias_tiles, row(attn_lq1[ia]), row(attn_lk1[ia]),
                                  row(attn_lq2[ia]), row(attn_lk2[ia]), row(attn_subln_g[ia]),
                                  batch=batch, seq=seq, lambda_init=lambda_init)
            h = _proj_res(att, attn_w_o, ia, zero_bias, h)
            ia += 1
        elif kind == 1:
            u = _norm_glu(h, g, conv_w_in, ic, row(conv_b_in[ic]))
            act = _conv_ln(u, conv_dw_w[ic], row(conv_dw_b[ic]), row(conv_ln_g[ic]), row(conv_ln_b[ic]), seq=seq)
            h = _proj_res(act, conv_w_out, ic, row(conv_b_out[ic]), h)
            ic += 1
        else:
            u, y = _norm_lru_in(h, g, lru_w_in, il)
            hy = _lru_core(u, y, lru_conv_w[il], row(lru_conv_b[il]), lru_w_a[il], row(lru_b_a[il]),
                           lru_w_i[il], row(lru_b_i[il]), row(lru_lambda[il]), batch=batch, seq=seq)
            h = _proj_res(hy, lru_w_out, il, zero_bias, h)
            il += 1
        final_g = row(final_norm_g) if layer == DEPTH - 1 else None
        h = _mlp(h, row(mlp_norm_g[layer]), mlp_w1_bf16, mlp_w2_bf16, layer, final_g)
    return h.reshape(batch, seq, d)
```

```python
import functools
import math

import numpy as np
import jax
import jax.numpy as jnp
from jax import lax
from jax.experimental import pallas as pl
from jax.experimental.pallas import tpu as pltpu

F32 = jnp.float32
BF16 = jnp.bfloat16

D_MODEL = 2048
DEPTH = 4
N_MIXERS = 3
HEAD_DIM = 128
HEAD_W = 2 * HEAD_DIM
N_HEADS = D_MODEL // HEAD_W
NUM_BUCKETS = 32
MAX_DISTANCE = 128
MASK_VALUE = -1e30
CONV_WIDTH = 31
LRU_BLOCKS = 8
LRU_BLOCK_SIZE = D_MODEL // LRU_BLOCKS
LRU_CONV_WIDTH = 4
LRU_C = 8.0
D_FF = 4 * D_MODEL

VMEM_LIMIT = 56 * 1024 * 1024
NORM_ROWS = 128
LANES = 128
SUBLANES = 8


def _params(*sem):
    return pltpu.CompilerParams(dimension_semantics=sem, vmem_limit_bytes=VMEM_LIMIT)


def _rms_rows_to(dst_ref, x_ref, g_ref, rows, eps):
    def body(c, carry):
        r0 = pl.multiple_of(c * NORM_ROWS, NORM_ROWS)
        x = x_ref[pl.ds(r0, NORM_ROWS), :]
        ms = jnp.mean(x * x, axis=-1, keepdims=True)
        dst_ref[pl.ds(r0, NORM_ROWS), :] = (x * lax.rsqrt(ms + eps) * g_ref[...]).astype(BF16)
        return carry
    lax.fori_loop(0, rows // NORM_ROWS, body, 0)


def _sigmoid(x):
    return 1.0 / (1.0 + jnp.exp(-x))


def _gelu_tanh(x):
    c = math.sqrt(2.0 / math.pi)
    return 0.5 * x * (1.0 + jnp.tanh(c * (x + 0.044715 * (x * x * x))))


def _norm_proj_kernel(x_ref, g_ref, w_ref, s_ref, o_ref, xn_ref, *, tm):
    @pl.when(pl.program_id(1) == 0)
    def _():
        _rms_rows_to(xn_ref, x_ref, g_ref, tm, 1e-6)
    acc = jnp.dot(xn_ref[...], w_ref[...].astype(BF16), preferred_element_type=F32)
    o_ref[...] = (acc * s_ref[...]).astype(o_ref.dtype)


def _norm_proj(x, g, w, layer, col_scale, *, tm=1024, tn=512):
    m, k = x.shape
    n = w.shape[2]
    return pl.pallas_call(
        functools.partial(_norm_proj_kernel, tm=tm),
        grid=(m // tm, n // tn),
        in_specs=[
            pl.BlockSpec((tm, k), lambda i, j: (i, 0)),
            pl.BlockSpec((1, k), lambda i, j: (0, 0)),
            pl.BlockSpec((None, k, tn), lambda i, j: (layer, 0, j)),
            pl.BlockSpec((1, tn), lambda i, j: (0, j)),
        ],
        out_specs=pl.BlockSpec((tm, tn), lambda i, j: (i, j)),
        out_shape=jax.ShapeDtypeStruct((m, n), BF16),
        scratch_shapes=[pltpu.VMEM((tm, k), BF16)],
        compiler_params=_params("parallel", "arbitrary"),
        name="norm_proj",
    )(x, g, w, col_scale)


def _norm_glu_kernel(x_ref, g_ref, wa_ref, wg_ref, ba_ref, bg_ref, o_ref, xn_ref, *, tm):
    @pl.when(pl.program_id(1) == 0)
    def _():
        _rms_rows_to(xn_ref, x_ref, g_ref, tm, 1e-6)
    xn = xn_ref[...]
    a = jnp.dot(xn, wa_ref[...].astype(BF16), preferred_element_type=F32) + ba_ref[...]
    gate = jnp.dot(xn, wg_ref[...].astype(BF16), preferred_element_type=F32) + bg_ref[...]
    o_ref[...] = a * _sigmoid(gate)


def _norm_glu(x, g, w, layer, b, *, tm=1024, tn=512):
    m, k = x.shape
    half = w.shape[2] // 2
    nj = half // tn
    return pl.pallas_call(
        functools.partial(_norm_glu_kernel, tm=tm),
        grid=(m // tm, nj),
        in_specs=[
            pl.BlockSpec((tm, k), lambda i, j: (i, 0)),
            pl.BlockSpec((1, k), lambda i, j: (0, 0)),
            pl.BlockSpec((None, k, tn), lambda i, j: (layer, 0, j)),
            pl.BlockSpec((None, k, tn), lambda i, j: (layer, 0, j + nj)),
            pl.BlockSpec((1, tn), lambda i, j: (0, j)),
            pl.BlockSpec((1, tn), lambda i, j: (0, j + nj)),
        ],
        out_specs=pl.BlockSpec((tm, tn), lambda i, j: (i, j)),
        out_shape=jax.ShapeDtypeStruct((m, half), F32),
        scratch_shapes=[pltpu.VMEM((tm, k), BF16)],
        compiler_params=_params("parallel", "arbitrary"),
        name="norm_glu",
    )(x, g, w, w, b, b)


def _norm_lru_in_kernel(x_ref, g_ref, wu_ref, wy_ref, u_ref, y_ref, xn_ref, *, tm):
    @pl.when(pl.program_id(1) == 0)
    def _():
        _rms_rows_to(xn_ref, x_ref, g_ref, tm, 1e-6)
    xn = xn_ref[...]
    u_ref[...] = jnp.dot(xn, wu_ref[...].astype(BF16), preferred_element_type=F32)
    y_ref[...] = _gelu_tanh(jnp.dot(xn, wy_ref[...].astype(BF16), preferred_element_type=F32))


def _norm_lru_in(x, g, w, layer, *, tm=1024, tn=512):
    m, k = x.shape
    half = w.shape[2] // 2
    nj = half // tn
    return pl.pallas_call(
        functools.partial(_norm_lru_in_kernel, tm=tm),
        grid=(m // tm, nj),
        in_specs=[
            pl.BlockSpec((tm, k), lambda i, j: (i, 0)),
            pl.BlockSpec((1, k), lambda i, j: (0, 0)),
            pl.BlockSpec((None, k, tn), lambda i, j: (layer, 0, j)),
            pl.BlockSpec((None, k, tn), lambda i, j: (layer, 0, j + nj)),
        ],
        out_specs=[pl.BlockSpec((tm, tn), lambda i, j: (i, j)),
                   pl.BlockSpec((tm, tn), lambda i, j: (i, j))],
        out_shape=[jax.ShapeDtypeStruct((m, half), F32), jax.ShapeDtypeStruct((m, half), F32)],
        scratch_shapes=[pltpu.VMEM((tm, k), BF16)],
        compiler_params=_params("parallel", "arbitrary"),
        name="norm_lru_in",
    )(x, g, w, w)


def _proj_res_kernel(a_ref, w_ref, b_ref, r_ref, o_ref, wb_ref):
    @pl.when(pl.program_id(1) == 0)
    def _():
        wb_ref[...] = w_ref[...].astype(BF16)
    acc = jnp.dot(a_ref[...], wb_ref[...], preferred_element_type=F32)
    o_ref[...] = r_ref[...] + (acc + b_ref[...])


def _proj_res(a, w, layer, b, res, *, tm=1024, tn=1024):
    m, k = a.shape
    n = w.shape[2]
    return pl.pallas_call(
        _proj_res_kernel,
        grid=(n // tn, m // tm),
        in_specs=[
            pl.BlockSpec((tm, k), lambda j, i: (i, 0)),
            pl.BlockSpec((None, k, tn), lambda j, i: (layer, 0, j)),
            pl.BlockSpec((1, tn), lambda j, i: (0, j)),
            pl.BlockSpec((tm, tn), lambda j, i: (i, j)),
        ],
        out_specs=pl.BlockSpec((tm, tn), lambda j, i: (i, j)),
        out_shape=jax.ShapeDtypeStruct((m, n), F32),
        scratch_shapes=[pltpu.VMEM((k, tn), BF16)],
        compiler_params=_params("parallel", "arbitrary"),
        name="proj_res",
    )(a, w, b, res)


def _mlp_kernel(*refs, tm, final_norm):
    if final_norm:
        x_ref, g_ref, w1_ref, w2_ref, fg_ref, o_ref, xn_ref = refs
    else:
        x_ref, g_ref, w1_ref, w2_ref, o_ref, xn_ref = refs
    f = pl.program_id(1)

    @pl.when(f == 0)
    def _():
        _rms_rows_to(xn_ref, x_ref, g_ref, tm, 1e-6)

    z = jnp.maximum(jnp.dot(xn_ref[...], w1_ref[...], preferred_element_type=F32), 0.0)
    part = jnp.dot((z * z).astype(BF16), w2_ref[...], preferred_element_type=F32)

    @pl.when(f == 0)
    def _():
        o_ref[...] = x_ref[...] + part

    @pl.when(f != 0)
    def _():
        o_ref[...] += part

    if final_norm:
        @pl.when(f == pl.num_programs(1) - 1)
        def _():
            def body(c, carry):
                rows = pl.ds(pl.multiple_of(c * NORM_ROWS, NORM_ROWS), NORM_ROWS)
                y = o_ref[rows, :]
                ms = jnp.mean(y * y, axis=-1, keepdims=True)
                o_ref[rows, :] = y * lax.rsqrt(ms + 1e-6) * fg_ref[...]
                return carry
            lax.fori_loop(0, tm // NORM_ROWS, body, 0)


def _mlp(x, g, w1, w2, layer, final_g=None, *, tm=512, tf=1024):
    m, d = x.shape
    ff = w1.shape[2]
    final_norm = final_g is not None
    row_spec = pl.BlockSpec((1, d), lambda i, f: (0, 0))
    return pl.pallas_call(
        functools.partial(_mlp_kernel, tm=tm, final_norm=final_norm),
        grid=(m // tm, ff // tf),
        in_specs=[
            pl.BlockSpec((tm, d), lambda i, f: (i, 0)),
            row_spec,
            pl.BlockSpec((None, d, tf), lambda i, f: (layer, 0, f)),
            pl.BlockSpec((None, tf, d), lambda i, f: (layer, f, 0)),
        ] + ([row_spec] if final_norm else []),
        out_specs=pl.BlockSpec((tm, d), lambda i, f: (i, 0)),
        out_shape=jax.ShapeDtypeStruct((m, d), F32),
        scratch_shapes=[pltpu.VMEM((tm, d), BF16)],
        compiler_params=_params("parallel", "arbitrary"),
        name="mlp",
    )(x, g, w1, w2, *((final_g,) if final_norm else ()))


ATTN_T = 512
ATTN_STRIP = 32
LOG2E = math.log2(math.e)


def _bucket_tiles(t):
    r = np.arange(t, dtype=np.int32)[:, None]
    c = np.arange(t, dtype=np.int32)[None, :]
    tiles = []
    for d in range(2):
        n = np.maximum(d * t + r - c, 0)
        max_exact = NUM_BUCKETS // 2
        nf = np.maximum(n, 1).astype(np.float32)
        large = max_exact + (np.log(nf / np.float32(max_exact)) / np.float32(math.log(MAX_DISTANCE / max_exact))
                             * np.float32(NUM_BUCKETS - max_exact)).astype(np.int32)
        large = np.minimum(large, NUM_BUCKETS - 1)
        bucket = np.where(n < max_exact, n, large).astype(np.int32)
        tiles.append(np.where(d * t + r - c >= 0, bucket, -1))
    return np.stack(tiles)


def _bias_tiles_kernel(rb_ref, bucket_ref, o_ref):
    h = pl.program_id(0)
    bucket = bucket_ref[0]
    acc = jnp.zeros(bucket.shape, F32)
    for j in range(NUM_BUCKETS):
        acc = jnp.where(bucket == j, rb_ref[j, h], acc)
    acc = (acc - rb_ref[NUM_BUCKETS - 1, h]) * LOG2E
    o_ref[0, 0] = jnp.where(bucket < 0, MASK_VALUE, acc)


def _bias_tiles(rel_bias, t):
    buckets = jnp.asarray(_bucket_tiles(t))
    return pl.pallas_call(
        _bias_tiles_kernel,
        grid=(N_HEADS, 2),
        in_specs=[
            pl.BlockSpec(memory_space=pltpu.SMEM),
            pl.BlockSpec((1, t, t), lambda h, d: (d, 0, 0)),
        ],
        out_specs=pl.BlockSpec((1, 1, t, t), lambda h, d: (h, d, 0, 0)),
        out_shape=jax.ShapeDtypeStruct((N_HEADS, 2, t, t), F32),
        compiler_params=_params("parallel", "parallel"),
        name="bias_tiles",
    )(rel_bias, buckets)


def _attn_kernel(lq1_ref, lk1_ref, lq2_ref, lk2_ref, sg_ref, q_ref, k_ref, v_ref, bias_ref,
                 o_ref, m_ref, l_ref, alpha_ref, acc_ref, s_ref, p_ref, *, t, lambda_init):
    qi = pl.program_id(2)
    chunks = [slice(j * LANES, (j + 1) * LANES) for j in range(t // LANES)]

    m_ref[...] = jnp.full(m_ref.shape, MASK_VALUE, F32)
    l_ref[...] = jnp.zeros(l_ref.shape, F32)
    acc_ref[...] = jnp.zeros(acc_ref.shape, F32)
    alpha_ref[...] = jnp.zeros(alpha_ref.shape, F32)
    p_ref[...] = jnp.zeros(p_ref.shape, BF16)

    def start(ki):
        return pl.multiple_of(jnp.maximum(ki, 0) * t, t)

    def qk_logits(ki, slot):
        for c in range(2):
            lanes = slice(c * HEAD_DIM, (c + 1) * HEAD_DIM)
            s_ref[slot, c] = lax.dot_general(q_ref[:, lanes], k_ref[pl.ds(start(ki), t), lanes],
                                             (((1,), (1,)), ((), ())), preferred_element_type=F32)

    def accumulate(ki, slot):
        v = v_ref[pl.ds(start(ki), t), :]
        for c in range(2):
            alpha = alpha_ref[slot, c]
            pv = jnp.dot(p_ref[slot, c], v, preferred_element_type=F32)
            acc_ref[c] = jnp.concatenate([alpha, alpha], axis=1) * acc_ref[c] + pv

    strips = [slice(r * ATTN_STRIP, (r + 1) * ATTN_STRIP) for r in range(t // ATTN_STRIP)]

    def block(ki, bias_index, has_next):
        cur = ki % 2
        other = 1 - cur

        def logits(c, rows):
            s = s_ref[cur, c, rows, :]
            if bias_index is not None:
                s = s + bias_ref[0, bias_index, rows, :]
            return s

        part_max = [[], []]
        for rows in strips:
            for c in range(2):
                s = logits(c, rows)
                part = s[:, chunks[0]]
                for ch in chunks[1:]:
                    part = jnp.maximum(part, s[:, ch])
                part_max[c].append(part)

        accumulate(ki - 1, other)
        m_new, alpha = [], []
        for c in range(2):
            m_prev = m_ref[c]
            m_c = jnp.maximum(m_prev, jnp.max(jnp.concatenate(part_max[c], axis=0), axis=1, keepdims=True))
            alpha.append(jnp.exp2(m_prev - m_c))
            alpha_ref[cur, c] = alpha[c]
            m_ref[c] = m_c
            m_new.append(m_c)

        part_sum = [[], []]
        for rows in strips:
            for c in range(2):
                s = logits(c, rows)
                m_rows = m_new[c][rows, :]
                part = None
                for ch in chunks:
                    p = jnp.exp2(s[:, ch] - m_rows)
                    part = p if part is None else part + p
                    p_ref[cur, c, rows, ch] = p.astype(BF16)
                part_sum[c].append(part)

        if has_next:
            qk_logits(ki + 1, other)
        for c in range(2):
            row_sum = jnp.sum(jnp.concatenate(part_sum[c], axis=0), axis=1, keepdims=True)
            l_ref[c] = alpha[c] * l_ref[c] + row_sum

    qk_logits(0, 0)

    def far_body(ki, carry):
        block(ki, None, True)
        return carry
    lax.fori_loop(0, jnp.maximum(qi - 1, 0), far_body, 0)

    @pl.when(qi >= 1)
    def _():
        block(qi - 1, 1, True)

    block(qi, 0, False)
    accumulate(qi, qi % 2)

    lam = (jnp.exp(jnp.sum(lq1_ref[...] * lk1_ref[...], keepdims=True))
           - jnp.exp(jnp.sum(lq2_ref[...] * lk2_ref[...], keepdims=True)) + lambda_init)
    l0 = l_ref[0]
    l1 = l_ref[1]
    o = (acc_ref[0] / jnp.concatenate([l0, l0], axis=1)
         - lam * (acc_ref[1] / jnp.concatenate([l1, l1], axis=1)))
    ms = jnp.mean(o * o, axis=-1, keepdims=True)
    o = o * lax.rsqrt(ms + 1e-5) * sg_ref[...] * (1.0 - lambda_init)
    o_ref[...] = o.astype(o_ref.dtype)


def _diff_attention(qkv, bias_tiles, lq1, lk1, lq2, lk2, subln_g, *, batch, seq, lambda_init):
    t = ATTN_T
    nq = seq // t
    lam_spec = pl.BlockSpec((1, HEAD_DIM), lambda b, h, i: (0, 0))
    return pl.pallas_call(
        functools.partial(_attn_kernel, t=t, lambda_init=lambda_init),
        grid=(batch, N_HEADS, nq),
        in_specs=[
            lam_spec, lam_spec, lam_spec, lam_spec,
            pl.BlockSpec((1, HEAD_W), lambda b, h, i: (0, 0)),
            pl.BlockSpec((t, HEAD_W), lambda b, h, i: (b * nq + i, h)),
            pl.BlockSpec((seq, HEAD_W), lambda b, h, i: (b, N_HEADS + h)),
            pl.BlockSpec((seq, HEAD_W), lambda b, h, i: (b, 2 * N_HEADS + h)),
            pl.BlockSpec((1, 2, t, t), lambda b, h, i: (h, 0, 0, 0)),
        ],
        out_specs=pl.BlockSpec((t, HEAD_W), lambda b, h, i: (b * nq + i, h)),
        out_shape=jax.ShapeDtypeStruct((batch * seq, D_MODEL), BF16),
        scratch_shapes=[pltpu.VMEM((2, t, LANES), F32), pltpu.VMEM((2, t, LANES), F32),
                        pltpu.VMEM((2, 2, t, LANES), F32),
                        pltpu.VMEM((2, t, HEAD_W), F32), pltpu.VMEM((2, 2, t, t), F32),
                        pltpu.VMEM((2, 2, t, t), BF16)],
        compiler_params=_params("parallel", "parallel", "arbitrary"),
        name="diff_attn",
    )(lq1, lk1, lq2, lk2, subln_g, qkv, qkv, qkv, bias_tiles)


CONV_T = 256
CONV_HALO = 32
CONV_ROWS = 64
CONV_LANES = 256


def _conv_ln_kernel(halo_ref, u_ref, w_ref, b_ref, g_ref, beta_ref, o_ref, sh_ref, y_ref, *, tiles_per_seq):
    i = pl.program_id(0)
    first = (i % tiles_per_seq) == 0

    @pl.when(first)
    def _():
        sh_ref[0, 0:CONV_HALO, :] = jnp.zeros((CONV_HALO, D_MODEL), F32)

    @pl.when(jnp.logical_not(first))
    def _():
        sh_ref[0, 0:CONV_HALO, :] = halo_ref[...]

    sh_ref[0, CONV_HALO:, :] = u_ref[...]
    window = CONV_HALO + CONV_T
    for s in range(1, SUBLANES):
        for c0 in range(0, D_MODEL, CONV_LANES):
            lanes = slice(c0, c0 + CONV_LANES)
            sh_ref[s, 0:window - SUBLANES, lanes] = sh_ref[0, s:s + window - SUBLANES, lanes]

    base = CONV_HALO - (CONV_WIDTH - 1)
    for r0 in range(0, CONV_T, CONV_ROWS):
        for c0 in range(0, D_MODEL, CONV_LANES):
            lanes = slice(c0, c0 + CONV_LANES)
            acc = jnp.zeros((CONV_ROWS, CONV_LANES), F32)
            for j in range(CONV_WIDTH):
                a, s = divmod(base + j, SUBLANES)
                x0 = r0 + a * SUBLANES
                acc = acc + w_ref[j:j + 1, lanes] * sh_ref[s, x0:x0 + CONV_ROWS, lanes]
            y_ref[r0:r0 + CONV_ROWS, lanes] = acc + b_ref[:, lanes]

    def ln_body(c, carry):
        r0 = pl.multiple_of(c * CONV_ROWS, CONV_ROWS)
        y = y_ref[pl.ds(r0, CONV_ROWS), :]
        mu = jnp.mean(y, axis=-1, keepdims=True)
        yc = y - mu
        var = jnp.mean(yc * yc, axis=-1, keepdims=True)
        z = yc * lax.rsqrt(var + 1e-5) * g_ref[...] + beta_ref[...]
        o_ref[pl.ds(r0, CONV_ROWS), :] = (z * _sigmoid(z)).astype(o_ref.dtype)
        return carry
    lax.fori_loop(0, CONV_T // CONV_ROWS, ln_body, 0)


def _conv_ln(u, dw_w, dw_b, ln_g, ln_b, *, seq):
    m, d = u.shape
    tiles_per_seq = seq // CONV_T
    ratio = CONV_T // CONV_HALO
    row = lambda i: (0, 0)
    return pl.pallas_call(
        functools.partial(_conv_ln_kernel, tiles_per_seq=tiles_per_seq),
        grid=(m // CONV_T,),
        in_specs=[
            pl.BlockSpec((CONV_HALO, d), lambda i: (jnp.maximum(i * ratio - 1, 0), 0)),
            pl.BlockSpec((CONV_T, d), lambda i: (i, 0)),
            pl.BlockSpec((CONV_WIDTH, d), row),
            pl.BlockSpec((1, d), row), pl.BlockSpec((1, d), row), pl.BlockSpec((1, d), row),
        ],
        out_specs=pl.BlockSpec((CONV_T, d), lambda i: (i, 0)),
        out_shape=jax.ShapeDtypeStruct((m, d), BF16),
        scratch_shapes=[pltpu.VMEM((SUBLANES, CONV_HALO + CONV_T, d), F32), pltpu.VMEM((CONV_T, d), F32)],
        compiler_params=_params("parallel"),
        name="conv_ln",
    )(u, u, dw_w, dw_b, ln_g, ln_b)


LRU_T = 256
LRU_HALO = 8


def _lru_kernel(halo_ref, u_ref, y_ref, cw_ref, cb_ref, wa_ref, ba_ref, wi_ref, bi_ref, lam_ref,
                o_ref, buf_ref, carry_ref):
    ti = pl.program_id(2)
    first = ti == 0
    buf_ref[LRU_HALO:, :] = u_ref[...]

    @pl.when(first)
    def _():
        buf_ref[0:LRU_HALO, :] = jnp.zeros((LRU_HALO, LRU_BLOCK_SIZE), F32)
        carry_ref[...] = jnp.zeros(carry_ref.shape, F32)

    @pl.when(jnp.logical_not(first))
    def _():
        buf_ref[0:LRU_HALO, :] = halo_ref[...]

    base = LRU_HALO - (LRU_CONV_WIDTH - 1)
    uc = jnp.zeros((LRU_T, LRU_BLOCK_SIZE), F32)
    for j in range(LRU_CONV_WIDTH):
        uc = uc + cw_ref[j:j + 1, :] * buf_ref[base + j:base + j + LRU_T, :]
    uc = uc + cb_ref[...]

    ucb = uc.astype(BF16)
    r = _sigmoid(jnp.dot(ucb, wa_ref[0].astype(BF16), preferred_element_type=F32) + ba_ref[...])
    gate_i = _sigmoid(jnp.dot(ucb, wi_ref[0].astype(BF16), preferred_element_type=F32) + bi_ref[...])
    neg_lam = -lam_ref[...]
    softplus = jnp.maximum(neg_lam, 0.0) + jnp.log1p(jnp.exp(-jnp.abs(neg_lam)))
    log_a = -LRU_C * r * softplus
    a = jnp.exp(log_a)
    mult = jnp.sqrt(-jnp.tanh(log_a) * (1.0 + a * a))
    b = mult * (gate_i * uc)

    row = lax.broadcasted_iota(jnp.int32, (LRU_T, LRU_BLOCK_SIZE), 0)
    s = 1
    while s < SUBLANES:
        valid = row >= s
        a_sh = jnp.where(valid, pltpu.roll(a, s, 0), 1.0)
        b_sh = jnp.where(valid, pltpu.roll(b, s, 0), 0.0)
        b = a * b_sh + b
        a = a * a_sh
        s *= 2
    while s < LRU_T:
        b = jnp.concatenate([b[:s], a[s:] * b[:-s] + b[s:]], axis=0)
        a = jnp.concatenate([a[:s], a[s:] * a[:-s]], axis=0)
        s *= 2
    hs = a * carry_ref[0:1, :] + b
    carry_ref[0:1, :] = hs[LRU_T - 1:LRU_T, :]
    o_ref[...] = (hs * y_ref[...]).astype(o_ref.dtype)


def _lru_core(u, y, conv_w, conv_b, w_a, b_a, w_i, b_i, lam, *, batch, seq):
    m, d = u.shape
    nt = seq // LRU_T
    ratio = LRU_T // LRU_HALO
    c = LRU_BLOCK_SIZE
    chan = lambda b, g, t: (0, g)
    tile = lambda b, g, t: (b * nt + t, g)
    gate_w = pl.BlockSpec((1, c, c), lambda b, g, t: (g, 0, 0))
    return pl.pallas_call(
        _lru_kernel,
        grid=(batch, LRU_BLOCKS, nt),
        in_specs=[
            pl.BlockSpec((LRU_HALO, c), lambda b, g, t: (jnp.maximum((b * nt + t) * ratio - 1, 0), g)),
            pl.BlockSpec((LRU_T, c), tile),
            pl.BlockSpec((LRU_T, c), tile),
            pl.BlockSpec((LRU_CONV_WIDTH, c), chan),
            pl.BlockSpec((1, c), chan),
            gate_w, pl.BlockSpec((1, c), chan),
            gate_w, pl.BlockSpec((1, c), chan),
            pl.BlockSpec((1, c), chan),
        ],
        out_specs=pl.BlockSpec((LRU_T, c), tile),
        out_shape=jax.ShapeDtypeStruct((m, d), BF16),
        scratch_shapes=[pltpu.VMEM((LRU_HALO + LRU_T, c), F32), pltpu.VMEM((8, c), F32)],
        compiler_params=_params("parallel", "parallel", "arbitrary"),
        name="lru_core",
    )(u, u, y, conv_w, conv_b, w_a, b_a, w_i, b_i, lam)


def kernel(x, rel_bias, mixer_norm_g, attn_w_qkv, attn_lq1, attn_lk1, attn_lq2, attn_lk2, attn_subln_g, attn_w_o, conv_w_in, conv_b_in, conv_dw_w, conv_dw_b, conv_ln_g, conv_ln_b, conv_w_out, conv_b_out, lru_w_in, lru_conv_w, lru_conv_b, lru_w_a, lru_b_a, lru_w_i, lru_b_i, lru_lambda, lru_w_out, mlp_norm_g, mlp_w1, mlp_w2, final_norm_g):
    batch, seq, d = x.shape
    h = x.reshape(batch * seq, d)
    row = lambda v: v.reshape(1, -1)
    zero_bias = jnp.zeros((1, d), F32)
    q_scale = jnp.concatenate([jnp.full((1, d), HEAD_DIM ** -0.5 * LOG2E, F32), jnp.ones((1, 2 * d), F32)], axis=1)
    bias_tiles = _bias_tiles(rel_bias, ATTN_T)
    mlp_w1_bf16 = mlp_w1.astype(BF16)
    mlp_w2_bf16 = mlp_w2.astype(BF16)

    ia = ic = il = 0
    for layer in range(DEPTH):
        kind = layer % N_MIXERS
        g = row(mixer_norm_g[layer])
        if kind == 0:
            lambda_init = 0.8 - 0.6 * math.exp(-0.3 * layer)
            qkv = _norm_proj(h, g, attn_w_qkv, ia, q_scale)
            att = _diff_attention(qkv, bias_tiles, row(attn_lq1[ia]), row(attn_lk1[ia]),
                                  row(attn_lq2[ia]), row(attn_lk2[ia]), row(attn_subln_g[ia]),
                                  batch=batch, seq=seq, lambda_init=lambda_init)
            h = _proj_res(att, attn_w_o, ia, zero_bias, h)
            ia += 1
        elif kind == 1:
            u = _norm_glu(h, g, conv_w_in, ic, row(conv_b_in[ic]))
            act = _conv_ln(u, conv_dw_w[ic], row(conv_dw_b[ic]), row(conv_ln_g[ic]), row(conv_ln_b[ic]), seq=seq)
            h = _proj_res(act, conv_w_out, ic, row(conv_b_out[ic]), h)
            ic += 1
        else:
            u, y = _norm_lru_in(h, g, lru_w_in, il)
            hy = _lru_core(u, y, lru_conv_w[il], row(lru_conv_b[il]), lru_w_a[il], row(lru_b_a[il]),
                           lru_w_i[il], row(lru_b_i[il]), row(lru_lambda[il]), batch=batch, seq=seq)
            h = _proj_res(hy, lru_w_out, il, zero_bias, h)
            il += 1
        final_g = row(final_norm_g) if layer == DEPTH - 1 else None
        h = _mlp(h, row(mlp_norm_g[layer]), mlp_w1_bf16, mlp_w2_bf16, layer, final_g)
    return h.reshape(batch, seq, d)
```

```python
import functools
import math

import numpy as np
import jax
import jax.numpy as jnp
from jax import lax
from jax.experimental import pallas as pl
from jax.experimental.pallas import tpu as pltpu

F32 = jnp.float32
BF16 = jnp.bfloat16

D_MODEL = 2048
DEPTH = 4
N_MIXERS = 3
HEAD_DIM = 128
HEAD_W = 2 * HEAD_DIM
N_HEADS = D_MODEL // HEAD_W
NUM_BUCKETS = 32
MAX_DISTANCE = 128
MASK_VALUE = -1e30
CONV_WIDTH = 31
LRU_BLOCKS = 8
LRU_BLOCK_SIZE = D_MODEL // LRU_BLOCKS
LRU_CONV_WIDTH = 4
LRU_C = 8.0
D_FF = 4 * D_MODEL

VMEM_LIMIT = 56 * 1024 * 1024
NORM_ROWS = 128
LANES = 128
SUBLANES = 8
BF16_ROWS = 16


def _params(*sem):
    return pltpu.CompilerParams(dimension_semantics=sem, vmem_limit_bytes=VMEM_LIMIT)


def _rms_rows_to(dst_ref, x_ref, g_ref, rows, eps):
    def body(c, carry):
        r0 = pl.multiple_of(c * NORM_ROWS, NORM_ROWS)
        x = x_ref[pl.ds(r0, NORM_ROWS), :]
        ms = jnp.mean(x * x, axis=-1, keepdims=True)
        dst_ref[pl.ds(r0, NORM_ROWS), :] = (x * lax.rsqrt(ms + eps) * g_ref[...]).astype(BF16)
        return carry
    lax.fori_loop(0, rows // NORM_ROWS, body, 0)


def _mlp_cast_specs(grid, layer):
    d, ff = D_MODEL, D_FF
    steps = math.prod(grid)

    def lin(*idx):
        n = 0
        for i, g in zip(idx, grid):
            n = n * g + i
        return n

    split = max(1, BF16_ROWS * steps // d)
    r1, c1, r2 = d * split // steps, ff // split, ff // steps
    in_specs = [pl.BlockSpec((None, r1, c1), lambda *idx: (layer, lin(*idx) // split, lin(*idx) % split)),
                pl.BlockSpec((None, r2, d), lambda *idx: (layer, lin(*idx), 0))]
    out_specs = [pl.BlockSpec((r1, c1), lambda *idx: (lin(*idx) // split, lin(*idx) % split)),
                 pl.BlockSpec((r2, d), lambda *idx: (lin(*idx), 0))]
    out_shapes = [jax.ShapeDtypeStruct((d, ff), BF16), jax.ShapeDtypeStruct((ff, d), BF16)]
    return in_specs, out_specs, out_shapes


def _cast_slabs(w1_ref, w2_ref, w1b_ref, w2b_ref):
    w1b_ref[...] = w1_ref[...].astype(BF16)
    w2b_ref[...] = w2_ref[...].astype(BF16)


def _sigmoid(x):
    return 1.0 / (1.0 + jnp.exp(-x))


def _gelu_tanh(x):
    c = math.sqrt(2.0 / math.pi)
    return 0.5 * x * (1.0 + jnp.tanh(c * (x + 0.044715 * (x * x * x))))


def _norm_proj_kernel(x_ref, g_ref, w_ref, s_ref, o_ref, xn_ref, *, tm):
    @pl.when(pl.program_id(1) == 0)
    def _():
        _rms_rows_to(xn_ref, x_ref, g_ref, tm, 1e-6)
    acc = jnp.dot(xn_ref[...], w_ref[...].astype(BF16), preferred_element_type=F32)
    o_ref[...] = (acc * s_ref[...]).astype(o_ref.dtype)


def _norm_proj(x, g, w, layer, col_scale, *, tm=1024, tn=512):
    m, k = x.shape
    n = w.shape[2]
    return pl.pallas_call(
        functools.partial(_norm_proj_kernel, tm=tm),
        grid=(m // tm, n // tn),
        in_specs=[
            pl.BlockSpec((tm, k), lambda i, j: (i, 0)),
            pl.BlockSpec((1, k), lambda i, j: (0, 0)),
            pl.BlockSpec((None, k, tn), lambda i, j: (layer, 0, j)),
            pl.BlockSpec((1, tn), lambda i, j: (0, j)),
        ],
        out_specs=pl.BlockSpec((tm, tn), lambda i, j: (i, j)),
        out_shape=jax.ShapeDtypeStruct((m, n), BF16),
        scratch_shapes=[pltpu.VMEM((tm, k), BF16)],
        compiler_params=_params("parallel", "arbitrary"),
        name="norm_proj",
    )(x, g, w, col_scale)


def _norm_glu_kernel(x_ref, g_ref, wa_ref, wg_ref, ba_ref, bg_ref, o_ref, xn_ref, *, tm):
    @pl.when(pl.program_id(1) == 0)
    def _():
        _rms_rows_to(xn_ref, x_ref, g_ref, tm, 1e-6)
    xn = xn_ref[...]
    a = jnp.dot(xn, wa_ref[...].astype(BF16), preferred_element_type=F32) + ba_ref[...]
    gate = jnp.dot(xn, wg_ref[...].astype(BF16), preferred_element_type=F32) + bg_ref[...]
    o_ref[...] = a * _sigmoid(gate)


def _norm_glu(x, g, w, layer, b, *, tm=1024, tn=512):
    m, k = x.shape
    half = w.shape[2] // 2
    nj = half // tn
    return pl.pallas_call(
        functools.partial(_norm_glu_kernel, tm=tm),
        grid=(m // tm, nj),
        in_specs=[
            pl.BlockSpec((tm, k), lambda i, j: (i, 0)),
            pl.BlockSpec((1, k), lambda i, j: (0, 0)),
            pl.BlockSpec((None, k, tn), lambda i, j: (layer, 0, j)),
            pl.BlockSpec((None, k, tn), lambda i, j: (layer, 0, j + nj)),
            pl.BlockSpec((1, tn), lambda i, j: (0, j)),
            pl.BlockSpec((1, tn), lambda i, j: (0, j + nj)),
        ],
        out_specs=pl.BlockSpec((tm, tn), lambda i, j: (i, j)),
        out_shape=jax.ShapeDtypeStruct((m, half), F32),
        scratch_shapes=[pltpu.VMEM((tm, k), BF16)],
        compiler_params=_params("parallel", "arbitrary"),
        name="norm_glu",
    )(x, g, w, w, b, b)


def _norm_lru_in_kernel(x_ref, g_ref, wu_ref, wy_ref, u_ref, y_ref, xn_ref, *, tm):
    @pl.when(pl.program_id(1) == 0)
    def _():
        _rms_rows_to(xn_ref, x_ref, g_ref, tm, 1e-6)
    xn = xn_ref[...]
    u_ref[...] = jnp.dot(xn, wu_ref[...].astype(BF16), preferred_element_type=F32)
    y_ref[...] = _gelu_tanh(jnp.dot(xn, wy_ref[...].astype(BF16), preferred_element_type=F32))


def _norm_lru_in(x, g, w, layer, *, tm=1024, tn=512):
    m, k = x.shape
    half = w.shape[2] // 2
    nj = half // tn
    return pl.pallas_call(
        functools.partial(_norm_lru_in_kernel, tm=tm),
        grid=(m // tm, nj),
        in_specs=[
            pl.BlockSpec((tm, k), lambda i, j: (i, 0)),
            pl.BlockSpec((1, k), lambda i, j: (0, 0)),
            pl.BlockSpec((None, k, tn), lambda i, j: (layer, 0, j)),
            pl.BlockSpec((None, k, tn), lambda i, j: (layer, 0, j + nj)),
        ],
        out_specs=[pl.BlockSpec((tm, tn), lambda i, j: (i, j)),
                   pl.BlockSpec((tm, tn), lambda i, j: (i, j))],
        out_shape=[jax.ShapeDtypeStruct((m, half), F32), jax.ShapeDtypeStruct((m, half), F32)],
        scratch_shapes=[pltpu.VMEM((tm, k), BF16)],
        compiler_params=_params("parallel", "arbitrary"),
        name="norm_lru_in",
    )(x, g, w, w)


def _proj_res_kernel(a_ref, w_ref, b_ref, r_ref, o_ref, wb_ref):
    @pl.when(pl.program_id(1) == 0)
    def _():
        wb_ref[...] = w_ref[...].astype(BF16)
    acc = jnp.dot(a_ref[...], wb_ref[...], preferred_element_type=F32)
    o_ref[...] = r_ref[...] + (acc + b_ref[...])


def _proj_res(a, w, layer, b, res, *, tm=1024, tn=1024):
    m, k = a.shape
    n = w.shape[2]
    return pl.pallas_call(
        _proj_res_kernel,
        grid=(n // tn, m // tm),
        in_specs=[
            pl.BlockSpec((tm, k), lambda j, i: (i, 0)),
            pl.BlockSpec((None, k, tn), lambda j, i: (layer, 0, j)),
            pl.BlockSpec((1, tn), lambda j, i: (0, j)),
            pl.BlockSpec((tm, tn), lambda j, i: (i, j)),
        ],
        out_specs=pl.BlockSpec((tm, tn), lambda j, i: (i, j)),
        out_shape=jax.ShapeDtypeStruct((m, n), F32),
        scratch_shapes=[pltpu.VMEM((k, tn), BF16)],
        compiler_params=_params("parallel", "arbitrary"),
        name="proj_res",
    )(a, w, b, res)


def _mlp_kernel(*refs, tm, final_norm):
    if final_norm:
        x_ref, g_ref, w1_ref, w2_ref, fg_ref, o_ref, xn_ref = refs
    else:
        x_ref, g_ref, w1_ref, w2_ref, o_ref, xn_ref = refs
    f = pl.program_id(1)

    @pl.when(f == 0)
    def _():
        _rms_rows_to(xn_ref, x_ref, g_ref, tm, 1e-6)

    z = jnp.maximum(jnp.dot(xn_ref[...], w1_ref[...], preferred_element_type=F32), 0.0)
    part = jnp.dot((z * z).astype(BF16), w2_ref[...], preferred_element_type=F32)

    @pl.when(f == 0)
    def _():
        o_ref[...] = x_ref[...] + part

    @pl.when(f != 0)
    def _():
        o_ref[...] += part

    if final_norm:
        @pl.when(f == pl.num_programs(1) - 1)
        def _():
            def body(c, carry):
                rows = pl.ds(pl.multiple_of(c * NORM_ROWS, NORM_ROWS), NORM_ROWS)
                y = o_ref[rows, :]
                ms = jnp.mean(y * y, axis=-1, keepdims=True)
                o_ref[rows, :] = y * lax.rsqrt(ms + 1e-6) * fg_ref[...]
                return carry
            lax.fori_loop(0, tm // NORM_ROWS, body, 0)


def _mlp(x, g, w1, w2, final_g=None, *, tm=512, tf=1024):
    m, d = x.shape
    ff = w1.shape[1]
    final_norm = final_g is not None
    row_spec = pl.BlockSpec((1, d), lambda i, f: (0, 0))
    return pl.pallas_call(
        functools.partial(_mlp_kernel, tm=tm, final_norm=final_norm),
        grid=(m // tm, ff // tf),
        in_specs=[
            pl.BlockSpec((tm, d), lambda i, f: (i, 0)),
            row_spec,
            pl.BlockSpec((d, tf), lambda i, f: (0, f)),
            pl.BlockSpec((tf, d), lambda i, f: (f, 0)),
        ] + ([row_spec] if final_norm else []),
        out_specs=pl.BlockSpec((tm, d), lambda i, f: (i, 0)),
        out_shape=jax.ShapeDtypeStruct((m, d), F32),
        scratch_shapes=[pltpu.VMEM((tm, d), BF16)],
        compiler_params=_params("parallel", "arbitrary"),
        name="mlp",
    )(x, g, w1, w2, *((final_g,) if final_norm else ()))


ATTN_T = 512
ATTN_STRIP = 32
LOG2E = math.log2(math.e)


def _bucket_tiles(t):
    r = np.arange(t, dtype=np.int32)[:, None]
    c = np.arange(t, dtype=np.int32)[None, :]
    tiles = []
    for d in range(2):
        n = np.maximum(d * t + r - c, 0)
        max_exact = NUM_BUCKETS // 2
        nf = np.maximum(n, 1).astype(np.float32)
        large = max_exact + (np.log(nf / np.float32(max_exact)) / np.float32(math.log(MAX_DISTANCE / max_exact))
                             * np.float32(NUM_BUCKETS - max_exact)).astype(np.int32)
        large = np.minimum(large, NUM_BUCKETS - 1)
        bucket = np.where(n < max_exact, n, large).astype(np.int32)
        tiles.append(np.where(d * t + r - c >= 0, bucket, -1))
    return np.stack(tiles)


def _bias_tiles_kernel(rb_ref, bucket_ref, o_ref):
    h = pl.program_id(0)
    bucket = bucket_ref[0]
    acc = jnp.zeros(bucket.shape, F32)
    for j in range(NUM_BUCKETS):
        acc = jnp.where(bucket == j, rb_ref[j, h], acc)
    acc = (acc - rb_ref[NUM_BUCKETS - 1, h]) * LOG2E
    o_ref[0, 0] = jnp.where(bucket < 0, MASK_VALUE, acc)


def _bias_tiles(rel_bias, t):
    buckets = jnp.asarray(_bucket_tiles(t))
    return pl.pallas_call(
        _bias_tiles_kernel,
        grid=(N_HEADS, 2),
        in_specs=[
            pl.BlockSpec(memory_space=pltpu.SMEM),
            pl.BlockSpec((1, t, t), lambda h, d: (d, 0, 0)),
        ],
        out_specs=pl.BlockSpec((1, 1, t, t), lambda h, d: (h, d, 0, 0)),
        out_shape=jax.ShapeDtypeStruct((N_HEADS, 2, t, t), F32),
        compiler_params=_params("parallel", "parallel"),
        name="bias_tiles",
    )(rel_bias, buckets)


def _attn_kernel(lq1_ref, lk1_ref, lq2_ref, lk2_ref, sg_ref, q_ref, k_ref, v_ref, bias_ref, w1_ref, w2_ref,
                 o_ref, w1b_ref, w2b_ref, m_ref, l_ref, alpha_ref, acc_ref, s_ref, p_ref, *, t, lambda_init):
    qi = pl.program_id(2)
    chunks = [slice(j * LANES, (j + 1) * LANES) for j in range(t // LANES)]
    _cast_slabs(w1_ref, w2_ref, w1b_ref, w2b_ref)

    m_ref[...] = jnp.full(m_ref.shape, MASK_VALUE, F32)
    l_ref[...] = jnp.zeros(l_ref.shape, F32)
    acc_ref[...] = jnp.zeros(acc_ref.shape, F32)
    alpha_ref[...] = jnp.zeros(alpha_ref.shape, F32)
    p_ref[...] = jnp.zeros(p_ref.shape, BF16)

    def start(ki):
        return pl.multiple_of(jnp.maximum(ki, 0) * t, t)

    def qk_logits(ki, slot):
        for c in range(2):
            lanes = slice(c * HEAD_DIM, (c + 1) * HEAD_DIM)
            s_ref[slot, c] = lax.dot_general(q_ref[:, lanes], k_ref[pl.ds(start(ki), t), lanes],
                                             (((1,), (1,)), ((), ())), preferred_element_type=F32)

    def accumulate(ki, slot):
        v = v_ref[pl.ds(start(ki), t), :]
        for c in range(2):
            alpha = alpha_ref[slot, c]
            pv = jnp.dot(p_ref[slot, c], v, preferred_element_type=F32)
            acc_ref[c] = jnp.concatenate([alpha, alpha], axis=1) * acc_ref[c] + pv

    strips = [slice(r * ATTN_STRIP, (r + 1) * ATTN_STRIP) for r in range(t // ATTN_STRIP)]

    def block(ki, cur, bias_index, has_next):
        other = 1 - cur

        def logits(c, rows):
            s = s_ref[cur, c, rows, :]
            if bias_index is not None:
                s = s + bias_ref[0, bias_index, rows, :]
            return s

        part_max = [[], []]
        for rows in strips:
            for c in range(2):
                s = logits(c, rows)
                part = s[:, chunks[0]]
                for ch in chunks[1:]:
                    part = jnp.maximum(part, s[:, ch])
                part_max[c].append(part)

        accumulate(ki - 1, other)
        m_new, alpha = [], []
        for c in range(2):
            m_prev = m_ref[c]
            m_c = jnp.maximum(m_prev, jnp.max(jnp.concatenate(part_max[c], axis=0), axis=1, keepdims=True))
            alpha.append(jnp.exp2(m_prev - m_c))
            alpha_ref[cur, c] = alpha[c]
            m_ref[c] = m_c
            m_new.append(m_c)

        part_sum = [[], []]
        for rows in strips:
            for c in range(2):
                s = logits(c, rows)
                m_rows = m_new[c][rows, :]
                part = None
                for ch in chunks:
                    p = jnp.exp2(s[:, ch] - m_rows)
                    part = p if part is None else part + p
                    p_ref[cur, c, rows, ch] = p.astype(BF16)
                part_sum[c].append(part)

        if has_next:
            qk_logits(ki + 1, other)
        for c in range(2):
            row_sum = jnp.sum(jnp.concatenate(part_sum[c], axis=0), axis=1, keepdims=True)
            l_ref[c] = alpha[c] * l_ref[c] + row_sum

    qk_logits(0, 0)

    n_far = jnp.maximum(qi - 1, 0)

    def far_pair(j, carry):
        block(2 * j, 0, None, True)
        block(2 * j + 1, 1, None, True)
        return carry
    lax.fori_loop(0, n_far // 2, far_pair, 0)

    @pl.when(qi % 2 == 1)
    def _():
        block(qi - 1, 0, 1, True)
        block(qi, 1, 0, False)
        accumulate(qi, 1)

    @pl.when(qi % 2 == 0)
    def _():
        @pl.when(qi >= 2)
        def _():
            block(qi - 2, 0, None, True)
            block(qi - 1, 1, 1, True)
        block(qi, 0, 0, False)
        accumulate(qi, 0)

    lam = (jnp.exp(jnp.sum(lq1_ref[...] * lk1_ref[...], keepdims=True))
           - jnp.exp(jnp.sum(lq2_ref[...] * lk2_ref[...], keepdims=True)) + lambda_init)
    l0 = l_ref[0]
    l1 = l_ref[1]
    o = (acc_ref[0] / jnp.concatenate([l0, l0], axis=1)
         - lam * (acc_ref[1] / jnp.concatenate([l1, l1], axis=1)))
    ms = jnp.mean(o * o, axis=-1, keepdims=True)
    o = o * lax.rsqrt(ms + 1e-5) * sg_ref[...] * (1.0 - lambda_init)
    o_ref[...] = o.astype(o_ref.dtype)


def _diff_attention(qkv, bias_tiles, lq1, lk1, lq2, lk2, subln_g, mlp_w1, mlp_w2, layer, *, batch, seq, lambda_init):
    t = ATTN_T
    nq = seq // t
    grid = (batch, N_HEADS, nq)
    lam_spec = pl.BlockSpec((1, HEAD_DIM), lambda b, h, i: (0, 0))
    cast_in, cast_out, cast_shapes = _mlp_cast_specs(grid, layer)
    return pl.pallas_call(
        functools.partial(_attn_kernel, t=t, lambda_init=lambda_init),
        grid=grid,
        in_specs=[
            lam_spec, lam_spec, lam_spec, lam_spec,
            pl.BlockSpec((1, HEAD_W), lambda b, h, i: (0, 0)),
            pl.BlockSpec((t, HEAD_W), lambda b, h, i: (b * nq + i, h)),
            pl.BlockSpec((seq, HEAD_W), lambda b, h, i: (b, N_HEADS + h)),
            pl.BlockSpec((seq, HEAD_W), lambda b, h, i: (b, 2 * N_HEADS + h)),
            pl.BlockSpec((1, 2, t, t), lambda b, h, i: (h, 0, 0, 0)),
        ] + cast_in,
        out_specs=[pl.BlockSpec((t, HEAD_W), lambda b, h, i: (b * nq + i, h))] + cast_out,
        out_shape=[jax.ShapeDtypeStruct((batch * seq, D_MODEL), BF16)] + cast_shapes,
        scratch_shapes=[pltpu.VMEM((2, t, LANES), F32), pltpu.VMEM((2, t, LANES), F32),
                        pltpu.VMEM((2, 2, t, LANES), F32),
                        pltpu.VMEM((2, t, HEAD_W), F32), pltpu.VMEM((2, 2, t, t), F32),
                        pltpu.VMEM((2, 2, t, t), BF16)],
        compiler_params=_params("parallel", "parallel", "arbitrary"),
        name="diff_attn",
    )(lq1, lk1, lq2, lk2, subln_g, qkv, qkv, qkv, bias_tiles, mlp_w1, mlp_w2)


CONV_T = 256
CONV_HALO = 32
CONV_ROWS = 64
CONV_LANES = 256


def _conv_ln_kernel(halo_ref, u_ref, w_ref, b_ref, g_ref, beta_ref, w1_ref, w2_ref,
                    o_ref, w1b_ref, w2b_ref, sh_ref, y_ref, *, tiles_per_seq):
    _cast_slabs(w1_ref, w2_ref, w1b_ref, w2b_ref)
    i = pl.program_id(0)
    first = (i % tiles_per_seq) == 0

    @pl.when(first)
    def _():
        sh_ref[0, 0:CONV_HALO, :] = jnp.zeros((CONV_HALO, D_MODEL), F32)

    @pl.when(jnp.logical_not(first))
    def _():
        sh_ref[0, 0:CONV_HALO, :] = halo_ref[...]

    sh_ref[0, CONV_HALO:, :] = u_ref[...]
    window = CONV_HALO + CONV_T
    for s in range(1, SUBLANES):
        for c0 in range(0, D_MODEL, CONV_LANES):
            lanes = slice(c0, c0 + CONV_LANES)
            sh_ref[s, 0:window - SUBLANES, lanes] = sh_ref[0, s:s + window - SUBLANES, lanes]

    base = CONV_HALO - (CONV_WIDTH - 1)
    for r0 in range(0, CONV_T, CONV_ROWS):
        for c0 in range(0, D_MODEL, CONV_LANES):
            lanes = slice(c0, c0 + CONV_LANES)
            acc = jnp.zeros((CONV_ROWS, CONV_LANES), F32)
            for j in range(CONV_WIDTH):
                a, s = divmod(base + j, SUBLANES)
                x0 = r0 + a * SUBLANES
                acc = acc + w_ref[j:j + 1, lanes] * sh_ref[s, x0:x0 + CONV_ROWS, lanes]
            y_ref[r0:r0 + CONV_ROWS, lanes] = acc + b_ref[:, lanes]

    def ln_body(c, carry):
        r0 = pl.multiple_of(c * CONV_ROWS, CONV_ROWS)
        y = y_ref[pl.ds(r0, CONV_ROWS), :]
        mu = jnp.mean(y, axis=-1, keepdims=True)
        yc = y - mu
        var = jnp.mean(yc * yc, axis=-1, keepdims=True)
        z = yc * lax.rsqrt(var + 1e-5) * g_ref[...] + beta_ref[...]
        o_ref[pl.ds(r0, CONV_ROWS), :] = (z * _sigmoid(z)).astype(o_ref.dtype)
        return carry
    lax.fori_loop(0, CONV_T // CONV_ROWS, ln_body, 0)


def _conv_ln(u, dw_w, dw_b, ln_g, ln_b, mlp_w1, mlp_w2, layer, *, seq):
    m, d = u.shape
    tiles_per_seq = seq // CONV_T
    ratio = CONV_T // CONV_HALO
    grid = (m // CONV_T,)
    row = lambda i: (0, 0)
    cast_in, cast_out, cast_shapes = _mlp_cast_specs(grid, layer)
    return pl.pallas_call(
        functools.partial(_conv_ln_kernel, tiles_per_seq=tiles_per_seq),
        grid=grid,
        in_specs=[
            pl.BlockSpec((CONV_HALO, d), lambda i: (jnp.maximum(i * ratio - 1, 0), 0)),
            pl.BlockSpec((CONV_T, d), lambda i: (i, 0)),
            pl.BlockSpec((CONV_WIDTH, d), row),
            pl.BlockSpec((1, d), row), pl.BlockSpec((1, d), row), pl.BlockSpec((1, d), row),
        ] + cast_in,
        out_specs=[pl.BlockSpec((CONV_T, d), lambda i: (i, 0))] + cast_out,
        out_shape=[jax.ShapeDtypeStruct((m, d), BF16)] + cast_shapes,
        scratch_shapes=[pltpu.VMEM((SUBLANES, CONV_HALO + CONV_T, d), F32), pltpu.VMEM((CONV_T, d), F32)],
        compiler_params=_params("parallel"),
        name="conv_ln",
    )(u, u, dw_w, dw_b, ln_g, ln_b, mlp_w1, mlp_w2)


LRU_T = 256
LRU_HALO = 8


def _lru_kernel(halo_ref, u_ref, y_ref, cw_ref, cb_ref, wa_ref, ba_ref, wi_ref, bi_ref, lam_ref, w1_ref, w2_ref,
                o_ref, w1b_ref, w2b_ref, buf_ref, carry_ref):
    _cast_slabs(w1_ref, w2_ref, w1b_ref, w2b_ref)
    ti = pl.program_id(2)
    first = ti == 0
    buf_ref[LRU_HALO:, :] = u_ref[...]

    @pl.when(first)
    def _():
        buf_ref[0:LRU_HALO, :] = jnp.zeros((LRU_HALO, LRU_BLOCK_SIZE), F32)
        carry_ref[...] = jnp.zeros(carry_ref.shape, F32)

    @pl.when(jnp.logical_not(first))
    def _():
        buf_ref[0:LRU_HALO, :] = halo_ref[...]

    base = LRU_HALO - (LRU_CONV_WIDTH - 1)
    uc = jnp.zeros((LRU_T, LRU_BLOCK_SIZE), F32)
    for j in range(LRU_CONV_WIDTH):
        uc = uc + cw_ref[j:j + 1, :] * buf_ref[base + j:base + j + LRU_T, :]
    uc = uc + cb_ref[...]

    ucb = uc.astype(BF16)
    r = _sigmoid(jnp.dot(ucb, wa_ref[0].astype(BF16), preferred_element_type=F32) + ba_ref[...])
    gate_i = _sigmoid(jnp.dot(ucb, wi_ref[0].astype(BF16), preferred_element_type=F32) + bi_ref[...])
    neg_lam = -lam_ref[...]
    softplus = jnp.maximum(neg_lam, 0.0) + jnp.log1p(jnp.exp(-jnp.abs(neg_lam)))
    log_a = -LRU_C * r * softplus
    a = jnp.exp(log_a)
    mult = jnp.sqrt(-jnp.tanh(log_a) * (1.0 + a * a))
    b = mult * (gate_i * uc)

    row = lax.broadcasted_iota(jnp.int32, (LRU_T, LRU_BLOCK_SIZE), 0)
    s = 1
    while s < SUBLANES:
        valid = row >= s
        a_sh = jnp.where(valid, pltpu.roll(a, s, 0), 1.0)
        b_sh = jnp.where(valid, pltpu.roll(b, s, 0), 0.0)
        b = a * b_sh + b
        a = a * a_sh
        s *= 2
    while s < LRU_T:
        b = jnp.concatenate([b[:s], a[s:] * b[:-s] + b[s:]], axis=0)
        a = jnp.concatenate([a[:s], a[s:] * a[:-s]], axis=0)
        s *= 2
    hs = a * carry_ref[0:1, :] + b
    carry_ref[0:1, :] = hs[LRU_T - 1:LRU_T, :]
    o_ref[...] = (hs * y_ref[...]).astype(o_ref.dtype)


def _lru_core(u, y, conv_w, conv_b, w_a, b_a, w_i, b_i, lam, mlp_w1, mlp_w2, layer, *, batch, seq):
    m, d = u.shape
    nt = seq // LRU_T
    ratio = LRU_T // LRU_HALO
    c = LRU_BLOCK_SIZE
    grid = (batch, LRU_BLOCKS, nt)
    chan = lambda b, g, t: (0, g)
    tile = lambda b, g, t: (b * nt + t, g)
    gate_w = pl.BlockSpec((1, c, c), lambda b, g, t: (g, 0, 0))
    cast_in, cast_out, cast_shapes = _mlp_cast_specs(grid, layer)
    return pl.pallas_call(
        _lru_kernel,
        grid=grid,
        in_specs=[
            pl.BlockSpec((LRU_HALO, c), lambda b, g, t: (jnp.maximum((b * nt + t) * ratio - 1, 0), g)),
            pl.BlockSpec((LRU_T, c), tile),
            pl.BlockSpec((LRU_T, c), tile),
            pl.BlockSpec((LRU_CONV_WIDTH, c), chan),
            pl.BlockSpec((1, c), chan),
            gate_w, pl.BlockSpec((1, c), chan),
            gate_w, pl.BlockSpec((1, c), chan),
            pl.BlockSpec((1, c), chan),
        ] + cast_in,
        out_specs=[pl.BlockSpec((LRU_T, c), tile)] + cast_out,
        out_shape=[jax.ShapeDtypeStruct((m, d), BF16)] + cast_shapes,
        scratch_shapes=[pltpu.VMEM((LRU_HALO + LRU_T, c), F32), pltpu.VMEM((8, c), F32)],
        compiler_params=_params("parallel", "parallel", "arbitrary"),
        name="lru_core",
    )(u, u, y, conv_w, conv_b, w_a, b_a, w_i, b_i, lam, mlp_w1, mlp_w2)


def kernel(x, rel_bias, mixer_norm_g, attn_w_qkv, attn_lq1, attn_lk1, attn_lq2, attn_lk2, attn_subln_g, attn_w_o, conv_w_in, conv_b_in, conv_dw_w, conv_dw_b, conv_ln_g, conv_ln_b, conv_w_out, conv_b_out, lru_w_in, lru_conv_w, lru_conv_b, lru_w_a, lru_b_a, lru_w_i, lru_b_i, lru_lambda, lru_w_out, mlp_norm_g, mlp_w1, mlp_w2, final_norm_g):
    batch, seq, d = x.shape
    h = x.reshape(batch * seq, d)
    row = lambda v: v.reshape(1, -1)
    zero_bias = jnp.zeros((1, d), F32)
    q_scale = jnp.concatenate([jnp.full((1, d), HEAD_DIM ** -0.5 * LOG2E, F32), jnp.ones((1, 2 * d), F32)], axis=1)
    bias_tiles = _bias_tiles(rel_bias, ATTN_T)

    ia = ic = il = 0
    for layer in range(DEPTH):
        kind = layer % N_MIXERS
        g = row(mixer_norm_g[layer])
        if kind == 0:
            lambda_init = 0.8 - 0.6 * math.exp(-0.3 * layer)
            qkv = _norm_proj(h, g, attn_w_qkv, ia, q_scale)
            att, w1, w2 = _diff_attention(qkv, bias_tiles, row(attn_lq1[ia]), row(attn_lk1[ia]),
                                          row(attn_lq2[ia]), row(attn_lk2[ia]), row(attn_subln_g[ia]),
                                          mlp_w1, mlp_w2, layer, batch=batch, seq=seq, lambda_init=lambda_init)
            h = _proj_res(att, attn_w_o, ia, zero_bias, h)
            ia += 1
        elif kind == 1:
            u = _norm_glu(h, g, conv_w_in, ic, row(conv_b_in[ic]))
            act, w1, w2 = _conv_ln(u, conv_dw_w[ic], row(conv_dw_b[ic]), row(conv_ln_g[ic]), row(conv_ln_b[ic]),
                                   mlp_w1, mlp_w2, layer, seq=seq)
            h = _proj_res(act, conv_w_out, ic, row(conv_b_out[ic]), h)
            ic += 1
        else:
            u, y = _norm_lru_in(h, g, lru_w_in, il)
            hy, w1, w2 = _lru_core(u, y, lru_conv_w[il], row(lru_conv_b[il]), lru_w_a[il], row(lru_b_a[il]),
                                   lru_w_i[il], row(lru_b_i[il]), row(lru_lambda[il]), mlp_w1, mlp_w2, layer,
                                   batch=batch, seq=seq)
            h = _proj_res(hy, lru_w_out, il, zero_bias, h)
            il += 1
        final_g = row(final_norm_g) if layer == DEPTH - 1 else None
        h = _mlp(h, row(mlp_norm_g[layer]), w1, w2, final_g)
    return h.reshape(batch, seq, d)
```

```python
import functools
import math

import numpy as np
import jax
import jax.numpy as jnp
from jax import lax
from jax.experimental import pallas as pl
from jax.experimental.pallas import tpu as pltpu

F32 = jnp.float32
BF16 = jnp.bfloat16

D_MODEL = 2048
DEPTH = 4
N_MIXERS = 3
HEAD_DIM = 128
HEAD_W = 2 * HEAD_DIM
N_HEADS = D_MODEL // HEAD_W
NUM_BUCKETS = 32
MAX_DISTANCE = 128
MASK_VALUE = -1e30
CONV_WIDTH = 31
LRU_BLOCKS = 8
LRU_BLOCK_SIZE = D_MODEL // LRU_BLOCKS
LRU_CONV_WIDTH = 4
LRU_C = 8.0
D_FF = 4 * D_MODEL

VMEM_LIMIT = 56 * 1024 * 1024
NORM_ROWS = 128
LANES = 128
SUBLANES = 8
BF16_ROWS = 16


def _params(*sem):
    return pltpu.CompilerParams(dimension_semantics=sem, vmem_limit_bytes=VMEM_LIMIT)


def _rms_rows_to(dst_ref, x_ref, g_ref, rows, eps):
    def body(c, carry):
        r0 = pl.multiple_of(c * NORM_ROWS, NORM_ROWS)
        x = x_ref[pl.ds(r0, NORM_ROWS), :]
        ms = jnp.mean(x * x, axis=-1, keepdims=True)
        dst_ref[pl.ds(r0, NORM_ROWS), :] = (x * lax.rsqrt(ms + eps) * g_ref[...]).astype(BF16)
        return carry
    lax.fori_loop(0, rows // NORM_ROWS, body, 0)


def _mlp_cast_specs(grid, layer):
    d, ff = D_MODEL, D_FF
    steps = math.prod(grid)

    def lin(*idx):
        n = 0
        for i, g in zip(idx, grid):
            n = n * g + i
        return n

    split = max(1, BF16_ROWS * steps // d)
    r1, c1, r2 = d * split // steps, ff // split, ff // steps
    in_specs = [pl.BlockSpec((None, r1, c1), lambda *idx: (layer, lin(*idx) // split, lin(*idx) % split)),
                pl.BlockSpec((None, r2, d), lambda *idx: (layer, lin(*idx), 0))]
    out_specs = [pl.BlockSpec((r1, c1), lambda *idx: (lin(*idx) // split, lin(*idx) % split)),
                 pl.BlockSpec((r2, d), lambda *idx: (lin(*idx), 0))]
    out_shapes = [jax.ShapeDtypeStruct((d, ff), BF16), jax.ShapeDtypeStruct((ff, d), BF16)]
    return in_specs, out_specs, out_shapes


def _cast_slabs(w1_ref, w2_ref, w1b_ref, w2b_ref):
    w1b_ref[...] = w1_ref[...].astype(BF16)
    w2b_ref[...] = w2_ref[...].astype(BF16)


def _sigmoid(x):
    return 1.0 / (1.0 + jnp.exp(-x))


def _gelu_tanh(x):
    c = math.sqrt(2.0 / math.pi)
    return 0.5 * x * (1.0 + jnp.tanh(c * (x + 0.044715 * (x * x * x))))


def _norm_proj_kernel(x_ref, g_ref, w_ref, s_ref, o_ref, xn_ref, *, tm):
    @pl.when(pl.program_id(1) == 0)
    def _():
        _rms_rows_to(xn_ref, x_ref, g_ref, tm, 1e-6)
    acc = jnp.dot(xn_ref[...], w_ref[...].astype(BF16), preferred_element_type=F32)
    o_ref[...] = (acc * s_ref[...]).astype(o_ref.dtype)


def _norm_proj(x, g, w, layer, col_scale, *, tm=1024, tn=1024):
    m, k = x.shape
    n = w.shape[2]
    return pl.pallas_call(
        functools.partial(_norm_proj_kernel, tm=tm),
        grid=(m // tm, n // tn),
        in_specs=[
            pl.BlockSpec((tm, k), lambda i, j: (i, 0)),
            pl.BlockSpec((1, k), lambda i, j: (0, 0)),
            pl.BlockSpec((None, k, tn), lambda i, j: (layer, 0, j)),
            pl.BlockSpec((1, tn), lambda i, j: (0, j)),
        ],
        out_specs=pl.BlockSpec((tm, tn), lambda i, j: (i, j)),
        out_shape=jax.ShapeDtypeStruct((m, n), BF16),
        scratch_shapes=[pltpu.VMEM((tm, k), BF16)],
        compiler_params=_params("parallel", "arbitrary"),
        name="norm_proj",
    )(x, g, w, col_scale)


def _norm_glu_kernel(x_ref, g_ref, wa_ref, wg_ref, ba_ref, bg_ref, o_ref, xn_ref, *, tm):
    @pl.when(pl.program_id(1) == 0)
    def _():
        _rms_rows_to(xn_ref, x_ref, g_ref, tm, 1e-6)
    xn = xn_ref[...]
    a = jnp.dot(xn, wa_ref[...].astype(BF16), preferred_element_type=F32) + ba_ref[...]
    gate = jnp.dot(xn, wg_ref[...].astype(BF16), preferred_element_type=F32) + bg_ref[...]
    o_ref[...] = a * _sigmoid(gate)


def _norm_glu(x, g, w, layer, b, *, tm=1024, tn=512):
    m, k = x.shape
    half = w.shape[2] // 2
    nj = half // tn
    return pl.pallas_call(
        functools.partial(_norm_glu_kernel, tm=tm),
        grid=(m // tm, nj),
        in_specs=[
            pl.BlockSpec((tm, k), lambda i, j: (i, 0)),
            pl.BlockSpec((1, k), lambda i, j: (0, 0)),
            pl.BlockSpec((None, k, tn), lambda i, j: (layer, 0, j)),
            pl.BlockSpec((None, k, tn), lambda i, j: (layer, 0, j + nj)),
            pl.BlockSpec((1, tn), lambda i, j: (0, j)),
            pl.BlockSpec((1, tn), lambda i, j: (0, j + nj)),
        ],
        out_specs=pl.BlockSpec((tm, tn), lambda i, j: (i, j)),
        out_shape=jax.ShapeDtypeStruct((m, half), F32),
        scratch_shapes=[pltpu.VMEM((tm, k), BF16)],
        compiler_params=_params("parallel", "arbitrary"),
        name="norm_glu",
    )(x, g, w, w, b, b)


def _norm_lru_in_kernel(x_ref, g_ref, wu_ref, wy_ref, u_ref, y_ref, xn_ref, *, tm):
    @pl.when(pl.program_id(1) == 0)
    def _():
        _rms_rows_to(xn_ref, x_ref, g_ref, tm, 1e-6)
    xn = xn_ref[...]
    u_ref[...] = jnp.dot(xn, wu_ref[...].astype(BF16), preferred_element_type=F32)
    y_ref[...] = _gelu_tanh(jnp.dot(xn, wy_ref[...].astype(BF16), preferred_element_type=F32))


def _norm_lru_in(x, g, w, layer, *, tm=1024, tn=512):
    m, k = x.shape
    half = w.shape[2] // 2
    nj = half // tn
    return pl.pallas_call(
        functools.partial(_norm_lru_in_kernel, tm=tm),
        grid=(m // tm, nj),
        in_specs=[
            pl.BlockSpec((tm, k), lambda i, j: (i, 0)),
            pl.BlockSpec((1, k), lambda i, j: (0, 0)),
            pl.BlockSpec((None, k, tn), lambda i, j: (layer, 0, j)),
            pl.BlockSpec((None, k, tn), lambda i, j: (layer, 0, j + nj)),
        ],
        out_specs=[pl.BlockSpec((tm, tn), lambda i, j: (i, j)),
                   pl.BlockSpec((tm, tn), lambda i, j: (i, j))],
        out_shape=[jax.ShapeDtypeStruct((m, half), F32), jax.ShapeDtypeStruct((m, half), F32)],
        scratch_shapes=[pltpu.VMEM((tm, k), BF16)],
        compiler_params=_params("parallel", "arbitrary"),
        name="norm_lru_in",
    )(x, g, w, w)


def _proj_res_kernel(a_ref, w_ref, b_ref, r_ref, o_ref, wb_ref):
    @pl.when(pl.program_id(1) == 0)
    def _():
        wb_ref[...] = w_ref[...].astype(BF16)
    acc = jnp.dot(a_ref[...], wb_ref[...], preferred_element_type=F32)
    o_ref[...] = r_ref[...] + (acc + b_ref[...])


def _proj_res(a, w, layer, b, res, *, tm=1024, tn=1024):
    m, k = a.shape
    n = w.shape[2]
    return pl.pallas_call(
        _proj_res_kernel,
        grid=(n // tn, m // tm),
        in_specs=[
            pl.BlockSpec((tm, k), lambda j, i: (i, 0)),
            pl.BlockSpec((None, k, tn), lambda j, i: (layer, 0, j)),
            pl.BlockSpec((1, tn), lambda j, i: (0, j)),
            pl.BlockSpec((tm, tn), lambda j, i: (i, j)),
        ],
        out_specs=pl.BlockSpec((tm, tn), lambda j, i: (i, j)),
        out_shape=jax.ShapeDtypeStruct((m, n), F32),
        scratch_shapes=[pltpu.VMEM((k, tn), BF16)],
        compiler_params=_params("parallel", "arbitrary"),
        name="proj_res",
    )(a, w, b, res)


def _mlp_kernel(*refs, tm, final_norm):
    if final_norm:
        x_ref, g_ref, w1_ref, w2_ref, fg_ref, o_ref, xn_ref = refs
    else:
        x_ref, g_ref, w1_ref, w2_ref, o_ref, xn_ref = refs
    f = pl.program_id(1)

    @pl.when(f == 0)
    def _():
        _rms_rows_to(xn_ref, x_ref, g_ref, tm, 1e-6)
        o_ref[...] = x_ref[...]

    z = jnp.maximum(jnp.dot(xn_ref[...], w1_ref[...], preferred_element_type=F32), 0.0)
    o_ref[...] += jnp.dot((z * z).astype(BF16), w2_ref[...], preferred_element_type=F32)

    if final_norm:
        @pl.when(f == pl.num_programs(1) - 1)
        def _():
            def body(c, carry):
                rows = pl.ds(pl.multiple_of(c * NORM_ROWS, NORM_ROWS), NORM_ROWS)
                y = o_ref[rows, :]
                ms = jnp.mean(y * y, axis=-1, keepdims=True)
                o_ref[rows, :] = y * lax.rsqrt(ms + 1e-6) * fg_ref[...]
                return carry
            lax.fori_loop(0, tm // NORM_ROWS, body, 0)


def _mlp(x, g, w1, w2, final_g=None, *, tm=512, tf=1024):
    m, d = x.shape
    ff = w1.shape[1]
    final_norm = final_g is not None
    row_spec = pl.BlockSpec((1, d), lambda i, f: (0, 0))
    return pl.pallas_call(
        functools.partial(_mlp_kernel, tm=tm, final_norm=final_norm),
        grid=(m // tm, ff // tf),
        in_specs=[
            pl.BlockSpec((tm, d), lambda i, f: (i, 0)),
            row_spec,
            pl.BlockSpec((d, tf), lambda i, f: (0, f)),
            pl.BlockSpec((tf, d), lambda i, f: (f, 0)),
        ] + ([row_spec] if final_norm else []),
        out_specs=pl.BlockSpec((tm, d), lambda i, f: (i, 0)),
        out_shape=jax.ShapeDtypeStruct((m, d), F32),
        scratch_shapes=[pltpu.VMEM((tm, d), BF16)],
        compiler_params=_params("parallel", "arbitrary"),
        name="mlp",
    )(x, g, w1, w2, *((final_g,) if final_norm else ()))


ATTN_T = 512
ATTN_STRIP = 32
LOG2E = math.log2(math.e)


def _bucket_tiles(t):
    r = np.arange(t, dtype=np.int32)[:, None]
    c = np.arange(t, dtype=np.int32)[None, :]
    tiles = []
    for d in range(2):
        n = np.maximum(d * t + r - c, 0)
        max_exact = NUM_BUCKETS // 2
        nf = np.maximum(n, 1).astype(np.float32)
        large = max_exact + (np.log(nf / np.float32(max_exact)) / np.float32(math.log(MAX_DISTANCE / max_exact))
                             * np.float32(NUM_BUCKETS - max_exact)).astype(np.int32)
        large = np.minimum(large, NUM_BUCKETS - 1)
        bucket = np.where(n < max_exact, n, large).astype(np.int32)
        tiles.append(np.where(d * t + r - c >= 0, bucket, -1))
    return np.stack(tiles)


def _bias_tiles_kernel(rb_ref, bucket_ref, o_ref):
    h = pl.program_id(0)
    bucket = bucket_ref[0]
    acc = jnp.zeros(bucket.shape, F32)
    for j in range(NUM_BUCKETS):
        acc = jnp.where(bucket == j, rb_ref[j, h], acc)
    acc = (acc - rb_ref[NUM_BUCKETS - 1, h]) * LOG2E
    o_ref[0, 0] = jnp.where(bucket < 0, MASK_VALUE, acc)


def _bias_tiles(rel_bias, t):
    buckets = jnp.asarray(_bucket_tiles(t))
    return pl.pallas_call(
        _bias_tiles_kernel,
        grid=(N_HEADS, 2),
        in_specs=[
            pl.BlockSpec(memory_space=pltpu.SMEM),
            pl.BlockSpec((1, t, t), lambda h, d: (d, 0, 0)),
        ],
        out_specs=pl.BlockSpec((1, 1, t, t), lambda h, d: (h, d, 0, 0)),
        out_shape=jax.ShapeDtypeStruct((N_HEADS, 2, t, t), F32),
        compiler_params=_params("parallel", "parallel"),
        name="bias_tiles",
    )(rel_bias, buckets)


def _attn_kernel(lq1_ref, lk1_ref, lq2_ref, lk2_ref, sg_ref, q_ref, k_ref, v_ref, bias_ref, w1_ref, w2_ref,
                 o_ref, w1b_ref, w2b_ref, m_ref, l_ref, alpha_ref, acc_ref, s_ref, p_ref, *, t, lambda_init):
    qi = pl.program_id(2)
    chunks = [slice(j * LANES, (j + 1) * LANES) for j in range(t // LANES)]
    _cast_slabs(w1_ref, w2_ref, w1b_ref, w2b_ref)

    m_ref[...] = jnp.full(m_ref.shape, MASK_VALUE, F32)
    l_ref[...] = jnp.zeros(l_ref.shape, F32)
    acc_ref[...] = jnp.zeros(acc_ref.shape, F32)
    alpha_ref[1] = jnp.zeros(alpha_ref.shape[1:], F32)
    p_ref[1] = jnp.zeros(p_ref.shape[1:], BF16)

    def start(ki):
        return pl.multiple_of(jnp.maximum(ki, 0) * t, t)

    def qk_logits(ki, slot):
        for c in range(2):
            lanes = slice(c * HEAD_DIM, (c + 1) * HEAD_DIM)
            s_ref[slot, c] = lax.dot_general(q_ref[:, lanes], k_ref[pl.ds(start(ki), t), lanes],
                                             (((1,), (1,)), ((), ())), preferred_element_type=F32)

    def accumulate(ki, slot):
        v = v_ref[pl.ds(start(ki), t), :]
        for c in range(2):
            alpha = alpha_ref[slot, c]
            pv = jnp.dot(p_ref[slot, c], v, preferred_element_type=F32)
            acc_ref[c] = jnp.concatenate([alpha, alpha], axis=1) * acc_ref[c] + pv

    strips = [slice(r * ATTN_STRIP, (r + 1) * ATTN_STRIP) for r in range(t // ATTN_STRIP)]

    def block(ki, cur, bias_index, has_next):
        other = 1 - cur

        def logits(c, rows):
            s = s_ref[cur, c, rows, :]
            if bias_index is not None:
                s = s + bias_ref[0, bias_index, rows, :]
            return s

        part_max = [[], []]
        for rows in strips:
            for c in range(2):
                s = logits(c, rows)
                part = s[:, chunks[0]]
                for ch in chunks[1:]:
                    part = jnp.maximum(part, s[:, ch])
                part_max[c].append(part)

        accumulate(ki - 1, other)
        m_new, alpha = [], []
        for c in range(2):
            m_prev = m_ref[c]
            m_c = jnp.maximum(m_prev, jnp.max(jnp.concatenate(part_max[c], axis=0), axis=1, keepdims=True))
            alpha.append(jnp.exp2(m_prev - m_c))
            alpha_ref[cur, c] = alpha[c]
            m_ref[c] = m_c
            m_new.append(m_c)

        part_sum = [[], []]
        for rows in strips:
            for c in range(2):
                s = logits(c, rows)
                m_rows = m_new[c][rows, :]
                part = None
                for ch in chunks:
                    p = jnp.exp2(s[:, ch] - m_rows)
                    part = p if part is None else part + p
                    p_ref[cur, c, rows, ch] = p.astype(BF16)
                part_sum[c].append(part)

        if has_next:
            qk_logits(ki + 1, other)
        for c in range(2):
            row_sum = jnp.sum(jnp.concatenate(part_sum[c], axis=0), axis=1, keepdims=True)
            l_ref[c] = alpha[c] * l_ref[c] + row_sum

    qk_logits(0, 0)

    n_far = jnp.maximum(qi - 1, 0)

    def far_pair(j, carry):
        block(2 * j, 0, None, True)
        block(2 * j + 1, 1, None, True)
        return carry
    lax.fori_loop(0, n_far // 2, far_pair, 0)

    @pl.when(qi % 2 == 1)
    def _():
        block(qi - 1, 0, 1, True)
        block(qi, 1, 0, False)
        accumulate(qi, 1)

    @pl.when(qi % 2 == 0)
    def _():
        @pl.when(qi >= 2)
        def _():
            block(qi - 2, 0, None, True)
            block(qi - 1, 1, 1, True)
        block(qi, 0, 0, False)
        accumulate(qi, 0)

    lam = (jnp.exp(jnp.sum(lq1_ref[...] * lk1_ref[...], keepdims=True))
           - jnp.exp(jnp.sum(lq2_ref[...] * lk2_ref[...], keepdims=True)) + lambda_init)
    inv0 = 1.0 / l_ref[0]
    inv1 = lam / l_ref[1]
    o = (acc_ref[0] * jnp.concatenate([inv0, inv0], axis=1)
         - acc_ref[1] * jnp.concatenate([inv1, inv1], axis=1))
    ms = jnp.mean(o * o, axis=-1, keepdims=True)
    o = o * lax.rsqrt(ms + 1e-5) * sg_ref[...] * (1.0 - lambda_init)
    o_ref[...] = o.astype(o_ref.dtype)


def _diff_attention(qkv, bias_tiles, lq1, lk1, lq2, lk2, subln_g, mlp_w1, mlp_w2, layer, *, batch, seq, lambda_init):
    t = ATTN_T
    nq = seq // t
    grid = (batch, N_HEADS, nq)
    lam_spec = pl.BlockSpec((1, HEAD_DIM), lambda b, h, i: (0, 0))
    cast_in, cast_out, cast_shapes = _mlp_cast_specs(grid, layer)
    return pl.pallas_call(
        functools.partial(_attn_kernel, t=t, lambda_init=lambda_init),
        grid=grid,
        in_specs=[
            lam_spec, lam_spec, lam_spec, lam_spec,
            pl.BlockSpec((1, HEAD_W), lambda b, h, i: (0, 0)),
            pl.BlockSpec((t, HEAD_W), lambda b, h, i: (b * nq + i, h)),
            pl.BlockSpec((seq, HEAD_W), lambda b, h, i: (b, N_HEADS + h)),
            pl.BlockSpec((seq, HEAD_W), lambda b, h, i: (b, 2 * N_HEADS + h)),
            pl.BlockSpec((1, 2, t, t), lambda b, h, i: (h, 0, 0, 0)),
        ] + cast_in,
        out_specs=[pl.BlockSpec((t, HEAD_W), lambda b, h, i: (b * nq + i, h))] + cast_out,
        out_shape=[jax.ShapeDtypeStruct((batch * seq, D_MODEL), BF16)] + cast_shapes,
        scratch_shapes=[pltpu.VMEM((2, t, LANES), F32), pltpu.VMEM((2, t, LANES), F32),
                        pltpu.VMEM((2, 2, t, LANES), F32),
                        pltpu.VMEM((2, t, HEAD_W), F32), pltpu.VMEM((2, 2, t, t), F32),
                        pltpu.VMEM((2, 2, t, t), BF16)],
        compiler_params=_params("parallel", "parallel", "arbitrary"),
        name="diff_attn",
    )(lq1, lk1, lq2, lk2, subln_g, qkv, qkv, qkv, bias_tiles, mlp_w1, mlp_w2)


CONV_T = 256
CONV_HALO = 32
CONV_ROWS = 64
CONV_LANES = 256


def _conv_ln_kernel(halo_ref, u_ref, w_ref, b_ref, g_ref, beta_ref, w1_ref, w2_ref,
                    o_ref, w1b_ref, w2b_ref, sh_ref, y_ref, *, tiles_per_seq):
    _cast_slabs(w1_ref, w2_ref, w1b_ref, w2b_ref)
    i = pl.program_id(0)
    first = (i % tiles_per_seq) == 0

    @pl.when(first)
    def _():
        sh_ref[0, 0:CONV_HALO, :] = jnp.zeros((CONV_HALO, D_MODEL), F32)

    @pl.when(jnp.logical_not(first))
    def _():
        sh_ref[0, 0:CONV_HALO, :] = halo_ref[...]

    sh_ref[0, CONV_HALO:, :] = u_ref[...]
    window = CONV_HALO + CONV_T
    for s in range(1, SUBLANES):
        for c0 in range(0, D_MODEL, CONV_LANES):
            lanes = slice(c0, c0 + CONV_LANES)
            sh_ref[s, 0:window - SUBLANES, lanes] = sh_ref[0, s:s + window - SUBLANES, lanes]

    base = CONV_HALO - (CONV_WIDTH - 1)
    for r0 in range(0, CONV_T, CONV_ROWS):
        for c0 in range(0, D_MODEL, CONV_LANES):
            lanes = slice(c0, c0 + CONV_LANES)
            acc = jnp.zeros((CONV_ROWS, CONV_LANES), F32)
            for j in range(CONV_WIDTH):
                a, s = divmod(base + j, SUBLANES)
                x0 = r0 + a * SUBLANES
                acc = acc + w_ref[j:j + 1, lanes] * sh_ref[s, x0:x0 + CONV_ROWS, lanes]
            y_ref[r0:r0 + CONV_ROWS, lanes] = acc + b_ref[:, lanes]

    def ln_body(c, carry):
        r0 = pl.multiple_of(c * CONV_ROWS, CONV_ROWS)
        y = y_ref[pl.ds(r0, CONV_ROWS), :]
        mu = jnp.mean(y, axis=-1, keepdims=True)
        yc = y - mu
        var = jnp.mean(yc * yc, axis=-1, keepdims=True)
        z = yc * lax.rsqrt(var + 1e-5) * g_ref[...] + beta_ref[...]
        o_ref[pl.ds(r0, CONV_ROWS), :] = (z * _sigmoid(z)).astype(o_ref.dtype)
        return carry
    lax.fori_loop(0, CONV_T // CONV_ROWS, ln_body, 0)


def _conv_ln(u, dw_w, dw_b, ln_g, ln_b, mlp_w1, mlp_w2, layer, *, seq):
    m, d = u.shape
    tiles_per_seq = seq // CONV_T
    ratio = CONV_T // CONV_HALO
    grid = (m // CONV_T,)
    row = lambda i: (0, 0)
    cast_in, cast_out, cast_shapes = _mlp_cast_specs(grid, layer)
    return pl.pallas_call(
        functools.partial(_conv_ln_kernel, tiles_per_seq=tiles_per_seq),
        grid=grid,
        in_specs=[
            pl.BlockSpec((CONV_HALO, d), lambda i: (jnp.maximum(i * ratio - 1, 0), 0)),
            pl.BlockSpec((CONV_T, d), lambda i: (i, 0)),
            pl.BlockSpec((CONV_WIDTH, d), row),
            pl.BlockSpec((1, d), row), pl.BlockSpec((1, d), row), pl.BlockSpec((1, d), row),
        ] + cast_in,
        out_specs=[pl.BlockSpec((CONV_T, d), lambda i: (i, 0))] + cast_out,
        out_shape=[jax.ShapeDtypeStruct((m, d), BF16)] + cast_shapes,
        scratch_shapes=[pltpu.VMEM((SUBLANES, CONV_HALO + CONV_T, d), F32), pltpu.VMEM((CONV_T, d), F32)],
        compiler_params=_params("parallel"),
        name="conv_ln",
    )(u, u, dw_w, dw_b, ln_g, ln_b, mlp_w1, mlp_w2)


LRU_T = 256
LRU_HALO = 8


def _lru_kernel(halo_ref, u_ref, y_ref, cw_ref, cb_ref, wa_ref, ba_ref, wi_ref, bi_ref, lam_ref, w1_ref, w2_ref,
                o_ref, w1b_ref, w2b_ref, buf_ref, carry_ref):
    _cast_slabs(w1_ref, w2_ref, w1b_ref, w2b_ref)
    ti = pl.program_id(2)
    first = ti == 0
    buf_ref[LRU_HALO:, :] = u_ref[...]

    @pl.when(first)
    def _():
        buf_ref[0:LRU_HALO, :] = jnp.zeros((LRU_HALO, LRU_BLOCK_SIZE), F32)
        carry_ref[...] = jnp.zeros(carry_ref.shape, F32)

    @pl.when(jnp.logical_not(first))
    def _():
        buf_ref[0:LRU_HALO, :] = halo_ref[...]

    base = LRU_HALO - (LRU_CONV_WIDTH - 1)
    uc = jnp.zeros((LRU_T, LRU_BLOCK_SIZE), F32)
    for j in range(LRU_CONV_WIDTH):
        uc = uc + cw_ref[j:j + 1, :] * buf_ref[base + j:base + j + LRU_T, :]
    uc = uc + cb_ref[...]

    ucb = uc.astype(BF16)
    r = _sigmoid(jnp.dot(ucb, wa_ref[0].astype(BF16), preferred_element_type=F32) + ba_ref[...])
    gate_i = _sigmoid(jnp.dot(ucb, wi_ref[0].astype(BF16), preferred_element_type=F32) + bi_ref[...])
    neg_lam = -lam_ref[...]
    softplus = jnp.maximum(neg_lam, 0.0) + jnp.log1p(jnp.exp(-jnp.abs(neg_lam)))
    log_a = -LRU_C * r * softplus
    a = jnp.exp(log_a)
    mult = jnp.sqrt(-jnp.tanh(log_a) * (1.0 + a * a))
    b = mult * (gate_i * uc)

    row = lax.broadcasted_iota(jnp.int32, (LRU_T, LRU_BLOCK_SIZE), 0)
    s = 1
    while s < SUBLANES:
        valid = row >= s
        a_sh = jnp.where(valid, pltpu.roll(a, s, 0), 1.0)
        b_sh = jnp.where(valid, pltpu.roll(b, s, 0), 0.0)
        b = a * b_sh + b
        a = a * a_sh
        s *= 2
    while s < LRU_T:
        b = jnp.concatenate([b[:s], a[s:] * b[:-s] + b[s:]], axis=0)
        a = jnp.concatenate([a[:s], a[s:] * a[:-s]], axis=0)
        s *= 2
    hs = a * carry_ref[0:1, :] + b
    carry_ref[0:1, :] = hs[LRU_T - 1:LRU_T, :]
    o_ref[...] = (hs * y_ref[...]).astype(o_ref.dtype)


def _lru_core(u, y, conv_w, conv_b, w_a, b_a, w_i, b_i, lam, mlp_w1, mlp_w2, layer, *, batch, seq):
    m, d = u.shape
    nt = seq // LRU_T
    ratio = LRU_T // LRU_HALO
    c = LRU_BLOCK_SIZE
    grid = (batch, LRU_BLOCKS, nt)
    chan = lambda b, g, t: (0, g)
    tile = lambda b, g, t: (b * nt + t, g)
    gate_w = pl.BlockSpec((1, c, c), lambda b, g, t: (g, 0, 0))
    cast_in, cast_out, cast_shapes = _mlp_cast_specs(grid, layer)
    return pl.pallas_call(
        _lru_kernel,
        grid=grid,
        in_specs=[
            pl.BlockSpec((LRU_HALO, c), lambda b, g, t: (jnp.maximum((b * nt + t) * ratio - 1, 0), g)),
            pl.BlockSpec((LRU_T, c), tile),
            pl.BlockSpec((LRU_T, c), tile),
            pl.BlockSpec((LRU_CONV_WIDTH, c), chan),
            pl.BlockSpec((1, c), chan),
            gate_w, pl.BlockSpec((1, c), chan),
            gate_w, pl.BlockSpec((1, c), chan),
            pl.BlockSpec((1, c), chan),
        ] + cast_in,
        out_specs=[pl.BlockSpec((LRU_T, c), tile)] + cast_out,
        out_shape=[jax.ShapeDtypeStruct((m, d), BF16)] + cast_shapes,
        scratch_shapes=[pltpu.VMEM((LRU_HALO + LRU_T, c), F32), pltpu.VMEM((8, c), F32)],
        compiler_params=_params("parallel", "parallel", "arbitrary"),
        name="lru_core",
    )(u, u, y, conv_w, conv_b, w_a, b_a, w_i, b_i, lam, mlp_w1, mlp_w2)


def kernel(x, rel_bias, mixer_norm_g, attn_w_qkv, attn_lq1, attn_lk1, attn_lq2, attn_lk2, attn_subln_g, attn_w_o, conv_w_in, conv_b_in, conv_dw_w, conv_dw_b, conv_ln_g, conv_ln_b, conv_w_out, conv_b_out, lru_w_in, lru_conv_w, lru_conv_b, lru_w_a, lru_b_a, lru_w_i, lru_b_i, lru_lambda, lru_w_out, mlp_norm_g, mlp_w1, mlp_w2, final_norm_g):
    batch, seq, d = x.shape
    h = x.reshape(batch * seq, d)
    row = lambda v: v.reshape(1, -1)
    zero_bias = jnp.zeros((1, d), F32)
    q_scale = jnp.concatenate([jnp.full((1, d), HEAD_DIM ** -0.5 * LOG2E, F32), jnp.ones((1, 2 * d), F32)], axis=1)
    bias_tiles = _bias_tiles(rel_bias, ATTN_T)

    ia = ic = il = 0
    for layer in range(DEPTH):
        kind = layer % N_MIXERS
        g = row(mixer_norm_g[layer])
        if kind == 0:
            lambda_init = 0.8 - 0.6 * math.exp(-0.3 * layer)
            qkv = _norm_proj(h, g, attn_w_qkv, ia, q_scale)
            att, w1, w2 = _diff_attention(qkv, bias_tiles, row(attn_lq1[ia]), row(attn_lk1[ia]),
                                          row(attn_lq2[ia]), row(attn_lk2[ia]), row(attn_subln_g[ia]),
                                          mlp_w1, mlp_w2, layer, batch=batch, seq=seq, lambda_init=lambda_init)
            h = _proj_res(att, attn_w_o, ia, zero_bias, h)
            ia += 1
        elif kind == 1:
            u = _norm_glu(h, g, conv_w_in, ic, row(conv_b_in[ic]))
            act, w1, w2 = _conv_ln(u, conv_dw_w[ic], row(conv_dw_b[ic]), row(conv_ln_g[ic]), row(conv_ln_b[ic]),
                                   mlp_w1, mlp_w2, layer, seq=seq)
            h = _proj_res(act, conv_w_out, ic, row(conv_b_out[ic]), h)
            ic += 1
        else:
            u, y = _norm_lru_in(h, g, lru_w_in, il)
            hy, w1, w2 = _lru_core(u, y, lru_conv_w[il], row(lru_conv_b[il]), lru_w_a[il], row(lru_b_a[il]),
                                   lru_w_i[il], row(lru_b_i[il]), row(lru_lambda[il]), mlp_w1, mlp_w2, layer,
                                   batch=batch, seq=seq)
            h = _proj_res(hy, lru_w_out, il, zero_bias, h)
            il += 1
        final_g = row(final_norm_g) if layer == DEPTH - 1 else None
        h = _mlp(h, row(mlp_norm_g[layer]), w1, w2, final_g)
    return h.reshape(batch, seq, d)
```

```python
import functools
import math

import numpy as np
import jax
import jax.numpy as jnp
from jax import lax
from jax.experimental import pallas as pl
from jax.experimental.pallas import tpu as pltpu

F32 = jnp.float32
BF16 = jnp.bfloat16

D_MODEL = 2048
DEPTH = 4
N_MIXERS = 3
HEAD_DIM = 128
HEAD_W = 2 * HEAD_DIM
N_HEADS = D_MODEL // HEAD_W
NUM_BUCKETS = 32
MAX_DISTANCE = 128
MASK_VALUE = -1e30
CONV_WIDTH = 31
LRU_BLOCKS = 8
LRU_BLOCK_SIZE = D_MODEL // LRU_BLOCKS
LRU_CONV_WIDTH = 4
LRU_C = 8.0
D_FF = 4 * D_MODEL

VMEM_LIMIT = 56 * 1024 * 1024
NORM_ROWS = 128
LANES = 128
SUBLANES = 8
BF16_ROWS = 16


def _params(*sem):
    return pltpu.CompilerParams(dimension_semantics=sem, vmem_limit_bytes=VMEM_LIMIT)


def _rms_rows_to(dst_ref, x_ref, g_ref, rows, eps):
    def body(c, carry):
        r0 = pl.multiple_of(c * NORM_ROWS, NORM_ROWS)
        x = x_ref[pl.ds(r0, NORM_ROWS), :]
        ms = jnp.mean(x * x, axis=-1, keepdims=True)
        dst_ref[pl.ds(r0, NORM_ROWS), :] = (x * lax.rsqrt(ms + eps) * g_ref[...]).astype(BF16)
        return carry
    lax.fori_loop(0, rows // NORM_ROWS, body, 0)


def _mlp_cast_specs(grid, layer):
    d, ff = D_MODEL, D_FF
    steps = math.prod(grid)

    def lin(*idx):
        n = 0
        for i, g in zip(idx, grid):
            n = n * g + i
        return n

    split = max(1, BF16_ROWS * steps // d)
    r1, c1, r2 = d * split // steps, ff // split, ff // steps
    in_specs = [pl.BlockSpec((None, r1, c1), lambda *idx: (layer, lin(*idx) // split, lin(*idx) % split)),
                pl.BlockSpec((None, r2, d), lambda *idx: (layer, lin(*idx), 0))]
    out_specs = [pl.BlockSpec((r1, c1), lambda *idx: (lin(*idx) // split, lin(*idx) % split)),
                 pl.BlockSpec((r2, d), lambda *idx: (lin(*idx), 0))]
    out_shapes = [jax.ShapeDtypeStruct((d, ff), BF16), jax.ShapeDtypeStruct((ff, d), BF16)]
    return in_specs, out_specs, out_shapes


def _cast_slabs(w1_ref, w2_ref, w1b_ref, w2b_ref):
    w1b_ref[...] = w1_ref[...].astype(BF16)
    w2b_ref[...] = w2_ref[...].astype(BF16)


def _sigmoid(x):
    return 1.0 / (1.0 + jnp.exp(-x))


def _gelu_tanh(x):
    c = math.sqrt(2.0 / math.pi)
    return 0.5 * x * (1.0 + jnp.tanh(c * (x + 0.044715 * (x * x * x))))


def _norm_proj_kernel(x_ref, g_ref, w_ref, s_ref, o_ref, xn_ref, *, tm):
    @pl.when(pl.program_id(1) == 0)
    def _():
        _rms_rows_to(xn_ref, x_ref, g_ref, tm, 1e-6)
    acc = jnp.dot(xn_ref[...], w_ref[...].astype(BF16), preferred_element_type=F32)
    o_ref[...] = (acc * s_ref[...]).astype(o_ref.dtype)


def _norm_proj(x, g, w, layer, col_scale, *, tm=1024, tn=1024):
    m, k = x.shape
    n = w.shape[2]
    return pl.pallas_call(
        functools.partial(_norm_proj_kernel, tm=tm),
        grid=(m // tm, n // tn),
        in_specs=[
            pl.BlockSpec((tm, k), lambda i, j: (i, 0)),
            pl.BlockSpec((1, k), lambda i, j: (0, 0)),
            pl.BlockSpec((None, k, tn), lambda i, j: (layer, 0, j)),
            pl.BlockSpec((1, tn), lambda i, j: (0, j)),
        ],
        out_specs=pl.BlockSpec((tm, tn), lambda i, j: (i, j)),
        out_shape=jax.ShapeDtypeStruct((m, n), BF16),
        scratch_shapes=[pltpu.VMEM((tm, k), BF16)],
        compiler_params=_params("parallel", "arbitrary"),
        name="norm_proj",
    )(x, g, w, col_scale)


def _norm_glu_kernel(x_ref, g_ref, wa_ref, wg_ref, ba_ref, bg_ref, o_ref, xn_ref, *, tm):
    @pl.when(pl.program_id(1) == 0)
    def _():
        _rms_rows_to(xn_ref, x_ref, g_ref, tm, 1e-6)
    xn = xn_ref[...]
    a = jnp.dot(xn, wa_ref[...].astype(BF16), preferred_element_type=F32) + ba_ref[...]
    gate = jnp.dot(xn, wg_ref[...].astype(BF16), preferred_element_type=F32) + bg_ref[...]
    o_ref[...] = a * _sigmoid(gate)


def _norm_glu(x, g, w, layer, b, *, tm=1024, tn=512):
    m, k = x.shape
    half = w.shape[2] // 2
    nj = half // tn
    return pl.pallas_call(
        functools.partial(_norm_glu_kernel, tm=tm),
        grid=(m // tm, nj),
        in_specs=[
            pl.BlockSpec((tm, k), lambda i, j: (i, 0)),
            pl.BlockSpec((1, k), lambda i, j: (0, 0)),
            pl.BlockSpec((None, k, tn), lambda i, j: (layer, 0, j)),
            pl.BlockSpec((None, k, tn), lambda i, j: (layer, 0, j + nj)),
            pl.BlockSpec((1, tn), lambda i, j: (0, j)),
            pl.BlockSpec((1, tn), lambda i, j: (0, j + nj)),
        ],
        out_specs=pl.BlockSpec((tm, tn), lambda i, j: (i, j)),
        out_shape=jax.ShapeDtypeStruct((m, half), F32),
        scratch_shapes=[pltpu.VMEM((tm, k), BF16)],
        compiler_params=_params("parallel", "arbitrary"),
        name="norm_glu",
    )(x, g, w, w, b, b)


def _norm_lru_in_kernel(x_ref, g_ref, wu_ref, wy_ref, u_ref, y_ref, xn_ref, *, tm):
    @pl.when(pl.program_id(1) == 0)
    def _():
        _rms_rows_to(xn_ref, x_ref, g_ref, tm, 1e-6)
    xn = xn_ref[...]
    u_ref[...] = jnp.dot(xn, wu_ref[...].astype(BF16), preferred_element_type=F32)
    y_ref[...] = _gelu_tanh(jnp.dot(xn, wy_ref[...].astype(BF16), preferred_element_type=F32))


def _norm_lru_in(x, g, w, layer, *, tm=1024, tn=512):
    m, k = x.shape
    half = w.shape[2] // 2
    nj = half // tn
    return pl.pallas_call(
        functools.partial(_norm_lru_in_kernel, tm=tm),
        grid=(m // tm, nj),
        in_specs=[
            pl.BlockSpec((tm, k), lambda i, j: (i, 0)),
            pl.BlockSpec((1, k), lambda i, j: (0, 0)),
            pl.BlockSpec((None, k, tn), lambda i, j: (layer, 0, j)),
            pl.BlockSpec((None, k, tn), lambda i, j: (layer, 0, j + nj)),
        ],
        out_specs=[pl.BlockSpec((tm, tn), lambda i, j: (i, j)),
                   pl.BlockSpec((tm, tn), lambda i, j: (i, j))],
        out_shape=[jax.ShapeDtypeStruct((m, half), F32), jax.ShapeDtypeStruct((m, half), F32)],
        scratch_shapes=[pltpu.VMEM((tm, k), BF16)],
        compiler_params=_params("parallel", "arbitrary"),
        name="norm_lru_in",
    )(x, g, w, w)


def _proj_res_kernel(a_ref, w_ref, b_ref, r_ref, o_ref, wb_ref):
    @pl.when(pl.program_id(1) == 0)
    def _():
        wb_ref[...] = w_ref[...].astype(BF16)
    acc = jnp.dot(a_ref[...], wb_ref[...], preferred_element_type=F32)
    o_ref[...] = r_ref[...] + (acc + b_ref[...])


def _proj_res(a, w, layer, b, res, *, tm=512, tn=2048):
    m, k = a.shape
    n = w.shape[2]
    return pl.pallas_call(
        _proj_res_kernel,
        grid=(n // tn, m // tm),
        in_specs=[
            pl.BlockSpec((tm, k), lambda j, i: (i, 0)),
            pl.BlockSpec((None, k, tn), lambda j, i: (layer, 0, j), pipeline_mode=pl.Buffered(1)),
            pl.BlockSpec((1, tn), lambda j, i: (0, j)),
            pl.BlockSpec((tm, tn), lambda j, i: (i, j)),
        ],
        out_specs=pl.BlockSpec((tm, tn), lambda j, i: (i, j)),
        out_shape=jax.ShapeDtypeStruct((m, n), F32),
        scratch_shapes=[pltpu.VMEM((k, tn), BF16)],
        compiler_params=_params("parallel", "arbitrary"),
        name="proj_res",
    )(a, w, b, res)


def _mlp_kernel(*refs, tm, final_norm, cast):
    refs = list(refs)
    x_ref, g_ref, w1_ref, w2_ref = refs[:4]
    del refs[:4]
    if final_norm:
        fg_ref = refs.pop(0)
    if cast:
        _cast_slabs(refs[0], refs[1], refs[3], refs[4])
        o_ref = refs[2]
    else:
        o_ref = refs[0]
    xn_ref = refs[-1]
    f = pl.program_id(1)

    @pl.when(f == 0)
    def _():
        _rms_rows_to(xn_ref, x_ref, g_ref, tm, 1e-6)
        o_ref[...] = x_ref[...]

    z = jnp.maximum(jnp.dot(xn_ref[...], w1_ref[...], preferred_element_type=F32), 0.0)
    o_ref[...] += jnp.dot((z * z).astype(BF16), w2_ref[...], preferred_element_type=F32)

    if final_norm:
        @pl.when(f == pl.num_programs(1) - 1)
        def _():
            def body(c, carry):
                rows = pl.ds(pl.multiple_of(c * NORM_ROWS, NORM_ROWS), NORM_ROWS)
                y = o_ref[rows, :]
                ms = jnp.mean(y * y, axis=-1, keepdims=True)
                o_ref[rows, :] = y * lax.rsqrt(ms + 1e-6) * fg_ref[...]
                return carry
            lax.fori_loop(0, tm // NORM_ROWS, body, 0)


def _mlp(x, g, w1, w2, final_g=None, mlp_weights=None, cast_layer=None, *, tm=512, tf=1024):
    m, d = x.shape
    ff = w1.shape[1]
    final_norm = final_g is not None
    cast = cast_layer is not None
    grid = (m // tm, ff // tf)
    row_spec = pl.BlockSpec((1, d), lambda i, f: (0, 0))
    cast_in, cast_out, cast_shapes = _mlp_cast_specs(grid, cast_layer) if cast else ([], [], [])
    return pl.pallas_call(
        functools.partial(_mlp_kernel, tm=tm, final_norm=final_norm, cast=cast),
        grid=grid,
        in_specs=[
            pl.BlockSpec((tm, d), lambda i, f: (i, 0)),
            row_spec,
            pl.BlockSpec((d, tf), lambda i, f: (0, f)),
            pl.BlockSpec((tf, d), lambda i, f: (f, 0)),
        ] + ([row_spec] if final_norm else []) + cast_in,
        out_specs=[pl.BlockSpec((tm, d), lambda i, f: (i, 0))] + cast_out,
        out_shape=[jax.ShapeDtypeStruct((m, d), F32)] + cast_shapes,
        scratch_shapes=[pltpu.VMEM((tm, d), BF16)],
        compiler_params=_params("parallel", "arbitrary"),
        name="mlp",
    )(x, g, w1, w2, *((final_g,) if final_norm else ()), *(mlp_weights if cast else ()))


ATTN_T = 512
ATTN_STRIP = 32
LOG2E = math.log2(math.e)


def _bucket_tiles(t):
    r = np.arange(t, dtype=np.int32)[:, None]
    c = np.arange(t, dtype=np.int32)[None, :]
    tiles = []
    for d in range(2):
        n = np.maximum(d * t + r - c, 0)
        max_exact = NUM_BUCKETS // 2
        nf = np.maximum(n, 1).astype(np.float32)
        large = max_exact + (np.log(nf / np.float32(max_exact)) / np.float32(math.log(MAX_DISTANCE / max_exact))
                             * np.float32(NUM_BUCKETS - max_exact)).astype(np.int32)
        large = np.minimum(large, NUM_BUCKETS - 1)
        bucket = np.where(n < max_exact, n, large).astype(np.int32)
        tiles.append(np.where(d * t + r - c >= 0, bucket, -1))
    return np.stack(tiles)


def _bias_tiles_kernel(rb_ref, bucket_ref, o_ref):
    h = pl.program_id(0)
    bucket = bucket_ref[0]
    acc = jnp.zeros(bucket.shape, F32)
    for j in range(NUM_BUCKETS):
        acc = jnp.where(bucket == j, rb_ref[j, h], acc)
    acc = (acc - rb_ref[NUM_BUCKETS - 1, h]) * LOG2E
    o_ref[0, 0] = jnp.where(bucket < 0, MASK_VALUE, acc)


def _bias_tiles(rel_bias, t):
    buckets = jnp.asarray(_bucket_tiles(t))
    return pl.pallas_call(
        _bias_tiles_kernel,
        grid=(N_HEADS, 2),
        in_specs=[
            pl.BlockSpec(memory_space=pltpu.SMEM),
            pl.BlockSpec((1, t, t), lambda h, d: (d, 0, 0)),
        ],
        out_specs=pl.BlockSpec((1, 1, t, t), lambda h, d: (h, d, 0, 0)),
        out_shape=jax.ShapeDtypeStruct((N_HEADS, 2, t, t), F32),
        compiler_params=_params("parallel", "parallel"),
        name="bias_tiles",
    )(rel_bias, buckets)


def _attn_kernel(*refs, t, lambda_init, cast):
    lq1_ref, lk1_ref, lq2_ref, lk2_ref, sg_ref, q_ref, k_ref, v_ref, bias_ref = refs[:9]
    if cast:
        w1_ref, w2_ref, o_ref, w1b_ref, w2b_ref = refs[9:14]
        _cast_slabs(w1_ref, w2_ref, w1b_ref, w2b_ref)
    else:
        o_ref = refs[9]
    m_ref, l_ref, alpha_ref, acc_ref, s_ref, p_ref = refs[-6:]
    qi = pl.program_id(2)
    chunks = [slice(j * LANES, (j + 1) * LANES) for j in range(t // LANES)]

    m_ref[...] = jnp.full(m_ref.shape, MASK_VALUE, F32)
    l_ref[...] = jnp.zeros(l_ref.shape, F32)
    acc_ref[...] = jnp.zeros(acc_ref.shape, F32)
    alpha_ref[1] = jnp.zeros(alpha_ref.shape[1:], F32)
    p_ref[1] = jnp.zeros(p_ref.shape[1:], BF16)

    def start(ki):
        return pl.multiple_of(jnp.maximum(ki, 0) * t, t)

    def qk_logits(ki, slot):
        for c in range(2):
            lanes = slice(c * HEAD_DIM, (c + 1) * HEAD_DIM)
            s_ref[slot, c] = lax.dot_general(q_ref[:, lanes], k_ref[pl.ds(start(ki), t), lanes],
                                             (((1,), (1,)), ((), ())), preferred_element_type=F32)

    def accumulate(ki, slot):
        v = v_ref[pl.ds(start(ki), t), :]
        for c in range(2):
            alpha = alpha_ref[slot, c]
            pv = jnp.dot(p_ref[slot, c], v, preferred_element_type=F32)
            acc_ref[c] = jnp.concatenate([alpha, alpha], axis=1) * acc_ref[c] + pv

    strips = [slice(r * ATTN_STRIP, (r + 1) * ATTN_STRIP) for r in range(t // ATTN_STRIP)]

    def block(ki, cur, bias_index, has_next):
        other = 1 - cur

        def logits(c, rows):
            s = s_ref[cur, c, rows, :]
            if bias_index is not None:
                s = s + bias_ref[0, bias_index, rows, :]
            return s

        part_max = [[], []]
        for rows in strips:
            for c in range(2):
                s = logits(c, rows)
                part = s[:, chunks[0]]
                for ch in chunks[1:]:
                    part = jnp.maximum(part, s[:, ch])
                part_max[c].append(part)

        accumulate(ki - 1, other)
        m_new, alpha = [], []
        for c in range(2):
            m_prev = m_ref[c]
            m_c = jnp.maximum(m_prev, jnp.max(jnp.concatenate(part_max[c], axis=0), axis=1, keepdims=True))
            alpha.append(jnp.exp2(m_prev - m_c))
            alpha_ref[cur, c] = alpha[c]
            m_ref[c] = m_c
            m_new.append(m_c)

        part_sum = [[], []]
        for rows in strips:
            for c in range(2):
                s = logits(c, rows)
                m_rows = m_new[c][rows, :]
                part = None
                for ch in chunks:
                    p = jnp.exp2(s[:, ch] - m_rows)
                    part = p if part is None else part + p
                    p_ref[cur, c, rows, ch] = p.astype(BF16)
                part_sum[c].append(part)

        if has_next:
            qk_logits(ki + 1, other)
        for c in range(2):
            row_sum = jnp.sum(jnp.concatenate(part_sum[c], axis=0), axis=1, keepdims=True)
            l_ref[c] = alpha[c] * l_ref[c] + row_sum

    qk_logits(0, 0)

    n_far = jnp.maximum(qi - 1, 0)

    def far_pair(j, carry):
        block(2 * j, 0, None, True)
        block(2 * j + 1, 1, None, True)
        return carry
    lax.fori_loop(0, n_far // 2, far_pair, 0)

    @pl.when(qi % 2 == 1)
    def _():
        block(qi - 1, 0, 1, True)
        block(qi, 1, 0, False)
        accumulate(qi, 1)

    @pl.when(qi % 2 == 0)
    def _():
        @pl.when(qi >= 2)
        def _():
            block(qi - 2, 0, None, True)
            block(qi - 1, 1, 1, True)
        block(qi, 0, 0, False)
        accumulate(qi, 0)

    lam = (jnp.exp(jnp.sum(lq1_ref[...] * lk1_ref[...], keepdims=True))
           - jnp.exp(jnp.sum(lq2_ref[...] * lk2_ref[...], keepdims=True)) + lambda_init)
    inv0 = 1.0 / l_ref[0]
    inv1 = lam / l_ref[1]
    o = (acc_ref[0] * jnp.concatenate([inv0, inv0], axis=1)
         - acc_ref[1] * jnp.concatenate([inv1, inv1], axis=1))
    ms = jnp.mean(o * o, axis=-1, keepdims=True)
    o = o * lax.rsqrt(ms + 1e-5) * sg_ref[...] * (1.0 - lambda_init)
    o_ref[...] = o.astype(o_ref.dtype)


def _diff_attention(qkv, bias_tiles, lq1, lk1, lq2, lk2, subln_g, mlp_weights=None, cast_layer=None,
                    *, batch, seq, lambda_init):
    t = ATTN_T
    nq = seq // t
    grid = (batch, N_HEADS, nq)
    lam_spec = pl.BlockSpec((1, HEAD_DIM), lambda b, h, i: (0, 0))
    cast = cast_layer is not None
    cast_in, cast_out, cast_shapes = _mlp_cast_specs(grid, cast_layer) if cast else ([], [], [])
    return pl.pallas_call(
        functools.partial(_attn_kernel, t=t, lambda_init=lambda_init, cast=cast),
        grid=grid,
        in_specs=[
            lam_spec, lam_spec, lam_spec, lam_spec,
            pl.BlockSpec((1, HEAD_W), lambda b, h, i: (0, 0)),
            pl.BlockSpec((t, HEAD_W), lambda b, h, i: (b * nq + i, h)),
            pl.BlockSpec((seq, HEAD_W), lambda b, h, i: (b, N_HEADS + h)),
            pl.BlockSpec((seq, HEAD_W), lambda b, h, i: (b, 2 * N_HEADS + h)),
            pl.BlockSpec((1, 2, t, t), lambda b, h, i: (h, 0, 0, 0)),
        ] + cast_in,
        out_specs=[pl.BlockSpec((t, HEAD_W), lambda b, h, i: (b * nq + i, h))] + cast_out,
        out_shape=[jax.ShapeDtypeStruct((batch * seq, D_MODEL), BF16)] + cast_shapes,
        scratch_shapes=[pltpu.VMEM((2, t, LANES), F32), pltpu.VMEM((2, t, LANES), F32),
                        pltpu.VMEM((2, 2, t, LANES), F32),
                        pltpu.VMEM((2, t, HEAD_W), F32), pltpu.VMEM((2, 2, t, t), F32),
                        pltpu.VMEM((2, 2, t, t), BF16)],
        compiler_params=_params("parallel", "parallel", "arbitrary"),
        name="diff_attn",
    )(lq1, lk1, lq2, lk2, subln_g, qkv, qkv, qkv, bias_tiles, *(mlp_weights if cast else ()))


CONV_T = 256
CONV_HALO = 32
CONV_ROWS = 64
CONV_LANES = 256


def _conv_ln_kernel(halo_ref, u_ref, w_ref, b_ref, g_ref, beta_ref, o_ref, sh_ref, y_ref, *, tiles_per_seq):
    i = pl.program_id(0)
    first = (i % tiles_per_seq) == 0

    @pl.when(first)
    def _():
        sh_ref[0, 0:CONV_HALO, :] = jnp.zeros((CONV_HALO, D_MODEL), F32)

    @pl.when(jnp.logical_not(first))
    def _():
        sh_ref[0, 0:CONV_HALO, :] = halo_ref[...]

    sh_ref[0, CONV_HALO:, :] = u_ref[...]
    window = CONV_HALO + CONV_T
    for s in range(1, SUBLANES):
        for c0 in range(0, D_MODEL, CONV_LANES):
            lanes = slice(c0, c0 + CONV_LANES)
            sh_ref[s, 0:window - SUBLANES, lanes] = sh_ref[0, s:s + window - SUBLANES, lanes]

    base = CONV_HALO - (CONV_WIDTH - 1)
    for r0 in range(0, CONV_T, CONV_ROWS):
        for c0 in range(0, D_MODEL, CONV_LANES):
            lanes = slice(c0, c0 + CONV_LANES)
            acc = jnp.zeros((CONV_ROWS, CONV_LANES), F32)
            for j in range(CONV_WIDTH):
                a, s = divmod(base + j, SUBLANES)
                x0 = r0 + a * SUBLANES
                acc = acc + w_ref[j:j + 1, lanes] * sh_ref[s, x0:x0 + CONV_ROWS, lanes]
            y_ref[r0:r0 + CONV_ROWS, lanes] = acc + b_ref[:, lanes]

    def ln_body(c, carry):
        r0 = pl.multiple_of(c * CONV_ROWS, CONV_ROWS)
        y = y_ref[pl.ds(r0, CONV_ROWS), :]
        mu = jnp.mean(y, axis=-1, keepdims=True)
        yc = y - mu
        var = jnp.mean(yc * yc, axis=-1, keepdims=True)
        z = yc * lax.rsqrt(var + 1e-5) * g_ref[...] + beta_ref[...]
        o_ref[pl.ds(r0, CONV_ROWS), :] = (z * _sigmoid(z)).astype(o_ref.dtype)
        return carry
    lax.fori_loop(0, CONV_T // CONV_ROWS, ln_body, 0)


def _conv_ln(u, dw_w, dw_b, ln_g, ln_b, *, seq):
    m, d = u.shape
    tiles_per_seq = seq // CONV_T
    ratio = CONV_T // CONV_HALO
    row = lambda i: (0, 0)
    return pl.pallas_call(
        functools.partial(_conv_ln_kernel, tiles_per_seq=tiles_per_seq),
        grid=(m // CONV_T,),
        in_specs=[
            pl.BlockSpec((CONV_HALO, d), lambda i: (jnp.maximum(i * ratio - 1, 0), 0)),
            pl.BlockSpec((CONV_T, d), lambda i: (i, 0)),
            pl.BlockSpec((CONV_WIDTH, d), row),
            pl.BlockSpec((1, d), row), pl.BlockSpec((1, d), row), pl.BlockSpec((1, d), row),
        ],
        out_specs=pl.BlockSpec((CONV_T, d), lambda i: (i, 0)),
        out_shape=jax.ShapeDtypeStruct((m, d), BF16),
        scratch_shapes=[pltpu.VMEM((SUBLANES, CONV_HALO + CONV_T, d), F32), pltpu.VMEM((CONV_T, d), F32)],
        compiler_params=_params("parallel"),
        name="conv_ln",
    )(u, u, dw_w, dw_b, ln_g, ln_b)


LRU_T = 512
LRU_HALO = 8


def _lru_kernel(halo_ref, u_ref, y_ref, cw_ref, cb_ref, wa_ref, ba_ref, wi_ref, bi_ref, lam_ref,
                o_ref, buf_ref, carry_ref):
    ti = pl.program_id(2)
    first = ti == 0
    buf_ref[LRU_HALO:, :] = u_ref[...]

    @pl.when(first)
    def _():
        buf_ref[0:LRU_HALO, :] = jnp.zeros((LRU_HALO, LRU_BLOCK_SIZE), F32)
        carry_ref[...] = jnp.zeros(carry_ref.shape, F32)

    @pl.when(jnp.logical_not(first))
    def _():
        buf_ref[0:LRU_HALO, :] = halo_ref[...]

    base = LRU_HALO - (LRU_CONV_WIDTH - 1)
    uc = jnp.zeros((LRU_T, LRU_BLOCK_SIZE), F32)
    for j in range(LRU_CONV_WIDTH):
        uc = uc + cw_ref[j:j + 1, :] * buf_ref[base + j:base + j + LRU_T, :]
    uc = uc + cb_ref[...]

    ucb = uc.astype(BF16)
    r = _sigmoid(jnp.dot(ucb, wa_ref[0].astype(BF16), preferred_element_type=F32) + ba_ref[...])
    gate_i = _sigmoid(jnp.dot(ucb, wi_ref[0].astype(BF16), preferred_element_type=F32) + bi_ref[...])
    neg_lam = -lam_ref[...]
    softplus = jnp.maximum(neg_lam, 0.0) + jnp.log1p(jnp.exp(-jnp.abs(neg_lam)))
    log_a = -LRU_C * r * softplus
    a = jnp.exp(log_a)
    mult = jnp.sqrt(-jnp.tanh(log_a) * (1.0 + a * a))
    b = mult * (gate_i * uc)

    row = lax.broadcasted_iota(jnp.int32, (LRU_T, LRU_BLOCK_SIZE), 0)
    s = 1
    while s < SUBLANES:
        valid = row >= s
        a_sh = jnp.where(valid, pltpu.roll(a, s, 0), 1.0)
        b_sh = jnp.where(valid, pltpu.roll(b, s, 0), 0.0)
        b = a * b_sh + b
        a = a * a_sh
        s *= 2
    while s < LRU_T:
        b = jnp.concatenate([b[:s], a[s:] * b[:-s] + b[s:]], axis=0)
        a = jnp.concatenate([a[:s], a[s:] * a[:-s]], axis=0)
        s *= 2
    hs = a * carry_ref[0:1, :] + b
    carry_ref[0:1, :] = hs[LRU_T - 1:LRU_T, :]
    o_ref[...] = (hs * y_ref[...]).astype(o_ref.dtype)


def _lru_core(u, y, conv_w, conv_b, w_a, b_a, w_i, b_i, lam, *, batch, seq):
    m, d = u.shape
    nt = seq // LRU_T
    ratio = LRU_T // LRU_HALO
    c = LRU_BLOCK_SIZE
    chan = lambda b, g, t: (0, g)
    tile = lambda b, g, t: (b * nt + t, g)
    gate_w = pl.BlockSpec((1, c, c), lambda b, g, t: (g, 0, 0))
    return pl.pallas_call(
        _lru_kernel,
        grid=(batch, LRU_BLOCKS, nt),
        in_specs=[
            pl.BlockSpec((LRU_HALO, c), lambda b, g, t: (jnp.maximum((b * nt + t) * ratio - 1, 0), g)),
            pl.BlockSpec((LRU_T, c), tile),
            pl.BlockSpec((LRU_T, c), tile),
            pl.BlockSpec((LRU_CONV_WIDTH, c), chan),
            pl.BlockSpec((1, c), chan),
            gate_w, pl.BlockSpec((1, c), chan),
            gate_w, pl.BlockSpec((1, c), chan),
            pl.BlockSpec((1, c), chan),
        ],
        out_specs=pl.BlockSpec((LRU_T, c), tile),
        out_shape=jax.ShapeDtypeStruct((m, d), BF16),
        scratch_shapes=[pltpu.VMEM((LRU_HALO + LRU_T, c), F32), pltpu.VMEM((8, c), F32)],
        compiler_params=_params("parallel", "parallel", "arbitrary"),
        name="lru_core",
    )(u, u, y, conv_w, conv_b, w_a, b_a, w_i, b_i, lam)


def kernel(x, rel_bias, mixer_norm_g, attn_w_qkv, attn_lq1, attn_lk1, attn_lq2, attn_lk2, attn_subln_g, attn_w_o, conv_w_in, conv_b_in, conv_dw_w, conv_dw_b, conv_ln_g, conv_ln_b, conv_w_out, conv_b_out, lru_w_in, lru_conv_w, lru_conv_b, lru_w_a, lru_b_a, lru_w_i, lru_b_i, lru_lambda, lru_w_out, mlp_norm_g, mlp_w1, mlp_w2, final_norm_g):
    batch, seq, d = x.shape
    h = x.reshape(batch * seq, d)
    row = lambda v: v.reshape(1, -1)
    zero_bias = jnp.zeros((1, d), F32)
    q_scale = jnp.concatenate([jnp.full((1, d), HEAD_DIM ** -0.5 * LOG2E, F32), jnp.ones((1, 2 * d), F32)], axis=1)
    bias_tiles = _bias_tiles(rel_bias, ATTN_T)

    ia = ic = il = 0
    for layer in range(DEPTH):
        kind = layer % N_MIXERS
        g = row(mixer_norm_g[layer])
        if kind == 0:
            lambda_init = 0.8 - 0.6 * math.exp(-0.3 * layer)
            qkv = _norm_proj(h, g, attn_w_qkv, ia, q_scale)
            outs = _diff_attention(qkv, bias_tiles, row(attn_lq1[ia]), row(attn_lk1[ia]),
                                   row(attn_lq2[ia]), row(attn_lk2[ia]), row(attn_subln_g[ia]),
                                   (mlp_w1, mlp_w2), 0 if layer == 0 else None,
                                   batch=batch, seq=seq, lambda_init=lambda_init)
            if layer == 0:
                w1, w2 = outs[1:]
            h = _proj_res(outs[0], attn_w_o, ia, zero_bias, h)
            ia += 1
        elif kind == 1:
            u = _norm_glu(h, g, conv_w_in, ic, row(conv_b_in[ic]))
            act = _conv_ln(u, conv_dw_w[ic], row(conv_dw_b[ic]), row(conv_ln_g[ic]), row(conv_ln_b[ic]), seq=seq)
            h = _proj_res(act, conv_w_out, ic, row(conv_b_out[ic]), h)
            ic += 1
        else:
            u, y = _norm_lru_in(h, g, lru_w_in, il)
            hy = _lru_core(u, y, lru_conv_w[il], row(lru_conv_b[il]), lru_w_a[il], row(lru_b_a[il]),
                           lru_w_i[il], row(lru_b_i[il]), row(lru_lambda[il]), batch=batch, seq=seq)
            h = _proj_res(hy, lru_w_out, il, zero_bias, h)
            il += 1
        last = layer == DEPTH - 1
        outs = _mlp(h, row(mlp_norm_g[layer]), w1, w2, row(final_norm_g) if last else None,
                    (mlp_w1, mlp_w2), None if last else layer + 1)
        h = outs[0]
        if not last:
            w1, w2 = outs[1:]
    return h.reshape(batch, seq, d)
```

```python
import functools
import math

import numpy as np
import jax
import jax.numpy as jnp
from jax import lax
from jax.experimental import pallas as pl
from jax.experimental.pallas import tpu as pltpu

F32 = jnp.float32
BF16 = jnp.bfloat16

D_MODEL = 2048
DEPTH = 4
N_MIXERS = 3
HEAD_DIM = 128
HEAD_W = 2 * HEAD_DIM
N_HEADS = D_MODEL // HEAD_W
NUM_BUCKETS = 32
MAX_DISTANCE = 128
MASK_VALUE = -1e30
CONV_WIDTH = 31
LRU_BLOCKS = 8
LRU_BLOCK_SIZE = D_MODEL // LRU_BLOCKS
LRU_CONV_WIDTH = 4
LRU_C = 8.0
D_FF = 4 * D_MODEL

VMEM_LIMIT = 56 * 1024 * 1024
NORM_ROWS = 128
LANES = 128
SUBLANES = 8
BF16_ROWS = 16


def _params(*sem):
    return pltpu.CompilerParams(dimension_semantics=sem, vmem_limit_bytes=VMEM_LIMIT)


def _rms_rows_to(dst_ref, x_ref, g_ref, rows, eps):
    def body(c, carry):
        r0 = pl.multiple_of(c * NORM_ROWS, NORM_ROWS)
        x = x_ref[pl.ds(r0, NORM_ROWS), :]
        ms = jnp.mean(x * x, axis=-1, keepdims=True)
        dst_ref[pl.ds(r0, NORM_ROWS), :] = (x * lax.rsqrt(ms + eps) * g_ref[...]).astype(BF16)
        return carry
    lax.fori_loop(0, rows // NORM_ROWS, body, 0)


def _mlp_cast_specs(grid, layer):
    d, ff = D_MODEL, D_FF
    steps = math.prod(grid)

    def lin(*idx):
        n = 0
        for i, g in zip(idx, grid):
            n = n * g + i
        return n

    split = max(1, BF16_ROWS * steps // d)
    r1, c1, r2 = d * split // steps, ff // split, ff // steps
    in_specs = [pl.BlockSpec((None, r1, c1), lambda *idx: (layer, lin(*idx) // split, lin(*idx) % split)),
                pl.BlockSpec((None, r2, d), lambda *idx: (layer, lin(*idx), 0))]
    out_specs = [pl.BlockSpec((r1, c1), lambda *idx: (lin(*idx) // split, lin(*idx) % split)),
                 pl.BlockSpec((r2, d), lambda *idx: (lin(*idx), 0))]
    out_shapes = [jax.ShapeDtypeStruct((d, ff), BF16), jax.ShapeDtypeStruct((ff, d), BF16)]
    return in_specs, out_specs, out_shapes


def _cast_slabs(w1_ref, w2_ref, w1b_ref, w2b_ref):
    w1b_ref[...] = w1_ref[...].astype(BF16)
    w2b_ref[...] = w2_ref[...].astype(BF16)


def _sigmoid(x):
    return 1.0 / (1.0 + jnp.exp(-x))


def _gelu_tanh(x):
    c = math.sqrt(2.0 / math.pi)
    return 0.5 * x * (1.0 + jnp.tanh(c * (x + 0.044715 * (x * x * x))))


def _norm_proj_kernel(x_ref, g_ref, w_ref, s_ref, o_ref, xn_ref, *, tm):
    @pl.when(pl.program_id(1) == 0)
    def _():
        _rms_rows_to(xn_ref, x_ref, g_ref, tm, 1e-6)
    acc = jnp.dot(xn_ref[...], w_ref[...].astype(BF16), preferred_element_type=F32)
    o_ref[...] = (acc * s_ref[...]).astype(o_ref.dtype)


def _norm_proj(x, g, w, layer, col_scale, *, tm=1024, tn=1024):
    m, k = x.shape
    n = w.shape[2]
    return pl.pallas_call(
        functools.partial(_norm_proj_kernel, tm=tm),
        grid=(m // tm, n // tn),
        in_specs=[
            pl.BlockSpec((tm, k), lambda i, j: (i, 0)),
            pl.BlockSpec((1, k), lambda i, j: (0, 0)),
            pl.BlockSpec((None, k, tn), lambda i, j: (layer, 0, j)),
            pl.BlockSpec((1, tn), lambda i, j: (0, j)),
        ],
        out_specs=pl.BlockSpec((tm, tn), lambda i, j: (i, j)),
        out_shape=jax.ShapeDtypeStruct((m, n), BF16),
        scratch_shapes=[pltpu.VMEM((tm, k), BF16)],
        compiler_params=_params("parallel", "arbitrary"),
        name="norm_proj",
    )(x, g, w, col_scale)


def _norm_glu_kernel(x_ref, g_ref, wa_ref, wg_ref, ba_ref, bg_ref, o_ref, xn_ref, *, tm):
    @pl.when(pl.program_id(1) == 0)
    def _():
        _rms_rows_to(xn_ref, x_ref, g_ref, tm, 1e-6)
    xn = xn_ref[...]
    a = jnp.dot(xn, wa_ref[...].astype(BF16), preferred_element_type=F32) + ba_ref[...]
    gate = jnp.dot(xn, wg_ref[...].astype(BF16), preferred_element_type=F32) + bg_ref[...]
    o_ref[...] = a * _sigmoid(gate)


def _norm_glu(x, g, w, layer, b, *, tm=1024, tn=512):
    m, k = x.shape
    half = w.shape[2] // 2
    nj = half // tn
    return pl.pallas_call(
        functools.partial(_norm_glu_kernel, tm=tm),
        grid=(m // tm, nj),
        in_specs=[
            pl.BlockSpec((tm, k), lambda i, j: (i, 0)),
            pl.BlockSpec((1, k), lambda i, j: (0, 0)),
            pl.BlockSpec((None, k, tn), lambda i, j: (layer, 0, j)),
            pl.BlockSpec((None, k, tn), lambda i, j: (layer, 0, j + nj)),
            pl.BlockSpec((1, tn), lambda i, j: (0, j)),
            pl.BlockSpec((1, tn), lambda i, j: (0, j + nj)),
        ],
        out_specs=pl.BlockSpec((tm, tn), lambda i, j: (i, j)),
        out_shape=jax.ShapeDtypeStruct((m, half), F32),
        scratch_shapes=[pltpu.VMEM((tm, k), BF16)],
        compiler_params=_params("parallel", "arbitrary"),
        name="norm_glu",
    )(x, g, w, w, b, b)


def _norm_lru_in_kernel(x_ref, g_ref, wu_ref, wy_ref, u_ref, y_ref, xn_ref, *, tm):
    @pl.when(pl.program_id(1) == 0)
    def _():
        _rms_rows_to(xn_ref, x_ref, g_ref, tm, 1e-6)
    xn = xn_ref[...]
    u_ref[...] = jnp.dot(xn, wu_ref[...].astype(BF16), preferred_element_type=F32)
    y_ref[...] = _gelu_tanh(jnp.dot(xn, wy_ref[...].astype(BF16), preferred_element_type=F32))


def _norm_lru_in(x, g, w, layer, *, tm=1024, tn=512):
    m, k = x.shape
    half = w.shape[2] // 2
    nj = half // tn
    return pl.pallas_call(
        functools.partial(_norm_lru_in_kernel, tm=tm),
        grid=(m // tm, nj),
        in_specs=[
            pl.BlockSpec((tm, k), lambda i, j: (i, 0)),
            pl.BlockSpec((1, k), lambda i, j: (0, 0)),
            pl.BlockSpec((None, k, tn), lambda i, j: (layer, 0, j)),
            pl.BlockSpec((None, k, tn), lambda i, j: (layer, 0, j + nj)),
        ],
        out_specs=[pl.BlockSpec((tm, tn), lambda i, j: (i, j)),
                   pl.BlockSpec((tm, tn), lambda i, j: (i, j))],
        out_shape=[jax.ShapeDtypeStruct((m, half), F32), jax.ShapeDtypeStruct((m, half), F32)],
        scratch_shapes=[pltpu.VMEM((tm, k), BF16)],
        compiler_params=_params("parallel", "arbitrary"),
        name="norm_lru_in",
    )(x, g, w, w)


def _proj_res_kernel(a_ref, w_ref, b_ref, r_ref, o_ref, wb_ref):
    @pl.when(pl.program_id(1) == 0)
    def _():
        wb_ref[...] = w_ref[...].astype(BF16)
    acc = jnp.dot(a_ref[...], wb_ref[...], preferred_element_type=F32)
    o_ref[...] = r_ref[...] + (acc + b_ref[...])


def _proj_res(a, w, layer, b, res, *, tm=512, tn=2048):
    m, k = a.shape
    n = w.shape[2]
    return pl.pallas_call(
        _proj_res_kernel,
        grid=(n // tn, m // tm),
        in_specs=[
            pl.BlockSpec((tm, k), lambda j, i: (i, 0)),
            pl.BlockSpec((None, k, tn), lambda j, i: (layer, 0, j), pipeline_mode=pl.Buffered(1)),
            pl.BlockSpec((1, tn), lambda j, i: (0, j)),
            pl.BlockSpec((tm, tn), lambda j, i: (i, j)),
        ],
        out_specs=pl.BlockSpec((tm, tn), lambda j, i: (i, j)),
        out_shape=jax.ShapeDtypeStruct((m, n), F32),
        scratch_shapes=[pltpu.VMEM((k, tn), BF16)],
        compiler_params=_params("parallel", "arbitrary"),
        name="proj_res",
    )(a, w, b, res)


def _mlp_kernel(*refs, tm, final_norm, cast):
    refs = list(refs)
    x_ref, g_ref, w1_ref, w2_ref = refs[:4]
    del refs[:4]
    if final_norm:
        fg_ref = refs.pop(0)
    if cast:
        _cast_slabs(refs[0], refs[1], refs[3], refs[4])
        o_ref = refs[2]
    else:
        o_ref = refs[0]
    xn_ref = refs[-1]
    f = pl.program_id(1)

    @pl.when(f == 0)
    def _():
        _rms_rows_to(xn_ref, x_ref, g_ref, tm, 1e-6)
        o_ref[...] = x_ref[...]

    z = jnp.maximum(jnp.dot(xn_ref[...], w1_ref[...], preferred_element_type=F32), 0.0)
    o_ref[...] += jnp.dot((z * z).astype(BF16), w2_ref[...], preferred_element_type=F32)

    if final_norm:
        @pl.when(f == pl.num_programs(1) - 1)
        def _():
            def body(c, carry):
                rows = pl.ds(pl.multiple_of(c * NORM_ROWS, NORM_ROWS), NORM_ROWS)
                y = o_ref[rows, :]
                ms = jnp.mean(y * y, axis=-1, keepdims=True)
                o_ref[rows, :] = y * lax.rsqrt(ms + 1e-6) * fg_ref[...]
                return carry
            lax.fori_loop(0, tm // NORM_ROWS, body, 0)


def _mlp(x, g, w1, w2, final_g=None, mlp_weights=None, cast_layer=None, *, tm=512, tf=1024):
    m, d = x.shape
    ff = w1.shape[1]
    final_norm = final_g is not None
    cast = cast_layer is not None
    grid = (m // tm, ff // tf)
    row_spec = pl.BlockSpec((1, d), lambda i, f: (0, 0))
    cast_in, cast_out, cast_shapes = _mlp_cast_specs(grid, cast_layer) if cast else ([], [], [])
    return pl.pallas_call(
        functools.partial(_mlp_kernel, tm=tm, final_norm=final_norm, cast=cast),
        grid=grid,
        in_specs=[
            pl.BlockSpec((tm, d), lambda i, f: (i, 0)),
            row_spec,
            pl.BlockSpec((d, tf), lambda i, f: (0, f)),
            pl.BlockSpec((tf, d), lambda i, f: (f, 0)),
        ] + ([row_spec] if final_norm else []) + cast_in,
        out_specs=[pl.BlockSpec((tm, d), lambda i, f: (i, 0))] + cast_out,
        out_shape=[jax.ShapeDtypeStruct((m, d), F32)] + cast_shapes,
        scratch_shapes=[pltpu.VMEM((tm, d), BF16)],
        compiler_params=_params("parallel", "arbitrary"),
        name="mlp",
    )(x, g, w1, w2, *((final_g,) if final_norm else ()), *(mlp_weights if cast else ()))


ATTN_T = 512
ATTN_STRIP = 32
ATTN_Q_PER_STEP = 4
LOG2E = math.log2(math.e)


def _bucket_tiles(t):
    r = np.arange(t, dtype=np.int32)[:, None]
    c = np.arange(t, dtype=np.int32)[None, :]
    tiles = []
    for d in range(2):
        n = np.maximum(d * t + r - c, 0)
        max_exact = NUM_BUCKETS // 2
        nf = np.maximum(n, 1).astype(np.float32)
        large = max_exact + (np.log(nf / np.float32(max_exact)) / np.float32(math.log(MAX_DISTANCE / max_exact))
                             * np.float32(NUM_BUCKETS - max_exact)).astype(np.int32)
        large = np.minimum(large, NUM_BUCKETS - 1)
        bucket = np.where(n < max_exact, n, large).astype(np.int32)
        tiles.append(np.where(d * t + r - c >= 0, bucket, -1))
    return np.stack(tiles)


def _bias_tiles_kernel(rb_ref, bucket_ref, o_ref):
    h = pl.program_id(0)
    bucket = bucket_ref[0]
    acc = jnp.zeros(bucket.shape, F32)
    for j in range(NUM_BUCKETS):
        acc = jnp.where(bucket == j, rb_ref[j, h], acc)
    acc = (acc - rb_ref[NUM_BUCKETS - 1, h]) * LOG2E
    o_ref[0, 0] = jnp.where(bucket < 0, MASK_VALUE, acc)


def _bias_tiles(rel_bias, t):
    buckets = jnp.asarray(_bucket_tiles(t))
    return pl.pallas_call(
        _bias_tiles_kernel,
        grid=(N_HEADS, 2),
        in_specs=[
            pl.BlockSpec(memory_space=pltpu.SMEM),
            pl.BlockSpec((1, t, t), lambda h, d: (d, 0, 0)),
        ],
        out_specs=pl.BlockSpec((1, 1, t, t), lambda h, d: (h, d, 0, 0)),
        out_shape=jax.ShapeDtypeStruct((N_HEADS, 2, t, t), F32),
        compiler_params=_params("parallel", "parallel"),
        name="bias_tiles",
    )(rel_bias, buckets)


def _attn_kernel(*refs, t, lambda_init, cast, q_per_step):
    lq1_ref, lk1_ref, lq2_ref, lk2_ref, sg_ref, q_ref, k_ref, v_ref, bias_ref = refs[:9]
    if cast:
        w1_ref, w2_ref, o_ref, w1b_ref, w2b_ref = refs[9:14]
        _cast_slabs(w1_ref, w2_ref, w1b_ref, w2b_ref)
    else:
        o_ref = refs[9]
    m_ref, l_ref, alpha_ref, acc_ref, s_ref, p_ref = refs[-6:]
    chunks = [slice(j * LANES, (j + 1) * LANES) for j in range(t // LANES)]

    def start(ki):
        return pl.multiple_of(jnp.maximum(ki, 0) * t, t)

    def qk_logits(jq, ki, slot):
        for c in range(2):
            lanes = slice(c * HEAD_DIM, (c + 1) * HEAD_DIM)
            s_ref[slot, c] = lax.dot_general(q_ref[pl.ds(start(jq), t), lanes], k_ref[pl.ds(start(ki), t), lanes],
                                             (((1,), (1,)), ((), ())), preferred_element_type=F32)

    def accumulate(ki, slot):
        v = v_ref[pl.ds(start(ki), t), :]
        for c in range(2):
            alpha = alpha_ref[slot, c]
            pv = jnp.dot(p_ref[slot, c], v, preferred_element_type=F32)
            acc_ref[c] = jnp.concatenate([alpha, alpha], axis=1) * acc_ref[c] + pv

    strips = [slice(r * ATTN_STRIP, (r + 1) * ATTN_STRIP) for r in range(t // ATTN_STRIP)]

    def block(jq, ki, cur, bias_index, has_next):
        other = 1 - cur

        def logits(c, rows):
            s = s_ref[cur, c, rows, :]
            if bias_index is not None:
                s = s + bias_ref[0, bias_index, rows, :]
            return s

        part_max = [[], []]
        for rows in strips:
            for c in range(2):
                s = logits(c, rows)
                part = s[:, chunks[0]]
                for ch in chunks[1:]:
                    part = jnp.maximum(part, s[:, ch])
                part_max[c].append(part)

        accumulate(ki - 1, other)
        m_new, alpha = [], []
        for c in range(2):
            m_prev = m_ref[c]
            m_c = jnp.maximum(m_prev, jnp.max(jnp.concatenate(part_max[c], axis=0), axis=1, keepdims=True))
            alpha.append(jnp.exp2(m_prev - m_c))
            alpha_ref[cur, c] = alpha[c]
            m_ref[c] = m_c
            m_new.append(m_c)

        part_sum = [[], []]
        for rows in strips:
            for c in range(2):
                s = logits(c, rows)
                m_rows = m_new[c][rows, :]
                part = None
                for ch in chunks:
                    p = jnp.exp2(s[:, ch] - m_rows)
                    part = p if part is None else part + p
                    p_ref[cur, c, rows, ch] = p.astype(BF16)
                part_sum[c].append(part)

        if has_next:
            qk_logits(jq, ki + 1, other)
        for c in range(2):
            row_sum = jnp.sum(jnp.concatenate(part_sum[c], axis=0), axis=1, keepdims=True)
            l_ref[c] = alpha[c] * l_ref[c] + row_sum

    lam = (jnp.exp(jnp.sum(lq1_ref[...] * lk1_ref[...], keepdims=True))
           - jnp.exp(jnp.sum(lq2_ref[...] * lk2_ref[...], keepdims=True)) + lambda_init)

    def q_block(jq, carry):
        qi = pl.program_id(2) * q_per_step + jq

        m_ref[...] = jnp.full(m_ref.shape, MASK_VALUE, F32)
        l_ref[...] = jnp.zeros(l_ref.shape, F32)
        acc_ref[...] = jnp.zeros(acc_ref.shape, F32)
        alpha_ref[1] = jnp.zeros(alpha_ref.shape[1:], F32)
        p_ref[1] = jnp.zeros(p_ref.shape[1:], BF16)

        n_far = jnp.maximum(qi - 1, 0)

        def far_pair(j, carry):
            block(jq, 2 * j, 0, None, True)
            block(jq, 2 * j + 1, 1, None, True)
            return carry
        lax.fori_loop(0, n_far // 2, far_pair, 0)

        @pl.when(qi % 2 == 1)
        def _():
            block(jq, qi - 1, 0, 1, True)
            block(jq, qi, 1, 0, False)
            accumulate(qi, 1)

        @pl.when(qi % 2 == 0)
        def _():
            @pl.when(qi >= 2)
            def _():
                block(jq, qi - 2, 0, None, True)
                block(jq, qi - 1, 1, 1, True)
            block(jq, qi, 0, 0, False)
            accumulate(qi, 0)

        qk_logits(jnp.minimum(jq + 1, q_per_step - 1), 0, 0)
        inv0 = 1.0 / l_ref[0]
        inv1 = lam / l_ref[1]
        o = (acc_ref[0] * jnp.concatenate([inv0, inv0], axis=1)
             - acc_ref[1] * jnp.concatenate([inv1, inv1], axis=1))
        ms = jnp.mean(o * o, axis=-1, keepdims=True)
        o = o * lax.rsqrt(ms + 1e-5) * sg_ref[...] * (1.0 - lambda_init)
        o_ref[pl.ds(start(jq), t), :] = o.astype(o_ref.dtype)
        return carry

    qk_logits(0, 0, 0)
    lax.fori_loop(0, q_per_step, q_block, 0)


def _diff_attention(qkv, bias_tiles, lq1, lk1, lq2, lk2, subln_g, mlp_weights=None, cast_layer=None,
                    *, batch, seq, lambda_init):
    t = ATTN_T
    qps = ATTN_Q_PER_STEP
    groups = seq // (t * qps)
    grid = (batch, N_HEADS, groups)
    lam_spec = pl.BlockSpec((1, HEAD_DIM), lambda b, h, i: (0, 0))
    cast = cast_layer is not None
    cast_in, cast_out, cast_shapes = _mlp_cast_specs(grid, cast_layer) if cast else ([], [], [])
    return pl.pallas_call(
        functools.partial(_attn_kernel, t=t, lambda_init=lambda_init, cast=cast, q_per_step=qps),
        grid=grid,
        in_specs=[
            lam_spec, lam_spec, lam_spec, lam_spec,
            pl.BlockSpec((1, HEAD_W), lambda b, h, i: (0, 0)),
            pl.BlockSpec((qps * t, HEAD_W), lambda b, h, i: (b * groups + i, h)),
            pl.BlockSpec((seq, HEAD_W), lambda b, h, i: (b, N_HEADS + h)),
            pl.BlockSpec((seq, HEAD_W), lambda b, h, i: (b, 2 * N_HEADS + h)),
            pl.BlockSpec((1, 2, t, t), lambda b, h, i: (h, 0, 0, 0)),
        ] + cast_in,
        out_specs=[pl.BlockSpec((qps * t, HEAD_W), lambda b, h, i: (b * groups + i, h))] + cast_out,
        out_shape=[jax.ShapeDtypeStruct((batch * seq, D_MODEL), BF16)] + cast_shapes,
        scratch_shapes=[pltpu.VMEM((2, t, LANES), F32), pltpu.VMEM((2, t, LANES), F32),
                        pltpu.VMEM((2, 2, t, LANES), F32),
                        pltpu.VMEM((2, t, HEAD_W), F32), pltpu.VMEM((2, 2, t, t), F32),
                        pltpu.VMEM((2, 2, t, t), BF16)],
        compiler_params=_params("parallel", "parallel", "arbitrary"),
        name="diff_attn",
    )(lq1, lk1, lq2, lk2, subln_g, qkv, qkv, qkv, bias_tiles, *(mlp_weights if cast else ()))


CONV_T = 256
CONV_HALO = 32
CONV_ROWS = 64
CONV_LANES = 256


def _conv_ln_kernel(halo_ref, u_ref, w_ref, b_ref, g_ref, beta_ref, o_ref, sh_ref, y_ref, *, tiles_per_seq):
    i = pl.program_id(0)
    first = (i % tiles_per_seq) == 0

    @pl.when(first)
    def _():
        sh_ref[0, 0:CONV_HALO, :] = jnp.zeros((CONV_HALO, D_MODEL), F32)

    @pl.when(jnp.logical_not(first))
    def _():
        sh_ref[0, 0:CONV_HALO, :] = halo_ref[...]

    sh_ref[0, CONV_HALO:, :] = u_ref[...]
    window = CONV_HALO + CONV_T
    for s in range(1, SUBLANES):
        for c0 in range(0, D_MODEL, CONV_LANES):
            lanes = slice(c0, c0 + CONV_LANES)
            sh_ref[s, 0:window - SUBLANES, lanes] = sh_ref[0, s:s + window - SUBLANES, lanes]

    base = CONV_HALO - (CONV_WIDTH - 1)
    for r0 in range(0, CONV_T, CONV_ROWS):
        for c0 in range(0, D_MODEL, CONV_LANES):
            lanes = slice(c0, c0 + CONV_LANES)
            acc = jnp.zeros((CONV_ROWS, CONV_LANES), F32)
            for j in range(CONV_WIDTH):
                a, s = divmod(base + j, SUBLANES)
                x0 = r0 + a * SUBLANES
                acc = acc + w_ref[j:j + 1, lanes] * sh_ref[s, x0:x0 + CONV_ROWS, lanes]
            y_ref[r0:r0 + CONV_ROWS, lanes] = acc + b_ref[:, lanes]

    def ln_body(c, carry):
        r0 = pl.multiple_of(c * CONV_ROWS, CONV_ROWS)
        y = y_ref[pl.ds(r0, CONV_ROWS), :]
        mu = jnp.mean(y, axis=-1, keepdims=True)
        yc = y - mu
        var = jnp.mean(yc * yc, axis=-1, keepdims=True)
        z = yc * lax.rsqrt(var + 1e-5) * g_ref[...] + beta_ref[...]
        o_ref[pl.ds(r0, CONV_ROWS), :] = (z * _sigmoid(z)).astype(o_ref.dtype)
        return carry
    lax.fori_loop(0, CONV_T // CONV_ROWS, ln_body, 0)


def _conv_ln(u, dw_w, dw_b, ln_g, ln_b, *, seq):
    m, d = u.shape
    tiles_per_seq = seq // CONV_T
    ratio = CONV_T // CONV_HALO
    row = lambda i: (0, 0)
    return pl.pallas_call(
        functools.partial(_conv_ln_kernel, tiles_per_seq=tiles_per_seq),
        grid=(m // CONV_T,),
        in_specs=[
            pl.BlockSpec((CONV_HALO, d), lambda i: (jnp.maximum(i * ratio - 1, 0), 0)),
            pl.BlockSpec((CONV_T, d), lambda i: (i, 0)),
            pl.BlockSpec((CONV_WIDTH, d), row),
            pl.BlockSpec((1, d), row), pl.BlockSpec((1, d), row), pl.BlockSpec((1, d), row),
        ],
        out_specs=pl.BlockSpec((CONV_T, d), lambda i: (i, 0)),
        out_shape=jax.ShapeDtypeStruct((m, d), BF16),
        scratch_shapes=[pltpu.VMEM((SUBLANES, CONV_HALO + CONV_T, d), F32), pltpu.VMEM((CONV_T, d), F32)],
        compiler_params=_params("parallel"),
        name="conv_ln",
    )(u, u, dw_w, dw_b, ln_g, ln_b)


LRU_T = 512
LRU_HALO = 8


def _lru_kernel(halo_ref, u_ref, y_ref, cw_ref, cb_ref, wa_ref, ba_ref, wi_ref, bi_ref, lam_ref,
                o_ref, buf_ref, carry_ref):
    ti = pl.program_id(2)
    first = ti == 0
    buf_ref[LRU_HALO:, :] = u_ref[...]

    @pl.when(first)
    def _():
        buf_ref[0:LRU_HALO, :] = jnp.zeros((LRU_HALO, LRU_BLOCK_SIZE), F32)
        carry_ref[...] = jnp.zeros(carry_ref.shape, F32)

    @pl.when(jnp.logical_not(first))
    def _():
        buf_ref[0:LRU_HALO, :] = halo_ref[...]

    base = LRU_HALO - (LRU_CONV_WIDTH - 1)
    uc = jnp.zeros((LRU_T, LRU_BLOCK_SIZE), F32)
    for j in range(LRU_CONV_WIDTH):
        uc = uc + cw_ref[j:j + 1, :] * buf_ref[base + j:base + j + LRU_T, :]
    uc = uc + cb_ref[...]

    ucb = uc.astype(BF16)
    r = _sigmoid(jnp.dot(ucb, wa_ref[0].astype(BF16), preferred_element_type=F32) + ba_ref[...])
    gate_i = _sigmoid(jnp.dot(ucb, wi_ref[0].astype(BF16), preferred_element_type=F32) + bi_ref[...])
    neg_lam = -lam_ref[...]
    softplus = jnp.maximum(neg_lam, 0.0) + jnp.log1p(jnp.exp(-jnp.abs(neg_lam)))
    log_a = -LRU_C * r * softplus
    a = jnp.exp(log_a)
    mult = jnp.sqrt(-jnp.tanh(log_a) * (1.0 + a * a))
    b = mult * (gate_i * uc)

    row = lax.broadcasted_iota(jnp.int32, (LRU_T, LRU_BLOCK_SIZE), 0)
    s = 1
    while s < SUBLANES:
        valid = row >= s
        a_sh = jnp.where(valid, pltpu.roll(a, s, 0), 1.0)
        b_sh = jnp.where(valid, pltpu.roll(b, s, 0), 0.0)
        b = a * b_sh + b
        a = a * a_sh
        s *= 2
    while s < LRU_T:
        b = jnp.concatenate([b[:s], a[s:] * b[:-s] + b[s:]], axis=0)
        a = jnp.concatenate([a[:s], a[s:] * a[:-s]], axis=0)
        s *= 2
    hs = a * carry_ref[0:1, :] + b
    carry_ref[0:1, :] = hs[LRU_T - 1:LRU_T, :]
    o_ref[...] = (hs * y_ref[...]).astype(o_ref.dtype)


def _lru_core(u, y, conv_w, conv_b, w_a, b_a, w_i, b_i, lam, *, batch, seq):
    m, d = u.shape
    nt = seq // LRU_T
    ratio = LRU_T // LRU_HALO
    c = LRU_BLOCK_SIZE
    chan = lambda b, g, t: (0, g)
    tile = lambda b, g, t: (b * nt + t, g)
    gate_w = pl.BlockSpec((1, c, c), lambda b, g, t: (g, 0, 0))
    return pl.pallas_call(
        _lru_kernel,
        grid=(batch, LRU_BLOCKS, nt),
        in_specs=[
            pl.BlockSpec((LRU_HALO, c), lambda b, g, t: (jnp.maximum((b * nt + t) * ratio - 1, 0), g)),
            pl.BlockSpec((LRU_T, c), tile),
            pl.BlockSpec((LRU_T, c), tile),
            pl.BlockSpec((LRU_CONV_WIDTH, c), chan),
            pl.BlockSpec((1, c), chan),
            gate_w, pl.BlockSpec((1, c), chan),
            gate_w, pl.BlockSpec((1, c), chan),
            pl.BlockSpec((1, c), chan),
        ],
        out_specs=pl.BlockSpec((LRU_T, c), tile),
        out_shape=jax.ShapeDtypeStruct((m, d), BF16),
        scratch_shapes=[pltpu.VMEM((LRU_HALO + LRU_T, c), F32), pltpu.VMEM((8, c), F32)],
        compiler_params=_params("parallel", "parallel", "arbitrary"),
        name="lru_core",
    )(u, u, y, conv_w, conv_b, w_a, b_a, w_i, b_i, lam)


def kernel(x, rel_bias, mixer_norm_g, attn_w_qkv, attn_lq1, attn_lk1, attn_lq2, attn_lk2, attn_subln_g, attn_w_o, conv_w_in, conv_b_in, conv_dw_w, conv_dw_b, conv_ln_g, conv_ln_b, conv_w_out, conv_b_out, lru_w_in, lru_conv_w, lru_conv_b, lru_w_a, lru_b_a, lru_w_i, lru_b_i, lru_lambda, lru_w_out, mlp_norm_g, mlp_w1, mlp_w2, final_norm_g):
    batch, seq, d = x.shape
    h = x.reshape(batch * seq, d)
    row = lambda v: v.reshape(1, -1)
    zero_bias = jnp.zeros((1, d), F32)
    q_scale = jnp.concatenate([jnp.full((1, d), HEAD_DIM ** -0.5 * LOG2E, F32), jnp.ones((1, 2 * d), F32)], axis=1)
    bias_tiles = _bias_tiles(rel_bias, ATTN_T)

    ia = ic = il = 0
    for layer in range(DEPTH):
        kind = layer % N_MIXERS
        g = row(mixer_norm_g[layer])
        if kind == 0:
            lambda_init = 0.8 - 0.6 * math.exp(-0.3 * layer)
            qkv = _norm_proj(h, g, attn_w_qkv, ia, q_scale)
            outs = _diff_attention(qkv, bias_tiles, row(attn_lq1[ia]), row(attn_lk1[ia]),
                                   row(attn_lq2[ia]), row(attn_lk2[ia]), row(attn_subln_g[ia]),
                                   (mlp_w1, mlp_w2), 0 if layer == 0 else None,
                                   batch=batch, seq=seq, lambda_init=lambda_init)
            if layer == 0:
                w1, w2 = outs[1:]
            h = _proj_res(outs[0], attn_w_o, ia, zero_bias, h)
            ia += 1
        elif kind == 1:
            u = _norm_glu(h, g, conv_w_in, ic, row(conv_b_in[ic]))
            act = _conv_ln(u, conv_dw_w[ic], row(conv_dw_b[ic]), row(conv_ln_g[ic]), row(conv_ln_b[ic]), seq=seq)
            h = _proj_res(act, conv_w_out, ic, row(conv_b_out[ic]), h)
            ic += 1
        else:
            u, y = _norm_lru_in(h, g, lru_w_in, il)
            hy = _lru_core(u, y, lru_conv_w[il], row(lru_conv_b[il]), lru_w_a[il], row(lru_b_a[il]),
                           lru_w_i[il], row(lru_b_i[il]), row(lru_lambda[il]), batch=batch, seq=seq)
            h = _proj_res(hy, lru_w_out, il, zero_bias, h)
            il += 1
        last = layer == DEPTH - 1
        outs = _mlp(h, row(mlp_norm_g[layer]), w1, w2, row(final_norm_g) if last else None,
                    (mlp_w1, mlp_w2), None if last else layer + 1)
        h = outs[0]
        if not last:
            w1, w2 = outs[1:]
    return h.reshape(batch, seq, d)
```

```python
import functools
import math

import numpy as np
import jax
import jax.numpy as jnp
from jax import lax
from jax.experimental import pallas as pl
from jax.experimental.pallas import tpu as pltpu

F32 = jnp.float32
BF16 = jnp.bfloat16

D_MODEL = 2048
DEPTH = 4
N_MIXERS = 3
HEAD_DIM = 128
HEAD_W = 2 * HEAD_DIM
N_HEADS = D_MODEL // HEAD_W
NUM_BUCKETS = 32
MAX_DISTANCE = 128
MASK_VALUE = -1e30
CONV_WIDTH = 31
LRU_BLOCKS = 8
LRU_BLOCK_SIZE = D_MODEL // LRU_BLOCKS
LRU_CONV_WIDTH = 4
LRU_C = 8.0
D_FF = 4 * D_MODEL

VMEM_LIMIT = 56 * 1024 * 1024
NORM_ROWS = 128
LANES = 128
SUBLANES = 8
BF16_ROWS = 16


def _params(*sem):
    return pltpu.CompilerParams(dimension_semantics=sem, vmem_limit_bytes=VMEM_LIMIT)


def _rms_rows_to(dst_ref, x_ref, g_ref, rows, eps):
    def body(c, carry):
        r0 = pl.multiple_of(c * NORM_ROWS, NORM_ROWS)
        x = x_ref[pl.ds(r0, NORM_ROWS), :]
        ms = jnp.mean(x * x, axis=-1, keepdims=True)
        dst_ref[pl.ds(r0, NORM_ROWS), :] = (x * lax.rsqrt(ms + eps) * g_ref[...]).astype(BF16)
        return carry
    lax.fori_loop(0, rows // NORM_ROWS, body, 0)


def _mlp_cast_specs(grid, layer):
    d, ff = D_MODEL, D_FF
    steps = math.prod(grid)

    def lin(*idx):
        n = 0
        for i, g in zip(idx, grid):
            n = n * g + i
        return n

    split = max(1, BF16_ROWS * steps // d)
    r1, c1, r2 = d * split // steps, ff // split, ff // steps
    in_specs = [pl.BlockSpec((None, r1, c1), lambda *idx: (layer, lin(*idx) // split, lin(*idx) % split)),
                pl.BlockSpec((None, r2, d), lambda *idx: (layer, lin(*idx), 0))]
    out_specs = [pl.BlockSpec((r1, c1), lambda *idx: (lin(*idx) // split, lin(*idx) % split)),
                 pl.BlockSpec((r2, d), lambda *idx: (lin(*idx), 0))]
    out_shapes = [jax.ShapeDtypeStruct((d, ff), BF16), jax.ShapeDtypeStruct((ff, d), BF16)]
    return in_specs, out_specs, out_shapes


def _cast_slabs(w1_ref, w2_ref, w1b_ref, w2b_ref):
    w1b_ref[...] = w1_ref[...].astype(BF16)
    w2b_ref[...] = w2_ref[...].astype(BF16)


def _sigmoid(x):
    return 0.5 * (jnp.tanh(0.5 * x) + 1.0)


def _gelu_tanh(x):
    c = math.sqrt(2.0 / math.pi)
    return 0.5 * x * (1.0 + jnp.tanh(c * (x + 0.044715 * (x * x * x))))


def _norm_proj_kernel(x_ref, g_ref, w_ref, s_ref, o_ref, xn_ref, *, tm):
    @pl.when(pl.program_id(1) == 0)
    def _():
        _rms_rows_to(xn_ref, x_ref, g_ref, tm, 1e-6)
    acc = jnp.dot(xn_ref[...], w_ref[...].astype(BF16), preferred_element_type=F32)
    o_ref[...] = (acc * s_ref[...]).astype(o_ref.dtype)


def _norm_proj(x, g, w, layer, col_scale, *, tm=1024, tn=1024):
    m, k = x.shape
    n = w.shape[2]
    return pl.pallas_call(
        functools.partial(_norm_proj_kernel, tm=tm),
        grid=(m // tm, n // tn),
        in_specs=[
            pl.BlockSpec((tm, k), lambda i, j: (i, 0)),
            pl.BlockSpec((1, k), lambda i, j: (0, 0)),
            pl.BlockSpec((None, k, tn), lambda i, j: (layer, 0, j)),
            pl.BlockSpec((1, tn), lambda i, j: (0, j)),
        ],
        out_specs=pl.BlockSpec((tm, tn), lambda i, j: (i, j)),
        out_shape=jax.ShapeDtypeStruct((m, n), BF16),
        scratch_shapes=[pltpu.VMEM((tm, k), BF16)],
        compiler_params=_params("parallel", "arbitrary"),
        name="norm_proj",
    )(x, g, w, col_scale)


def _norm_glu_kernel(x_ref, g_ref, wa_ref, wg_ref, ba_ref, bg_ref, o_ref, xn_ref, *, tm):
    @pl.when(pl.program_id(1) == 0)
    def _():
        _rms_rows_to(xn_ref, x_ref, g_ref, tm, 1e-6)
    xn = xn_ref[...]
    a = jnp.dot(xn, wa_ref[...].astype(BF16), preferred_element_type=F32) + ba_ref[...]
    gate = jnp.dot(xn, wg_ref[...].astype(BF16), preferred_element_type=F32) + bg_ref[...]
    o_ref[...] = a * _sigmoid(gate)


def _norm_glu(x, g, w, layer, b, *, tm=1024, tn=512):
    m, k = x.shape
    half = w.shape[2] // 2
    nj = half // tn
    return pl.pallas_call(
        functools.partial(_norm_glu_kernel, tm=tm),
        grid=(m // tm, nj),
        in_specs=[
            pl.BlockSpec((tm, k), lambda i, j: (i, 0)),
            pl.BlockSpec((1, k), lambda i, j: (0, 0)),
            pl.BlockSpec((None, k, tn), lambda i, j: (layer, 0, j)),
            pl.BlockSpec((None, k, tn), lambda i, j: (layer, 0, j + nj)),
            pl.BlockSpec((1, tn), lambda i, j: (0, j)),
            pl.BlockSpec((1, tn), lambda i, j: (0, j + nj)),
        ],
        out_specs=pl.BlockSpec((tm, tn), lambda i, j: (i, j)),
        out_shape=jax.ShapeDtypeStruct((m, half), F32),
        scratch_shapes=[pltpu.VMEM((tm, k), BF16)],
        compiler_params=_params("parallel", "arbitrary"),
        name="norm_glu",
    )(x, g, w, w, b, b)


def _norm_lru_in_kernel(x_ref, g_ref, wu_ref, wy_ref, u_ref, y_ref, xn_ref, *, tm):
    @pl.when(pl.program_id(1) == 0)
    def _():
        _rms_rows_to(xn_ref, x_ref, g_ref, tm, 1e-6)
    xn = xn_ref[...]
    u_ref[...] = jnp.dot(xn, wu_ref[...].astype(BF16), preferred_element_type=F32)
    y_ref[...] = _gelu_tanh(jnp.dot(xn, wy_ref[...].astype(BF16), preferred_element_type=F32))


def _norm_lru_in(x, g, w, layer, *, tm=1024, tn=512):
    m, k = x.shape
    half = w.shape[2] // 2
    nj = half // tn
    return pl.pallas_call(
        functools.partial(_norm_lru_in_kernel, tm=tm),
        grid=(m // tm, nj),
        in_specs=[
            pl.BlockSpec((tm, k), lambda i, j: (i, 0)),
            pl.BlockSpec((1, k), lambda i, j: (0, 0)),
            pl.BlockSpec((None, k, tn), lambda i, j: (layer, 0, j)),
            pl.BlockSpec((None, k, tn), lambda i, j: (layer, 0, j + nj)),
        ],
        out_specs=[pl.BlockSpec((tm, tn), lambda i, j: (i, j)),
                   pl.BlockSpec((tm, tn), lambda i, j: (i, j))],
        out_shape=[jax.ShapeDtypeStruct((m, half), F32), jax.ShapeDtypeStruct((m, half), F32)],
        scratch_shapes=[pltpu.VMEM((tm, k), BF16)],
        compiler_params=_params("parallel", "arbitrary"),
        name="norm_lru_in",
    )(x, g, w, w)


def _proj_res_kernel(a_ref, w_ref, b_ref, r_ref, o_ref, wb_ref):
    @pl.when(pl.program_id(1) == 0)
    def _():
        wb_ref[...] = w_ref[...].astype(BF16)
    acc = jnp.dot(a_ref[...], wb_ref[...], preferred_element_type=F32)
    o_ref[...] = r_ref[...] + (acc + b_ref[...])


def _proj_res(a, w, layer, b, res, *, tm=512, tn=2048):
    m, k = a.shape
    n = w.shape[2]
    return pl.pallas_call(
        _proj_res_kernel,
        grid=(n // tn, m // tm),
        in_specs=[
            pl.BlockSpec((tm, k), lambda j, i: (i, 0)),
            pl.BlockSpec((None, k, tn), lambda j, i: (layer, 0, j), pipeline_mode=pl.Buffered(1)),
            pl.BlockSpec((1, tn), lambda j, i: (0, j)),
            pl.BlockSpec((tm, tn), lambda j, i: (i, j)),
        ],
        out_specs=pl.BlockSpec((tm, tn), lambda j, i: (i, j)),
        out_shape=jax.ShapeDtypeStruct((m, n), F32),
        scratch_shapes=[pltpu.VMEM((k, tn), BF16)],
        compiler_params=_params("parallel", "arbitrary"),
        name="proj_res",
    )(a, w, b, res)


def _mlp_kernel(*refs, tm, final_norm, cast):
    refs = list(refs)
    x_ref, g_ref, w1_ref, w2_ref = refs[:4]
    del refs[:4]
    if final_norm:
        fg_ref = refs.pop(0)
    if cast:
        _cast_slabs(refs[0], refs[1], refs[3], refs[4])
        o_ref = refs[2]
    else:
        o_ref = refs[0]
    xn_ref = refs[-1]
    f = pl.program_id(1)

    @pl.when(f == 0)
    def _():
        _rms_rows_to(xn_ref, x_ref, g_ref, tm, 1e-6)
        o_ref[...] = x_ref[...]

    z = jnp.maximum(jnp.dot(xn_ref[...], w1_ref[...], preferred_element_type=F32), 0.0)
    o_ref[...] += jnp.dot((z * z).astype(BF16), w2_ref[...], preferred_element_type=F32)

    if final_norm:
        @pl.when(f == pl.num_programs(1) - 1)
        def _():
            def body(c, carry):
                rows = pl.ds(pl.multiple_of(c * NORM_ROWS, NORM_ROWS), NORM_ROWS)
                y = o_ref[rows, :]
                ms = jnp.mean(y * y, axis=-1, keepdims=True)
                o_ref[rows, :] = y * lax.rsqrt(ms + 1e-6) * fg_ref[...]
                return carry
            lax.fori_loop(0, tm // NORM_ROWS, body, 0)


def _mlp(x, g, w1, w2, final_g=None, mlp_weights=None, cast_layer=None, *, tm=512, tf=1024):
    m, d = x.shape
    ff = w1.shape[1]
    final_norm = final_g is not None
    cast = cast_layer is not None
    grid = (m // tm, ff // tf)
    row_spec = pl.BlockSpec((1, d), lambda i, f: (0, 0))
    cast_in, cast_out, cast_shapes = _mlp_cast_specs(grid, cast_layer) if cast else ([], [], [])
    return pl.pallas_call(
        functools.partial(_mlp_kernel, tm=tm, final_norm=final_norm, cast=cast),
        grid=grid,
        in_specs=[
            pl.BlockSpec((tm, d), lambda i, f: (i, 0)),
            row_spec,
            pl.BlockSpec((d, tf), lambda i, f: (0, f)),
            pl.BlockSpec((tf, d), lambda i, f: (f, 0)),
        ] + ([row_spec] if final_norm else []) + cast_in,
        out_specs=[pl.BlockSpec((tm, d), lambda i, f: (i, 0))] + cast_out,
        out_shape=[jax.ShapeDtypeStruct((m, d), F32)] + cast_shapes,
        scratch_shapes=[pltpu.VMEM((tm, d), BF16)],
        compiler_params=_params("parallel", "arbitrary"),
        name="mlp",
    )(x, g, w1, w2, *((final_g,) if final_norm else ()), *(mlp_weights if cast else ()))


ATTN_T = 512
ATTN_STRIP = 32
ATTN_Q_PER_STEP = 4
LOG2E = math.log2(math.e)


def _bucket_tiles(t):
    r = np.arange(t, dtype=np.int32)[:, None]
    c = np.arange(t, dtype=np.int32)[None, :]
    tiles = []
    for d in range(2):
        n = np.maximum(d * t + r - c, 0)
        max_exact = NUM_BUCKETS // 2
        nf = np.maximum(n, 1).astype(np.float32)
        large = max_exact + (np.log(nf / np.float32(max_exact)) / np.float32(math.log(MAX_DISTANCE / max_exact))
                             * np.float32(NUM_BUCKETS - max_exact)).astype(np.int32)
        large = np.minimum(large, NUM_BUCKETS - 1)
        bucket = np.where(n < max_exact, n, large).astype(np.int32)
        tiles.append(np.where(d * t + r - c >= 0, bucket, -1))
    return np.stack(tiles)


def _bias_needed(t):
    buckets = _bucket_tiles(t)
    last = NUM_BUCKETS - 1
    return [[[bool((buckets[d, r:r + ATTN_STRIP, c:c + LANES] != last).any())
              for c in range(0, t, LANES)] for r in range(0, t, ATTN_STRIP)] for d in range(2)]


def _bias_tiles_kernel(rb_ref, bucket_ref, o_ref):
    h = pl.program_id(0)
    bucket = bucket_ref[0]
    acc = jnp.zeros(bucket.shape, F32)
    for j in range(NUM_BUCKETS):
        acc = jnp.where(bucket == j, rb_ref[j, h], acc)
    acc = (acc - rb_ref[NUM_BUCKETS - 1, h]) * LOG2E
    o_ref[0, 0] = jnp.where(bucket < 0, MASK_VALUE, acc)


def _bias_tiles(rel_bias, t):
    buckets = jnp.asarray(_bucket_tiles(t))
    return pl.pallas_call(
        _bias_tiles_kernel,
        grid=(N_HEADS, 2),
        in_specs=[
            pl.BlockSpec(memory_space=pltpu.SMEM),
            pl.BlockSpec((1, t, t), lambda h, d: (d, 0, 0)),
        ],
        out_specs=pl.BlockSpec((1, 1, t, t), lambda h, d: (h, d, 0, 0)),
        out_shape=jax.ShapeDtypeStruct((N_HEADS, 2, t, t), F32),
        compiler_params=_params("parallel", "parallel"),
        name="bias_tiles",
    )(rel_bias, buckets)


def _attn_kernel(*refs, t, lambda_init, cast, q_per_step):
    lq1_ref, lk1_ref, lq2_ref, lk2_ref, sg_ref, q_ref, k_ref, v_ref, bias_ref = refs[:9]
    if cast:
        w1_ref, w2_ref, o_ref, w1b_ref, w2b_ref = refs[9:14]
        _cast_slabs(w1_ref, w2_ref, w1b_ref, w2b_ref)
    else:
        o_ref = refs[9]
    m_ref, l_ref, alpha_ref, acc_ref, s_ref, p_ref = refs[-6:]
    chunks = [slice(j * LANES, (j + 1) * LANES) for j in range(t // LANES)]
    bias_needed = _bias_needed(t)

    def start(ki):
        return pl.multiple_of(jnp.maximum(ki, 0) * t, t)

    def qk_logits(jq, ki, slot):
        for c in range(2):
            lanes = slice(c * HEAD_DIM, (c + 1) * HEAD_DIM)
            s_ref[slot, c] = lax.dot_general(q_ref[pl.ds(start(jq), t), lanes], k_ref[pl.ds(start(ki), t), lanes],
                                             (((1,), (1,)), ((), ())), preferred_element_type=F32)

    def accumulate(ki, slot):
        v = v_ref[pl.ds(start(ki), t), :]
        for c in range(2):
            alpha = alpha_ref[slot, c]
            pv = jnp.dot(p_ref[slot, c], v, preferred_element_type=F32)
            acc_ref[c] = jnp.concatenate([alpha, alpha], axis=1) * acc_ref[c] + pv

    strips = [slice(r * ATTN_STRIP, (r + 1) * ATTN_STRIP) for r in range(t // ATTN_STRIP)]

    def block(jq, ki, cur, bias_index, has_next):
        other = 1 - cur

        def logits(c, rows, ch):
            s = s_ref[cur, c, rows, ch]
            if bias_index is not None and bias_needed[bias_index][rows.start // ATTN_STRIP][ch.start // LANES]:
                s = s + bias_ref[0, bias_index, rows, ch]
            return s

        part_max = [[], []]
        for rows in strips:
            for c in range(2):
                part = logits(c, rows, chunks[0])
                for ch in chunks[1:]:
                    part = jnp.maximum(part, logits(c, rows, ch))
                part_max[c].append(part)

        accumulate(ki - 1, other)
        m_new, alpha = [], []
        for c in range(2):
            m_prev = m_ref[c]
            m_c = jnp.maximum(m_prev, jnp.max(jnp.concatenate(part_max[c], axis=0), axis=1, keepdims=True))
            alpha.append(jnp.exp2(m_prev - m_c))
            alpha_ref[cur, c] = alpha[c]
            m_ref[c] = m_c
            m_new.append(m_c)

        part_sum = [[], []]
        for rows in strips:
            for c in range(2):
                m_rows = m_new[c][rows, :]
                part = None
                for ch in chunks:
                    p = jnp.exp2(logits(c, rows, ch) - m_rows)
                    part = p if part is None else part + p
                    p_ref[cur, c, rows, ch] = p.astype(BF16)
                part_sum[c].append(part)

        if has_next:
            qk_logits(jq, ki + 1, other)
        for c in range(2):
            row_sum = jnp.sum(jnp.concatenate(part_sum[c], axis=0), axis=1, keepdims=True)
            l_ref[c] = alpha[c] * l_ref[c] + row_sum

    lam = (jnp.exp(jnp.sum(lq1_ref[...] * lk1_ref[...], keepdims=True))
           - jnp.exp(jnp.sum(lq2_ref[...] * lk2_ref[...], keepdims=True)) + lambda_init)

    def q_block(jq, carry):
        qi = pl.program_id(2) * q_per_step + jq

        m_ref[...] = jnp.full(m_ref.shape, MASK_VALUE, F32)
        l_ref[...] = jnp.zeros(l_ref.shape, F32)
        acc_ref[...] = jnp.zeros(acc_ref.shape, F32)
        alpha_ref[1] = jnp.zeros(alpha_ref.shape[1:], F32)
        p_ref[1] = jnp.zeros(p_ref.shape[1:], BF16)

        n_far = jnp.maximum(qi - 1, 0)

        def far_pair(j, carry):
            block(jq, 2 * j, 0, None, True)
            block(jq, 2 * j + 1, 1, None, True)
            return carry
        lax.fori_loop(0, n_far // 2, far_pair, 0)

        @pl.when(qi % 2 == 1)
        def _():
            block(jq, qi - 1, 0, 1, True)
            block(jq, qi, 1, 0, False)
            accumulate(qi, 1)

        @pl.when(qi % 2 == 0)
        def _():
            @pl.when(qi >= 2)
            def _():
                block(jq, qi - 2, 0, None, True)
                block(jq, qi - 1, 1, 1, True)
            block(jq, qi, 0, 0, False)
            accumulate(qi, 0)

        qk_logits(jnp.minimum(jq + 1, q_per_step - 1), 0, 0)
        inv0 = 1.0 / l_ref[0]
        inv1 = lam / l_ref[1]
        o = (acc_ref[0] * jnp.concatenate([inv0, inv0], axis=1)
             - acc_ref[1] * jnp.concatenate([inv1, inv1], axis=1))
        ms = jnp.mean(o * o, axis=-1, keepdims=True)
        o = o * lax.rsqrt(ms + 1e-5) * sg_ref[...] * (1.0 - lambda_init)
        o_ref[pl.ds(start(jq), t), :] = o.astype(o_ref.dtype)
        return carry

    qk_logits(0, 0, 0)
    lax.fori_loop(0, q_per_step, q_block, 0)


def _diff_attention(qkv, bias_tiles, lq1, lk1, lq2, lk2, subln_g, mlp_weights=None, cast_layer=None,
                    *, batch, seq, lambda_init):
    t = ATTN_T
    qps = ATTN_Q_PER_STEP
    groups = seq // (t * qps)
    grid = (batch, N_HEADS, groups)
    lam_spec = pl.BlockSpec((1, HEAD_DIM), lambda b, h, i: (0, 0))
    cast = cast_layer is not None
    cast_in, cast_out, cast_shapes = _mlp_cast_specs(grid, cast_layer) if cast else ([], [], [])
    return pl.pallas_call(
        functools.partial(_attn_kernel, t=t, lambda_init=lambda_init, cast=cast, q_per_step=qps),
        grid=grid,
        in_specs=[
            lam_spec, lam_spec, lam_spec, lam_spec,
            pl.BlockSpec((1, HEAD_W), lambda b, h, i: (0, 0)),
            pl.BlockSpec((qps * t, HEAD_W), lambda b, h, i: (b * groups + i, h)),
            pl.BlockSpec((seq, HEAD_W), lambda b, h, i: (b, N_HEADS + h)),
            pl.BlockSpec((seq, HEAD_W), lambda b, h, i: (b, 2 * N_HEADS + h)),
            pl.BlockSpec((1, 2, t, t), lambda b, h, i: (h, 0, 0, 0)),
        ] + cast_in,
        out_specs=[pl.BlockSpec((qps * t, HEAD_W), lambda b, h, i: (b * groups + i, h))] + cast_out,
        out_shape=[jax.ShapeDtypeStruct((batch * seq, D_MODEL), BF16)] + cast_shapes,
        scratch_shapes=[pltpu.VMEM((2, t, LANES), F32), pltpu.VMEM((2, t, LANES), F32),
                        pltpu.VMEM((2, 2, t, LANES), F32),
                        pltpu.VMEM((2, t, HEAD_W), F32), pltpu.VMEM((2, 2, t, t), F32),
                        pltpu.VMEM((2, 2, t, t), BF16)],
        compiler_params=_params("parallel", "parallel", "arbitrary"),
        name="diff_attn",
    )(lq1, lk1, lq2, lk2, subln_g, qkv, qkv, qkv, bias_tiles, *(mlp_weights if cast else ()))


CONV_T = 256
CONV_HALO = 32
CONV_ROWS = 64
CONV_LANES = 256


def _conv_ln_kernel(halo_ref, u_ref, w_ref, b_ref, g_ref, beta_ref, o_ref, sh_ref, y_ref, *, tiles_per_seq):
    i = pl.program_id(0)
    first = (i % tiles_per_seq) == 0

    @pl.when(first)
    def _():
        sh_ref[0, 0:CONV_HALO, :] = jnp.zeros((CONV_HALO, D_MODEL), F32)

    @pl.when(jnp.logical_not(first))
    def _():
        sh_ref[0, 0:CONV_HALO, :] = halo_ref[...]

    sh_ref[0, CONV_HALO:, :] = u_ref[...]
    window = CONV_HALO + CONV_T
    for s in range(1, SUBLANES):
        for c0 in range(0, D_MODEL, CONV_LANES):
            lanes = slice(c0, c0 + CONV_LANES)
            sh_ref[s, 0:window - SUBLANES, lanes] = sh_ref[0, s:s + window - SUBLANES, lanes]

    base = CONV_HALO - (CONV_WIDTH - 1)
    for r0 in range(0, CONV_T, CONV_ROWS):
        for c0 in range(0, D_MODEL, CONV_LANES):
            lanes = slice(c0, c0 + CONV_LANES)
            acc = jnp.zeros((CONV_ROWS, CONV_LANES), F32)
            for j in range(CONV_WIDTH):
                a, s = divmod(base + j, SUBLANES)
                x0 = r0 + a * SUBLANES
                acc = acc + w_ref[j:j + 1, lanes] * sh_ref[s, x0:x0 + CONV_ROWS, lanes]
            y_ref[r0:r0 + CONV_ROWS, lanes] = acc + b_ref[:, lanes]

    def ln_body(c, carry):
        r0 = pl.multiple_of(c * CONV_ROWS, CONV_ROWS)
        y = y_ref[pl.ds(r0, CONV_ROWS), :]
        mu = jnp.mean(y, axis=-1, keepdims=True)
        yc = y - mu
        var = jnp.mean(yc * yc, axis=-1, keepdims=True)
        z = yc * lax.rsqrt(var + 1e-5) * g_ref[...] + beta_ref[...]
        o_ref[pl.ds(r0, CONV_ROWS), :] = (z * _sigmoid(z)).astype(o_ref.dtype)
        return carry
    lax.fori_loop(0, CONV_T // CONV_ROWS, ln_body, 0)


def _conv_ln(u, dw_w, dw_b, ln_g, ln_b, *, seq):
    m, d = u.shape
    tiles_per_seq = seq // CONV_T
    ratio = CONV_T // CONV_HALO
    row = lambda i: (0, 0)
    return pl.pallas_call(
        functools.partial(_conv_ln_kernel, tiles_per_seq=tiles_per_seq),
        grid=(m // CONV_T,),
        in_specs=[
            pl.BlockSpec((CONV_HALO, d), lambda i: (jnp.maximum(i * ratio - 1, 0), 0)),
            pl.BlockSpec((CONV_T, d), lambda i: (i, 0)),
            pl.BlockSpec((CONV_WIDTH, d), row),
            pl.BlockSpec((1, d), row), pl.BlockSpec((1, d), row), pl.BlockSpec((1, d), row),
        ],
        out_specs=pl.BlockSpec((CONV_T, d), lambda i: (i, 0)),
        out_shape=jax.ShapeDtypeStruct((m, d), BF16),
        scratch_shapes=[pltpu.VMEM((SUBLANES, CONV_HALO + CONV_T, d), F32), pltpu.VMEM((CONV_T, d), F32)],
        compiler_params=_params("parallel"),
        name="conv_ln",
    )(u, u, dw_w, dw_b, ln_g, ln_b)


LRU_T = 512
LRU_HALO = 8


def _lru_kernel(halo_ref, u_ref, y_ref, cw_ref, cb_ref, wa_ref, ba_ref, wi_ref, bi_ref, lam_ref,
                o_ref, buf_ref, carry_ref):
    ti = pl.program_id(2)
    first = ti == 0
    buf_ref[LRU_HALO:, :] = u_ref[...]

    @pl.when(first)
    def _():
        buf_ref[0:LRU_HALO, :] = jnp.zeros((LRU_HALO, LRU_BLOCK_SIZE), F32)
        carry_ref[...] = jnp.zeros(carry_ref.shape, F32)

    @pl.when(jnp.logical_not(first))
    def _():
        buf_ref[0:LRU_HALO, :] = halo_ref[...]

    base = LRU_HALO - (LRU_CONV_WIDTH - 1)
    uc = jnp.zeros((LRU_T, LRU_BLOCK_SIZE), F32)
    for j in range(LRU_CONV_WIDTH):
        uc = uc + cw_ref[j:j + 1, :] * buf_ref[base + j:base + j + LRU_T, :]
    uc = uc + cb_ref[...]

    ucb = uc.astype(BF16)
    r = _sigmoid(jnp.dot(ucb, wa_ref[0].astype(BF16), preferred_element_type=F32) + ba_ref[...])
    gate_i = _sigmoid(jnp.dot(ucb, wi_ref[0].astype(BF16), preferred_element_type=F32) + bi_ref[...])
    neg_lam = -lam_ref[...]
    softplus = jnp.maximum(neg_lam, 0.0) + jnp.log1p(jnp.exp(-jnp.abs(neg_lam)))
    log_a = -LRU_C * r * softplus
    a = jnp.exp(log_a)
    mult = jnp.sqrt(-jnp.tanh(log_a) * (1.0 + a * a))
    b = mult * (gate_i * uc)

    row = lax.broadcasted_iota(jnp.int32, (LRU_T, LRU_BLOCK_SIZE), 0)
    s = 1
    while s < SUBLANES:
        valid = row >= s
        a_sh = jnp.where(valid, pltpu.roll(a, s, 0), 1.0)
        b_sh = jnp.where(valid, pltpu.roll(b, s, 0), 0.0)
        b = a * b_sh + b
        a = a * a_sh
        s *= 2
    while s < LRU_T:
        b = jnp.concatenate([b[:s], a[s:] * b[:-s] + b[s:]], axis=0)
        a = jnp.concatenate([a[:s], a[s:] * a[:-s]], axis=0)
        s *= 2
    hs = a * carry_ref[0:1, :] + b
    carry_ref[0:1, :] = hs[LRU_T - 1:LRU_T, :]
    o_ref[...] = (hs * y_ref[...]).astype(o_ref.dtype)


def _lru_core(u, y, conv_w, conv_b, w_a, b_a, w_i, b_i, lam, *, batch, seq):
    m, d = u.shape
    nt = seq // LRU_T
    ratio = LRU_T // LRU_HALO
    c = LRU_BLOCK_SIZE
    chan = lambda b, g, t: (0, g)
    tile = lambda b, g, t: (b * nt + t, g)
    gate_w = pl.BlockSpec((1, c, c), lambda b, g, t: (g, 0, 0))
    return pl.pallas_call(
        _lru_kernel,
        grid=(batch, LRU_BLOCKS, nt),
        in_specs=[
            pl.BlockSpec((LRU_HALO, c), lambda b, g, t: (jnp.maximum((b * nt + t) * ratio - 1, 0), g)),
            pl.BlockSpec((LRU_T, c), tile),
            pl.BlockSpec((LRU_T, c), tile),
            pl.BlockSpec((LRU_CONV_WIDTH, c), chan),
            pl.BlockSpec((1, c), chan),
            gate_w, pl.BlockSpec((1, c), chan),
            gate_w, pl.BlockSpec((1, c), chan),
            pl.BlockSpec((1, c), chan),
        ],
        out_specs=pl.BlockSpec((LRU_T, c), tile),
        out_shape=jax.ShapeDtypeStruct((m, d), BF16),
        scratch_shapes=[pltpu.VMEM((LRU_HALO + LRU_T, c), F32), pltpu.VMEM((8, c), F32)],
        compiler_params=_params("parallel", "parallel", "arbitrary"),
        name="lru_core",
    )(u, u, y, conv_w, conv_b, w_a, b_a, w_i, b_i, lam)


def kernel(x, rel_bias, mixer_norm_g, attn_w_qkv, attn_lq1, attn_lk1, attn_lq2, attn_lk2, attn_subln_g, attn_w_o, conv_w_in, conv_b_in, conv_dw_w, conv_dw_b, conv_ln_g, conv_ln_b, conv_w_out, conv_b_out, lru_w_in, lru_conv_w, lru_conv_b, lru_w_a, lru_b_a, lru_w_i, lru_b_i, lru_lambda, lru_w_out, mlp_norm_g, mlp_w1, mlp_w2, final_norm_g):
    batch, seq, d = x.shape
    h = x.reshape(batch * seq, d)
    row = lambda v: v.reshape(1, -1)
    zero_bias = jnp.zeros((1, d), F32)
    q_scale = jnp.concatenate([jnp.full((1, d), HEAD_DIM ** -0.5 * LOG2E, F32), jnp.ones((1, 2 * d), F32)], axis=1)
    bias_tiles = _bias_tiles(rel_bias, ATTN_T)

    ia = ic = il = 0
    for layer in range(DEPTH):
        kind = layer % N_MIXERS
        g = row(mixer_norm_g[layer])
        if kind == 0:
            lambda_init = 0.8 - 0.6 * math.exp(-0.3 * layer)
            qkv = _norm_proj(h, g, attn_w_qkv, ia, q_scale)
            outs = _diff_attention(qkv, bias_tiles, row(attn_lq1[ia]), row(attn_lk1[ia]),
                                   row(attn_lq2[ia]), row(attn_lk2[ia]), row(attn_subln_g[ia]),
                                   (mlp_w1, mlp_w2), 0 if layer == 0 else None,
                                   batch=batch, seq=seq, lambda_init=lambda_init)
            if layer == 0:
                w1, w2 = outs[1:]
            h = _proj_res(outs[0], attn_w_o, ia, zero_bias, h)
            ia += 1
        elif kind == 1:
            u = _norm_glu(h, g, conv_w_in, ic, row(conv_b_in[ic]))
            act = _conv_ln(u, conv_dw_w[ic], row(conv_dw_b[ic]), row(conv_ln_g[ic]), row(conv_ln_b[ic]), seq=seq)
            h = _proj_res(act, conv_w_out, ic, row(conv_b_out[ic]), h)
            ic += 1
        else:
            u, y = _norm_lru_in(h, g, lru_w_in, il)
            hy = _lru_core(u, y, lru_conv_w[il], row(lru_conv_b[il]), lru_w_a[il], row(lru_b_a[il]),
                           lru_w_i[il], row(lru_b_i[il]), row(lru_lambda[il]), batch=batch, seq=seq)
            h = _proj_res(hy, lru_w_out, il, zero_bias, h)
            il += 1
        last = layer == DEPTH - 1
        outs = _mlp(h, row(mlp_norm_g[layer]), w1, w2, row(final_norm_g) if last else None,
                    (mlp_w1, mlp_w2), None if last else layer + 1)
        h = outs[0]
        if not last:
            w1, w2 = outs[1:]
    return h.reshape(batch, seq, d)
```

```python
import functools
import math

import numpy as np
import jax
import jax.numpy as jnp
from jax import lax
from jax.experimental import pallas as pl
from jax.experimental.pallas import tpu as pltpu

F32 = jnp.float32
BF16 = jnp.bfloat16

D_MODEL = 2048
DEPTH = 4
N_MIXERS = 3
HEAD_DIM = 128
HEAD_W = 2 * HEAD_DIM
N_HEADS = D_MODEL // HEAD_W
NUM_BUCKETS = 32
MAX_DISTANCE = 128
MASK_VALUE = -1e30
CONV_WIDTH = 31
LRU_BLOCKS = 8
LRU_BLOCK_SIZE = D_MODEL // LRU_BLOCKS
LRU_CONV_WIDTH = 4
LRU_C = 8.0
D_FF = 4 * D_MODEL

VMEM_LIMIT = 56 * 1024 * 1024
NORM_ROWS = 128
LANES = 128
SUBLANES = 8
BF16_ROWS = 16
MLP_CAST_EVERY = 2


def _params(*sem):
    return pltpu.CompilerParams(dimension_semantics=sem, vmem_limit_bytes=VMEM_LIMIT)


def _rms_rows_to(dst_ref, x_ref, g_ref, rows, eps):
    def body(c, carry):
        r0 = pl.multiple_of(c * NORM_ROWS, NORM_ROWS)
        x = x_ref[pl.ds(r0, NORM_ROWS), :]
        ms = jnp.mean(x * x, axis=-1, keepdims=True)
        dst_ref[pl.ds(r0, NORM_ROWS), :] = (x * lax.rsqrt(ms + eps) * g_ref[...]).astype(BF16)
        return carry
    lax.fori_loop(0, rows // NORM_ROWS, body, 0)


def _mlp_cast_specs(grid, layer, every=1):
    d, ff = D_MODEL, D_FF
    steps = math.prod(grid) // every

    def lin(*idx):
        n = 0
        for i, g in zip(idx, grid):
            n = n * g + i
        return n // every

    split = max(1, BF16_ROWS * steps // d)
    r1, c1, r2 = d * split // steps, ff // split, ff // steps
    in_specs = [pl.BlockSpec((None, r1, c1), lambda *idx: (layer, lin(*idx) // split, lin(*idx) % split)),
                pl.BlockSpec((None, r2, d), lambda *idx: (layer, lin(*idx), 0))]
    out_specs = [pl.BlockSpec((r1, c1), lambda *idx: (lin(*idx) // split, lin(*idx) % split)),
                 pl.BlockSpec((r2, d), lambda *idx: (lin(*idx), 0))]
    out_shapes = [jax.ShapeDtypeStruct((d, ff), BF16), jax.ShapeDtypeStruct((ff, d), BF16)]
    return in_specs, out_specs, out_shapes


def _cast_slabs(w1_ref, w2_ref, w1b_ref, w2b_ref):
    w1b_ref[...] = w1_ref[...].astype(BF16)
    w2b_ref[...] = w2_ref[...].astype(BF16)


def _sigmoid(x):
    return 0.5 * (jnp.tanh(0.5 * x) + 1.0)


def _gelu_tanh(x):
    c = math.sqrt(2.0 / math.pi)
    return 0.5 * x * (1.0 + jnp.tanh(c * (x + 0.044715 * (x * x * x))))


def _norm_proj_kernel(x_ref, g_ref, w_ref, s_ref, o_ref, xn_ref, *, tm):
    @pl.when(pl.program_id(1) == 0)
    def _():
        _rms_rows_to(xn_ref, x_ref, g_ref, tm, 1e-6)
    acc = jnp.dot(xn_ref[...], w_ref[...].astype(BF16), preferred_element_type=F32)
    o_ref[...] = (acc * s_ref[...]).astype(o_ref.dtype)


def _norm_proj(x, g, w, layer, col_scale, *, tm=1024, tn=1024):
    m, k = x.shape
    n = w.shape[2]
    return pl.pallas_call(
        functools.partial(_norm_proj_kernel, tm=tm),
        grid=(m // tm, n // tn),
        in_specs=[
            pl.BlockSpec((tm, k), lambda i, j: (i, 0)),
            pl.BlockSpec((1, k), lambda i, j: (0, 0)),
            pl.BlockSpec((None, k, tn), lambda i, j: (layer, 0, j)),
            pl.BlockSpec((1, tn), lambda i, j: (0, j)),
        ],
        out_specs=pl.BlockSpec((tm, tn), lambda i, j: (i, j)),
        out_shape=jax.ShapeDtypeStruct((m, n), BF16),
        scratch_shapes=[pltpu.VMEM((tm, k), BF16)],
        compiler_params=_params("parallel", "arbitrary"),
        name="norm_proj",
    )(x, g, w, col_scale)


def _norm_glu_kernel(x_ref, g_ref, wa_ref, wg_ref, ba_ref, bg_ref, o_ref, xn_ref, *, tm):
    @pl.when(pl.program_id(1) == 0)
    def _():
        _rms_rows_to(xn_ref, x_ref, g_ref, tm, 1e-6)
    xn = xn_ref[...]
    a = jnp.dot(xn, wa_ref[...].astype(BF16), preferred_element_type=F32) + ba_ref[...]
    gate = jnp.dot(xn, wg_ref[...].astype(BF16), preferred_element_type=F32) + bg_ref[...]
    o_ref[...] = a * _sigmoid(gate)


def _norm_glu(x, g, w, layer, b, *, tm=1024, tn=512):
    m, k = x.shape
    half = w.shape[2] // 2
    nj = half // tn
    return pl.pallas_call(
        functools.partial(_norm_glu_kernel, tm=tm),
        grid=(m // tm, nj),
        in_specs=[
            pl.BlockSpec((tm, k), lambda i, j: (i, 0)),
            pl.BlockSpec((1, k), lambda i, j: (0, 0)),
            pl.BlockSpec((None, k, tn), lambda i, j: (layer, 0, j)),
            pl.BlockSpec((None, k, tn), lambda i, j: (layer, 0, j + nj)),
            pl.BlockSpec((1, tn), lambda i, j: (0, j)),
            pl.BlockSpec((1, tn), lambda i, j: (0, j + nj)),
        ],
        out_specs=pl.BlockSpec((tm, tn), lambda i, j: (i, j)),
        out_shape=jax.ShapeDtypeStruct((m, half), F32),
        scratch_shapes=[pltpu.VMEM((tm, k), BF16)],
        compiler_params=_params("parallel", "arbitrary"),
        name="norm_glu",
    )(x, g, w, w, b, b)


def _norm_lru_in_kernel(x_ref, g_ref, wu_ref, wy_ref, u_ref, y_ref, xn_ref, *, tm):
    @pl.when(pl.program_id(1) == 0)
    def _():
        _rms_rows_to(xn_ref, x_ref, g_ref, tm, 1e-6)
    xn = xn_ref[...]
    u_ref[...] = jnp.dot(xn, wu_ref[...].astype(BF16), preferred_element_type=F32)
    y_ref[...] = _gelu_tanh(jnp.dot(xn, wy_ref[...].astype(BF16), preferred_element_type=F32))


def _norm_lru_in(x, g, w, layer, *, tm=1024, tn=512):
    m, k = x.shape
    half = w.shape[2] // 2
    nj = half // tn
    return pl.pallas_call(
        functools.partial(_norm_lru_in_kernel, tm=tm),
        grid=(m // tm, nj),
        in_specs=[
            pl.BlockSpec((tm, k), lambda i, j: (i, 0)),
            pl.BlockSpec((1, k), lambda i, j: (0, 0)),
            pl.BlockSpec((None, k, tn), lambda i, j: (layer, 0, j)),
            pl.BlockSpec((None, k, tn), lambda i, j: (layer, 0, j + nj)),
        ],
        out_specs=[pl.BlockSpec((tm, tn), lambda i, j: (i, j)),
                   pl.BlockSpec((tm, tn), lambda i, j: (i, j))],
        out_shape=[jax.ShapeDtypeStruct((m, half), F32), jax.ShapeDtypeStruct((m, half), F32)],
        scratch_shapes=[pltpu.VMEM((tm, k), BF16)],
        compiler_params=_params("parallel", "arbitrary"),
        name="norm_lru_in",
    )(x, g, w, w)


def _proj_res_kernel(a_ref, w_ref, b_ref, r_ref, o_ref, wb_ref):
    @pl.when(pl.program_id(1) == 0)
    def _():
        wb_ref[...] = w_ref[...].astype(BF16)
    acc = jnp.dot(a_ref[...], wb_ref[...], preferred_element_type=F32)
    o_ref[...] = r_ref[...] + (acc + b_ref[...])


def _proj_res(a, w, layer, b, res, *, tm=512, tn=2048):
    m, k = a.shape
    n = w.shape[2]
    return pl.pallas_call(
        _proj_res_kernel,
        grid=(n // tn, m // tm),
        in_specs=[
            pl.BlockSpec((tm, k), lambda j, i: (i, 0)),
            pl.BlockSpec((None, k, tn), lambda j, i: (layer, 0, j), pipeline_mode=pl.Buffered(1)),
            pl.BlockSpec((1, tn), lambda j, i: (0, j)),
            pl.BlockSpec((tm, tn), lambda j, i: (i, j)),
        ],
        out_specs=pl.BlockSpec((tm, tn), lambda j, i: (i, j)),
        out_shape=jax.ShapeDtypeStruct((m, n), F32),
        scratch_shapes=[pltpu.VMEM((k, tn), BF16)],
        compiler_params=_params("parallel", "arbitrary"),
        name="proj_res",
    )(a, w, b, res)


def _mlp_kernel(*refs, tm, final_norm, cast):
    refs = list(refs)
    x_ref, g_ref, w1_ref, w2_ref = refs[:4]
    del refs[:4]
    if final_norm:
        fg_ref = refs.pop(0)
    f = pl.program_id(1)
    if cast:
        @pl.when(f % MLP_CAST_EVERY == 0)
        def _():
            _cast_slabs(refs[0], refs[1], refs[3], refs[4])
        o_ref = refs[2]
    else:
        o_ref = refs[0]
    xn_ref = refs[-1]

    @pl.when(f == 0)
    def _():
        _rms_rows_to(xn_ref, x_ref, g_ref, tm, 1e-6)
        o_ref[...] = x_ref[...]

    z = jnp.maximum(jnp.dot(xn_ref[...], w1_ref[...], preferred_element_type=F32), 0.0)
    o_ref[...] += jnp.dot((z * z).astype(BF16), w2_ref[...], preferred_element_type=F32)

    if final_norm:
        @pl.when(f == pl.num_programs(1) - 1)
        def _():
            def body(c, carry):
                rows = pl.ds(pl.multiple_of(c * NORM_ROWS, NORM_ROWS), NORM_ROWS)
                y = o_ref[rows, :]
                ms = jnp.mean(y * y, axis=-1, keepdims=True)
                o_ref[rows, :] = y * lax.rsqrt(ms + 1e-6) * fg_ref[...]
                return carry
            lax.fori_loop(0, tm // NORM_ROWS, body, 0)


def _mlp(x, g, w1, w2, final_g=None, mlp_weights=None, cast_layer=None, *, tm=512, tf=1024):
    m, d = x.shape
    ff = w1.shape[1]
    final_norm = final_g is not None
    cast = cast_layer is not None
    grid = (m // tm, ff // tf)
    row_spec = pl.BlockSpec((1, d), lambda i, f: (0, 0))
    cast_in, cast_out, cast_shapes = _mlp_cast_specs(grid, cast_layer, MLP_CAST_EVERY) if cast else ([], [], [])
    return pl.pallas_call(
        functools.partial(_mlp_kernel, tm=tm, final_norm=final_norm, cast=cast),
        grid=grid,
        in_specs=[
            pl.BlockSpec((tm, d), lambda i, f: (i, 0)),
            row_spec,
            pl.BlockSpec((d, tf), lambda i, f: (0, f)),
            pl.BlockSpec((tf, d), lambda i, f: (f, 0)),
        ] + ([row_spec] if final_norm else []) + cast_in,
        out_specs=[pl.BlockSpec((tm, d), lambda i, f: (i, 0))] + cast_out,
        out_shape=[jax.ShapeDtypeStruct((m, d), F32)] + cast_shapes,
        scratch_shapes=[pltpu.VMEM((tm, d), BF16)],
        compiler_params=_params("parallel", "arbitrary"),
        name="mlp",
    )(x, g, w1, w2, *((final_g,) if final_norm else ()), *(mlp_weights if cast else ()))


ATTN_T = 512
ATTN_STRIP = 32
ATTN_Q_PER_STEP = 4
LOG2E = math.log2(math.e)


def _bucket_tiles(t):
    r = np.arange(t, dtype=np.int32)[:, None]
    c = np.arange(t, dtype=np.int32)[None, :]
    tiles = []
    for d in range(2):
        n = np.maximum(d * t + r - c, 0)
        max_exact = NUM_BUCKETS // 2
        nf = np.maximum(n, 1).astype(np.float32)
        large = max_exact + (np.log(nf / np.float32(max_exact)) / np.float32(math.log(MAX_DISTANCE / max_exact))
                             * np.float32(NUM_BUCKETS - max_exact)).astype(np.int32)
        large = np.minimum(large, NUM_BUCKETS - 1)
        bucket = np.where(n < max_exact, n, large).astype(np.int32)
        tiles.append(np.where(d * t + r - c >= 0, bucket, -1))
    return np.stack(tiles)


def _bias_needed(t):
    buckets = _bucket_tiles(t)
    last = NUM_BUCKETS - 1
    return [[[bool((buckets[d, r:r + ATTN_STRIP, c:c + LANES] != last).any())
              for c in range(0, t, LANES)] for r in range(0, t, ATTN_STRIP)] for d in range(2)]


def _bias_tiles_kernel(rb_ref, bucket_ref, o_ref):
    h = pl.program_id(0)
    bucket = bucket_ref[0]
    acc = jnp.zeros(bucket.shape, F32)
    for j in range(NUM_BUCKETS):
        acc = jnp.where(bucket == j, rb_ref[j, h], acc)
    acc = (acc - rb_ref[NUM_BUCKETS - 1, h]) * LOG2E
    o_ref[0, 0] = jnp.where(bucket < 0, MASK_VALUE, acc)


def _bias_tiles(rel_bias, t):
    buckets = jnp.asarray(_bucket_tiles(t))
    return pl.pallas_call(
        _bias_tiles_kernel,
        grid=(N_HEADS, 2),
        in_specs=[
            pl.BlockSpec(memory_space=pltpu.SMEM),
            pl.BlockSpec((1, t, t), lambda h, d: (d, 0, 0)),
        ],
        out_specs=pl.BlockSpec((1, 1, t, t), lambda h, d: (h, d, 0, 0)),
        out_shape=jax.ShapeDtypeStruct((N_HEADS, 2, t, t), F32),
        compiler_params=_params("parallel", "parallel"),
        name="bias_tiles",
    )(rel_bias, buckets)


def _attn_kernel(*refs, t, lambda_init, cast, q_per_step):
    lq1_ref, lk1_ref, lq2_ref, lk2_ref, sg_ref, q_ref, k_ref, v_ref, bias_ref = refs[:9]
    if cast:
        w1_ref, w2_ref, o_ref, w1b_ref, w2b_ref = refs[9:14]
        _cast_slabs(w1_ref, w2_ref, w1b_ref, w2b_ref)
    else:
        o_ref = refs[9]
    m_ref, l_ref, alpha_ref, acc_ref, s_ref, p_ref = refs[-6:]
    chunks = [slice(j * LANES, (j + 1) * LANES) for j in range(t // LANES)]
    bias_needed = _bias_needed(t)

    def start(ki):
        return pl.multiple_of(jnp.maximum(ki, 0) * t, t)

    def qk_logits(jq, ki, slot):
        for c in range(2):
            lanes = slice(c * HEAD_DIM, (c + 1) * HEAD_DIM)
            s_ref[slot, c] = lax.dot_general(q_ref[pl.ds(start(jq), t), lanes], k_ref[pl.ds(start(ki), t), lanes],
                                             (((1,), (1,)), ((), ())), preferred_element_type=F32)

    def accumulate(ki, slot):
        v = v_ref[pl.ds(start(ki), t), :]
        for c in range(2):
            alpha = alpha_ref[slot, c]
            pv = jnp.dot(p_ref[slot, c], v, preferred_element_type=F32)
            acc_ref[c] = jnp.concatenate([alpha, alpha], axis=1) * acc_ref[c] + pv

    strips = [slice(r * ATTN_STRIP, (r + 1) * ATTN_STRIP) for r in range(t // ATTN_STRIP)]

    def block(jq, ki, cur, bias_index, has_next):
        other = 1 - cur

        def logits(c, rows, ch):
            s = s_ref[cur, c, rows, ch]
            if bias_index is not None and bias_needed[bias_index][rows.start // ATTN_STRIP][ch.start // LANES]:
                s = s + bias_ref[0, bias_index, rows, ch]
            return s

        part_max = [[], []]
        for rows in strips:
            for c in range(2):
                part = logits(c, rows, chunks[0])
                for ch in chunks[1:]:
                    part = jnp.maximum(part, logits(c, rows, ch))
                part_max[c].append(part)

        accumulate(ki - 1, other)
        m_new, alpha = [], []
        for c in range(2):
            m_prev = m_ref[c]
            m_c = jnp.maximum(m_prev, jnp.max(jnp.concatenate(part_max[c], axis=0), axis=1, keepdims=True))
            alpha.append(jnp.exp2(m_prev - m_c))
            alpha_ref[cur, c] = alpha[c]
            m_ref[c] = m_c
            m_new.append(m_c)

        part_sum = [[], []]
        for rows in strips:
            for c in range(2):
                m_rows = m_new[c][rows, :]
                part = None
                for ch in chunks:
                    p = jnp.exp2(logits(c, rows, ch) - m_rows)
                    part = p if part is None else part + p
                    p_ref[cur, c, rows, ch] = p.astype(BF16)
                part_sum[c].append(part)

        if has_next:
            qk_logits(jq, ki + 1, other)
        for c in range(2):
            row_sum = jnp.sum(jnp.concatenate(part_sum[c], axis=0), axis=1, keepdims=True)
            l_ref[c] = alpha[c] * l_ref[c] + row_sum

    lam = (jnp.exp(jnp.sum(lq1_ref[...] * lk1_ref[...], keepdims=True))
           - jnp.exp(jnp.sum(lq2_ref[...] * lk2_ref[...], keepdims=True)) + lambda_init)

    def q_block(jq, carry):
        qi = pl.program_id(2) * q_per_step + jq

        m_ref[...] = jnp.full(m_ref.shape, MASK_VALUE, F32)
        l_ref[...] = jnp.zeros(l_ref.shape, F32)
        acc_ref[...] = jnp.zeros(acc_ref.shape, F32)
        alpha_ref[1] = jnp.zeros(alpha_ref.shape[1:], F32)
        p_ref[1] = jnp.zeros(p_ref.shape[1:], BF16)

        n_far = jnp.maximum(qi - 1, 0)

        def far_pair(j, carry):
            block(jq, 2 * j, 0, None, True)
            block(jq, 2 * j + 1, 1, None, True)
            return carry
        lax.fori_loop(0, n_far // 2, far_pair, 0)

        @pl.when(qi % 2 == 1)
        def _():
            block(jq, qi - 1, 0, 1, True)
            block(jq, qi, 1, 0, False)
            accumulate(qi, 1)

        @pl.when(qi % 2 == 0)
        def _():
            @pl.when(qi >= 2)
            def _():
                block(jq, qi - 2, 0, None, True)
                block(jq, qi - 1, 1, 1, True)
            block(jq, qi, 0, 0, False)
            accumulate(qi, 0)

        qk_logits(jnp.minimum(jq + 1, q_per_step - 1), 0, 0)
        inv0 = 1.0 / l_ref[0]
        inv1 = lam / l_ref[1]
        o = (acc_ref[0] * jnp.concatenate([inv0, inv0], axis=1)
             - acc_ref[1] * jnp.concatenate([inv1, inv1], axis=1))
        ms = jnp.mean(o * o, axis=-1, keepdims=True)
        o = o * lax.rsqrt(ms + 1e-5) * sg_ref[...] * (1.0 - lambda_init)
        o_ref[pl.ds(start(jq), t), :] = o.astype(o_ref.dtype)
        return carry

    qk_logits(0, 0, 0)
    lax.fori_loop(0, q_per_step, q_block, 0)


def _diff_attention(qkv, bias_tiles, lq1, lk1, lq2, lk2, subln_g, mlp_weights=None, cast_layer=None,
                    *, batch, seq, lambda_init):
    t = ATTN_T
    qps = ATTN_Q_PER_STEP
    groups = seq // (t * qps)
    grid = (batch, N_HEADS, groups)
    lam_spec = pl.BlockSpec((1, HEAD_DIM), lambda b, h, i: (0, 0))
    cast = cast_layer is not None
    cast_in, cast_out, cast_shapes = _mlp_cast_specs(grid, cast_layer) if cast else ([], [], [])
    return pl.pallas_call(
        functools.partial(_attn_kernel, t=t, lambda_init=lambda_init, cast=cast, q_per_step=qps),
        grid=grid,
        in_specs=[
            lam_spec, lam_spec, lam_spec, lam_spec,
            pl.BlockSpec((1, HEAD_W), lambda b, h, i: (0, 0)),
            pl.BlockSpec((qps * t, HEAD_W), lambda b, h, i: (b * groups + i, h)),
            pl.BlockSpec((seq, HEAD_W), lambda b, h, i: (b, N_HEADS + h)),
            pl.BlockSpec((seq, HEAD_W), lambda b, h, i: (b, 2 * N_HEADS + h)),
            pl.BlockSpec((1, 2, t, t), lambda b, h, i: (h, 0, 0, 0)),
        ] + cast_in,
        out_specs=[pl.BlockSpec((qps * t, HEAD_W), lambda b, h, i: (b * groups + i, h))] + cast_out,
        out_shape=[jax.ShapeDtypeStruct((batch * seq, D_MODEL), BF16)] + cast_shapes,
        scratch_shapes=[pltpu.VMEM((2, t, LANES), F32), pltpu.VMEM((2, t, LANES), F32),
                        pltpu.VMEM((2, 2, t, LANES), F32),
                        pltpu.VMEM((2, t, HEAD_W), F32), pltpu.VMEM((2, 2, t, t), F32),
                        pltpu.VMEM((2, 2, t, t), BF16)],
        compiler_params=_params("parallel", "parallel", "arbitrary"),
        name="diff_attn",
    )(lq1, lk1, lq2, lk2, subln_g, qkv, qkv, qkv, bias_tiles, *(mlp_weights if cast else ()))


CONV_T = 256
CONV_HALO = 32
CONV_ROWS = 64
CONV_LANES = 256


def _conv_ln_kernel(halo_ref, u_ref, w_ref, b_ref, g_ref, beta_ref, o_ref, sh_ref, y_ref, *, tiles_per_seq):
    i = pl.program_id(0)
    first = (i % tiles_per_seq) == 0

    @pl.when(first)
    def _():
        sh_ref[0, 0:CONV_HALO, :] = jnp.zeros((CONV_HALO, D_MODEL), F32)

    @pl.when(jnp.logical_not(first))
    def _():
        sh_ref[0, 0:CONV_HALO, :] = halo_ref[...]

    sh_ref[0, CONV_HALO:, :] = u_ref[...]
    window = CONV_HALO + CONV_T
    for s in range(1, SUBLANES):
        for c0 in range(0, D_MODEL, CONV_LANES):
            lanes = slice(c0, c0 + CONV_LANES)
            sh_ref[s, 0:window - SUBLANES, lanes] = sh_ref[0, s:s + window - SUBLANES, lanes]

    base = CONV_HALO - (CONV_WIDTH - 1)
    for r0 in range(0, CONV_T, CONV_ROWS):
        for c0 in range(0, D_MODEL, CONV_LANES):
            lanes = slice(c0, c0 + CONV_LANES)
            acc = jnp.zeros((CONV_ROWS, CONV_LANES), F32)
            for j in range(CONV_WIDTH):
                a, s = divmod(base + j, SUBLANES)
                x0 = r0 + a * SUBLANES
                acc = acc + w_ref[j:j + 1, lanes] * sh_ref[s, x0:x0 + CONV_ROWS, lanes]
            y_ref[r0:r0 + CONV_ROWS, lanes] = acc + b_ref[:, lanes]

    def ln_body(c, carry):
        r0 = pl.multiple_of(c * CONV_ROWS, CONV_ROWS)
        y = y_ref[pl.ds(r0, CONV_ROWS), :]
        mu = jnp.mean(y, axis=-1, keepdims=True)
        yc = y - mu
        var = jnp.mean(yc * yc, axis=-1, keepdims=True)
        z = yc * lax.rsqrt(var + 1e-5) * g_ref[...] + beta_ref[...]
        o_ref[pl.ds(r0, CONV_ROWS), :] = (z * _sigmoid(z)).astype(o_ref.dtype)
        return carry
    lax.fori_loop(0, CONV_T // CONV_ROWS, ln_body, 0)


def _conv_ln(u, dw_w, dw_b, ln_g, ln_b, *, seq):
    m, d = u.shape
    tiles_per_seq = seq // CONV_T
    ratio = CONV_T // CONV_HALO
    row = lambda i: (0, 0)
    return pl.pallas_call(
        functools.partial(_conv_ln_kernel, tiles_per_seq=tiles_per_seq),
        grid=(m // CONV_T,),
        in_specs=[
            pl.BlockSpec((CONV_HALO, d), lambda i: (jnp.maximum(i * ratio - 1, 0), 0)),
            pl.BlockSpec((CONV_T, d), lambda i: (i, 0)),
            pl.BlockSpec((CONV_WIDTH, d), row),
            pl.BlockSpec((1, d), row), pl.BlockSpec((1, d), row), pl.BlockSpec((1, d), row),
        ],
        out_specs=pl.BlockSpec((CONV_T, d), lambda i: (i, 0)),
        out_shape=jax.ShapeDtypeStruct((m, d), BF16),
        scratch_shapes=[pltpu.VMEM((SUBLANES, CONV_HALO + CONV_T, d), F32), pltpu.VMEM((CONV_T, d), F32)],
        compiler_params=_params("parallel"),
        name="conv_ln",
    )(u, u, dw_w, dw_b, ln_g, ln_b)


LRU_T = 512
LRU_HALO = 8


def _lru_kernel(halo_ref, u_ref, y_ref, cw_ref, cb_ref, wa_ref, ba_ref, wi_ref, bi_ref, lam_ref,
                o_ref, buf_ref, carry_ref):
    ti = pl.program_id(2)
    first = ti == 0
    buf_ref[LRU_HALO:, :] = u_ref[...]

    @pl.when(first)
    def _():
        buf_ref[0:LRU_HALO, :] = jnp.zeros((LRU_HALO, LRU_BLOCK_SIZE), F32)
        carry_ref[...] = jnp.zeros(carry_ref.shape, F32)

    @pl.when(jnp.logical_not(first))
    def _():
        buf_ref[0:LRU_HALO, :] = halo_ref[...]

    base = LRU_HALO - (LRU_CONV_WIDTH - 1)
    uc = jnp.zeros((LRU_T, LRU_BLOCK_SIZE), F32)
    for j in range(LRU_CONV_WIDTH):
        uc = uc + cw_ref[j:j + 1, :] * buf_ref[base + j:base + j + LRU_T, :]
    uc = uc + cb_ref[...]

    ucb = uc.astype(BF16)
    r = _sigmoid(jnp.dot(ucb, wa_ref[0].astype(BF16), preferred_element_type=F32) + ba_ref[...])
    gate_i = _sigmoid(jnp.dot(ucb, wi_ref[0].astype(BF16), preferred_element_type=F32) + bi_ref[...])
    neg_lam = -lam_ref[...]
    softplus = jnp.maximum(neg_lam, 0.0) + jnp.log1p(jnp.exp(-jnp.abs(neg_lam)))
    log_a = -LRU_C * r * softplus
    a = jnp.exp(log_a)
    mult = jnp.sqrt(-jnp.tanh(log_a) * (1.0 + a * a))
    b = mult * (gate_i * uc)

    row = lax.broadcasted_iota(jnp.int32, (LRU_T, LRU_BLOCK_SIZE), 0)
    s = 1
    while s < SUBLANES:
        valid = row >= s
        a_sh = jnp.where(valid, pltpu.roll(a, s, 0), 1.0)
        b_sh = jnp.where(valid, pltpu.roll(b, s, 0), 0.0)
        b = a * b_sh + b
        a = a * a_sh
        s *= 2
    while s < LRU_T:
        b = jnp.concatenate([b[:s], a[s:] * b[:-s] + b[s:]], axis=0)
        a = jnp.concatenate([a[:s], a[s:] * a[:-s]], axis=0)
        s *= 2
    hs = a * carry_ref[0:1, :] + b
    carry_ref[0:1, :] = hs[LRU_T - 1:LRU_T, :]
    o_ref[...] = (hs * y_ref[...]).astype(o_ref.dtype)


def _lru_core(u, y, conv_w, conv_b, w_a, b_a, w_i, b_i, lam, *, batch, seq):
    m, d = u.shape
    nt = seq // LRU_T
    ratio = LRU_T // LRU_HALO
    c = LRU_BLOCK_SIZE
    chan = lambda b, g, t: (0, g)
    tile = lambda b, g, t: (b * nt + t, g)
    gate_w = pl.BlockSpec((1, c, c), lambda b, g, t: (g, 0, 0))
    return pl.pallas_call(
        _lru_kernel,
        grid=(batch, LRU_BLOCKS, nt),
        in_specs=[
            pl.BlockSpec((LRU_HALO, c), lambda b, g, t: (jnp.maximum((b * nt + t) * ratio - 1, 0), g)),
            pl.BlockSpec((LRU_T, c), tile),
            pl.BlockSpec((LRU_T, c), tile),
            pl.BlockSpec((LRU_CONV_WIDTH, c), chan),
            pl.BlockSpec((1, c), chan),
            gate_w, pl.BlockSpec((1, c), chan),
            gate_w, pl.BlockSpec((1, c), chan),
            pl.BlockSpec((1, c), chan),
        ],
        out_specs=pl.BlockSpec((LRU_T, c), tile),
        out_shape=jax.ShapeDtypeStruct((m, d), BF16),
        scratch_shapes=[pltpu.VMEM((LRU_HALO + LRU_T, c), F32), pltpu.VMEM((8, c), F32)],
        compiler_params=_params("parallel", "parallel", "arbitrary"),
        name="lru_core",
    )(u, u, y, conv_w, conv_b, w_a, b_a, w_i, b_i, lam)


def kernel(x, rel_bias, mixer_norm_g, attn_w_qkv, attn_lq1, attn_lk1, attn_lq2, attn_lk2, attn_subln_g, attn_w_o, conv_w_in, conv_b_in, conv_dw_w, conv_dw_b, conv_ln_g, conv_ln_b, conv_w_out, conv_b_out, lru_w_in, lru_conv_w, lru_conv_b, lru_w_a, lru_b_a, lru_w_i, lru_b_i, lru_lambda, lru_w_out, mlp_norm_g, mlp_w1, mlp_w2, final_norm_g):
    batch, seq, d = x.shape
    h = x.reshape(batch * seq, d)
    row = lambda v: v.reshape(1, -1)
    zero_bias = jnp.zeros((1, d), F32)
    q_scale = jnp.concatenate([jnp.full((1, d), HEAD_DIM ** -0.5 * LOG2E, F32), jnp.ones((1, 2 * d), F32)], axis=1)
    bias_tiles = _bias_tiles(rel_bias, ATTN_T)

    ia = ic = il = 0
    for layer in range(DEPTH):
        kind = layer % N_MIXERS
        g = row(mixer_norm_g[layer])
        if kind == 0:
            lambda_init = 0.8 - 0.6 * math.exp(-0.3 * layer)
            qkv = _norm_proj(h, g, attn_w_qkv, ia, q_scale)
            outs = _diff_attention(qkv, bias_tiles, row(attn_lq1[ia]), row(attn_lk1[ia]),
                                   row(attn_lq2[ia]), row(attn_lk2[ia]), row(attn_subln_g[ia]),
                                   (mlp_w1, mlp_w2), 0 if layer == 0 else None,
                                   batch=batch, seq=seq, lambda_init=lambda_init)
            if layer == 0:
                w1, w2 = outs[1:]
            h = _proj_res(outs[0], attn_w_o, ia, zero_bias, h)
            ia += 1
        elif kind == 1:
            u = _norm_glu(h, g, conv_w_in, ic, row(conv_b_in[ic]))
            act = _conv_ln(u, conv_dw_w[ic], row(conv_dw_b[ic]), row(conv_ln_g[ic]), row(conv_ln_b[ic]), seq=seq)
            h = _proj_res(act, conv_w_out, ic, row(conv_b_out[ic]), h)
            ic += 1
        else:
            u, y = _norm_lru_in(h, g, lru_w_in, il)
            hy = _lru_core(u, y, lru_conv_w[il], row(lru_conv_b[il]), lru_w_a[il], row(lru_b_a[il]),
                           lru_w_i[il], row(lru_b_i[il]), row(lru_lambda[il]), batch=batch, seq=seq)
            h = _proj_res(hy, lru_w_out, il, zero_bias, h)
            il += 1
        last = layer == DEPTH - 1
        outs = _mlp(h, row(mlp_norm_g[layer]), w1, w2, row(final_norm_g) if last else None,
                    (mlp_w1, mlp_w2), None if last else layer + 1)
        h = outs[0]
        if not last:
            w1, w2 = outs[1:]
    return h.reshape(batch, seq, d)
```

```python
import functools
import math

import numpy as np
import jax
import jax.numpy as jnp
from jax import lax
from jax.experimental import pallas as pl
from jax.experimental.pallas import tpu as pltpu

F32 = jnp.float32
BF16 = jnp.bfloat16

D_MODEL = 2048
DEPTH = 4
N_MIXERS = 3
HEAD_DIM = 128
HEAD_W = 2 * HEAD_DIM
N_HEADS = D_MODEL // HEAD_W
NUM_BUCKETS = 32
MAX_DISTANCE = 128
MASK_VALUE = -1e30
CONV_WIDTH = 31
LRU_BLOCKS = 8
LRU_BLOCK_SIZE = D_MODEL // LRU_BLOCKS
LRU_CONV_WIDTH = 4
LRU_C = 8.0
D_FF = 4 * D_MODEL

VMEM_LIMIT = 56 * 1024 * 1024
NORM_ROWS = 128
LANES = 128
SUBLANES = 8
BF16_ROWS = 16


def _params(*sem):
    return pltpu.CompilerParams(dimension_semantics=sem, vmem_limit_bytes=VMEM_LIMIT)


def _rms_rows_to(dst_ref, x_ref, g_ref, rows, eps):
    def body(c, carry):
        r0 = pl.multiple_of(c * NORM_ROWS, NORM_ROWS)
        x = x_ref[pl.ds(r0, NORM_ROWS), :]
        ms = jnp.mean(x * x, axis=-1, keepdims=True)
        dst_ref[pl.ds(r0, NORM_ROWS), :] = (x * lax.rsqrt(ms + eps) * g_ref[...]).astype(BF16)
        return carry
    lax.fori_loop(0, rows // NORM_ROWS, body, 0)


def _mlp_cast_specs(grid, layer):
    d, ff = D_MODEL, D_FF
    steps = math.prod(grid)

    def lin(*idx):
        n = 0
        for i, g in zip(idx, grid):
            n = n * g + i
        return n

    split = max(1, BF16_ROWS * steps // d)
    r1, c1, r2 = d * split // steps, ff // split, ff // steps
    in_specs = [pl.BlockSpec((None, r1, c1), lambda *idx: (layer, lin(*idx) // split, lin(*idx) % split)),
                pl.BlockSpec((None, r2, d), lambda *idx: (layer, lin(*idx), 0))]
    out_specs = [pl.BlockSpec((r1, c1), lambda *idx: (lin(*idx) // split, lin(*idx) % split)),
                 pl.BlockSpec((r2, d), lambda *idx: (lin(*idx), 0))]
    out_shapes = [jax.ShapeDtypeStruct((d, ff), BF16), jax.ShapeDtypeStruct((ff, d), BF16)]
    return in_specs, out_specs, out_shapes


def _cast_slabs(w1_ref, w2_ref, w1b_ref, w2b_ref):
    w1b_ref[...] = w1_ref[...].astype(BF16)
    w2b_ref[...] = w2_ref[...].astype(BF16)


def _sigmoid(x):
    return 0.5 * (jnp.tanh(0.5 * x) + 1.0)


def _gelu_tanh(x):
    c = math.sqrt(2.0 / math.pi)
    return 0.5 * x * (1.0 + jnp.tanh(c * (x + 0.044715 * (x * x * x))))


def _norm_proj_kernel(x_ref, g_ref, w_ref, s_ref, o_ref, xn_ref, *, tm):
    @pl.when(pl.program_id(1) == 0)
    def _():
        _rms_rows_to(xn_ref, x_ref, g_ref, tm, 1e-6)
    acc = jnp.dot(xn_ref[...], w_ref[...].astype(BF16), preferred_element_type=F32)
    o_ref[...] = (acc * s_ref[...]).astype(o_ref.dtype)


def _norm_proj(x, g, w, layer, col_scale, *, tm=1024, tn=1024):
    m, k = x.shape
    n = w.shape[2]
    return pl.pallas_call(
        functools.partial(_norm_proj_kernel, tm=tm),
        grid=(m // tm, n // tn),
        in_specs=[
            pl.BlockSpec((tm, k), lambda i, j: (i, 0)),
            pl.BlockSpec((1, k), lambda i, j: (0, 0)),
            pl.BlockSpec((None, k, tn), lambda i, j: (layer, 0, j)),
            pl.BlockSpec((1, tn), lambda i, j: (0, j)),
        ],
        out_specs=pl.BlockSpec((tm, tn), lambda i, j: (i, j)),
        out_shape=jax.ShapeDtypeStruct((m, n), BF16),
        scratch_shapes=[pltpu.VMEM((tm, k), BF16)],
        compiler_params=_params("parallel", "arbitrary"),
        name="norm_proj",
    )(x, g, w, col_scale)


def _norm_glu_kernel(x_ref, g_ref, wa_ref, wg_ref, ba_ref, bg_ref, o_ref, xn_ref, *, tm):
    @pl.when(pl.program_id(1) == 0)
    def _():
        _rms_rows_to(xn_ref, x_ref, g_ref, tm, 1e-6)
    xn = xn_ref[...]
    a = jnp.dot(xn, wa_ref[...].astype(BF16), preferred_element_type=F32) + ba_ref[...]
    gate = jnp.dot(xn, wg_ref[...].astype(BF16), preferred_element_type=F32) + bg_ref[...]
    o_ref[...] = a * _sigmoid(gate)


def _norm_glu(x, g, w, layer, b, *, tm=1024, tn=512):
    m, k = x.shape
    half = w.shape[2] // 2
    nj = half // tn
    return pl.pallas_call(
        functools.partial(_norm_glu_kernel, tm=tm),
        grid=(m // tm, nj),
        in_specs=[
            pl.BlockSpec((tm, k), lambda i, j: (i, 0)),
            pl.BlockSpec((1, k), lambda i, j: (0, 0)),
            pl.BlockSpec((None, k, tn), lambda i, j: (layer, 0, j)),
            pl.BlockSpec((None, k, tn), lambda i, j: (layer, 0, j + nj)),
            pl.BlockSpec((1, tn), lambda i, j: (0, j)),
            pl.BlockSpec((1, tn), lambda i, j: (0, j + nj)),
        ],
        out_specs=pl.BlockSpec((tm, tn), lambda i, j: (i, j)),
        out_shape=jax.ShapeDtypeStruct((m, half), F32),
        scratch_shapes=[pltpu.VMEM((tm, k), BF16)],
        compiler_params=_params("parallel", "arbitrary"),
        name="norm_glu",
    )(x, g, w, w, b, b)


def _norm_lru_in_kernel(x_ref, g_ref, wu_ref, wy_ref, u_ref, y_ref, xn_ref, *, tm):
    @pl.when(pl.program_id(1) == 0)
    def _():
        _rms_rows_to(xn_ref, x_ref, g_ref, tm, 1e-6)
    xn = xn_ref[...]
    u_ref[...] = jnp.dot(xn, wu_ref[...].astype(BF16), preferred_element_type=F32)
    y_ref[...] = _gelu_tanh(jnp.dot(xn, wy_ref[...].astype(BF16), preferred_element_type=F32))


def _norm_lru_in(x, g, w, layer, *, tm=1024, tn=512):
    m, k = x.shape
    half = w.shape[2] // 2
    nj = half // tn
    return pl.pallas_call(
        functools.partial(_norm_lru_in_kernel, tm=tm),
        grid=(m // tm, nj),
        in_specs=[
            pl.BlockSpec((tm, k), lambda i, j: (i, 0)),
            pl.BlockSpec((1, k), lambda i, j: (0, 0)),
            pl.BlockSpec((None, k, tn), lambda i, j: (layer, 0, j)),
            pl.BlockSpec((None, k, tn), lambda i, j: (layer, 0, j + nj)),
        ],
        out_specs=[pl.BlockSpec((tm, tn), lambda i, j: (i, j)),
                   pl.BlockSpec((tm, tn), lambda i, j: (i, j))],
        out_shape=[jax.ShapeDtypeStruct((m, half), F32), jax.ShapeDtypeStruct((m, half), F32)],
        scratch_shapes=[pltpu.VMEM((tm, k), BF16)],
        compiler_params=_params("parallel", "arbitrary"),
        name="norm_lru_in",
    )(x, g, w, w)


def _proj_res_kernel(a_ref, w_ref, b_ref, r_ref, o_ref, wb_ref):
    @pl.when(pl.program_id(1) == 0)
    def _():
        wb_ref[...] = w_ref[...].astype(BF16)
    acc = jnp.dot(a_ref[...], wb_ref[...], preferred_element_type=F32)
    o_ref[...] = r_ref[...] + (acc + b_ref[...])


def _proj_res(a, w, layer, b, res, *, tm=512, tn=2048):
    m, k = a.shape
    n = w.shape[2]
    return pl.pallas_call(
        _proj_res_kernel,
        grid=(n // tn, m // tm),
        in_specs=[
            pl.BlockSpec((tm, k), lambda j, i: (i, 0)),
            pl.BlockSpec((None, k, tn), lambda j, i: (layer, 0, j), pipeline_mode=pl.Buffered(1)),
            pl.BlockSpec((1, tn), lambda j, i: (0, j)),
            pl.BlockSpec((tm, tn), lambda j, i: (i, j)),
        ],
        out_specs=pl.BlockSpec((tm, tn), lambda j, i: (i, j)),
        out_shape=jax.ShapeDtypeStruct((m, n), F32),
        scratch_shapes=[pltpu.VMEM((k, tn), BF16)],
        compiler_params=_params("parallel", "arbitrary"),
        name="proj_res",
    )(a, w, b, res)


def _mlp_kernel(*refs, tm, final_norm, cast):
    refs = list(refs)
    x_ref, g_ref, w1_ref, w2_ref = refs[:4]
    del refs[:4]
    if final_norm:
        fg_ref = refs.pop(0)
    if cast:
        _cast_slabs(refs[0], refs[1], refs[3], refs[4])
        o_ref = refs[2]
    else:
        o_ref = refs[0]
    xn_ref = refs[-1]
    f = pl.program_id(1)

    @pl.when(f == 0)
    def _():
        _rms_rows_to(xn_ref, x_ref, g_ref, tm, 1e-6)
        o_ref[...] = x_ref[...]

    z = jnp.maximum(jnp.dot(xn_ref[...], w1_ref[...], preferred_element_type=F32), 0.0)
    o_ref[...] += jnp.dot((z * z).astype(BF16), w2_ref[...], preferred_element_type=F32)

    if final_norm:
        @pl.when(f == pl.num_programs(1) - 1)
        def _():
            def body(c, carry):
                rows = pl.ds(pl.multiple_of(c * NORM_ROWS, NORM_ROWS), NORM_ROWS)
                y = o_ref[rows, :]
                ms = jnp.mean(y * y, axis=-1, keepdims=True)
                o_ref[rows, :] = y * lax.rsqrt(ms + 1e-6) * fg_ref[...]
                return carry
            lax.fori_loop(0, tm // NORM_ROWS, body, 0)


def _mlp(x, g, w1, w2, final_g=None, mlp_weights=None, cast_layer=None, *, tm=512, tf=1024):
    m, d = x.shape
    ff = w1.shape[1]
    final_norm = final_g is not None
    cast = cast_layer is not None
    grid = (m // tm, ff // tf)
    row_spec = pl.BlockSpec((1, d), lambda i, f: (0, 0))
    cast_in, cast_out, cast_shapes = _mlp_cast_specs(grid, cast_layer) if cast else ([], [], [])
    return pl.pallas_call(
        functools.partial(_mlp_kernel, tm=tm, final_norm=final_norm, cast=cast),
        grid=grid,
        in_specs=[
            pl.BlockSpec((tm, d), lambda i, f: (i, 0)),
            row_spec,
            pl.BlockSpec((d, tf), lambda i, f: (0, f)),
            pl.BlockSpec((tf, d), lambda i, f: (f, 0)),
        ] + ([row_spec] if final_norm else []) + cast_in,
        out_specs=[pl.BlockSpec((tm, d), lambda i, f: (i, 0))] + cast_out,
        out_shape=[jax.ShapeDtypeStruct((m, d), F32)] + cast_shapes,
        scratch_shapes=[pltpu.VMEM((tm, d), BF16)],
        compiler_params=_params("parallel", "arbitrary"),
        name="mlp",
    )(x, g, w1, w2, *((final_g,) if final_norm else ()), *(mlp_weights if cast else ()))


ATTN_T = 512
ATTN_STRIP = 32
ATTN_Q_PER_STEP = 4
LOG2E = math.log2(math.e)


def _bucket_tiles(t):
    r = np.arange(t, dtype=np.int32)[:, None]
    c = np.arange(t, dtype=np.int32)[None, :]
    tiles = []
    for d in range(2):
        n = np.maximum(d * t + r - c, 0)
        max_exact = NUM_BUCKETS // 2
        nf = np.maximum(n, 1).astype(np.float32)
        large = max_exact + (np.log(nf / np.float32(max_exact)) / np.float32(math.log(MAX_DISTANCE / max_exact))
                             * np.float32(NUM_BUCKETS - max_exact)).astype(np.int32)
        large = np.minimum(large, NUM_BUCKETS - 1)
        bucket = np.where(n < max_exact, n, large).astype(np.int32)
        tiles.append(np.where(d * t + r - c >= 0, bucket, -1))
    return np.stack(tiles)


def _bias_needed(t):
    buckets = _bucket_tiles(t)
    last = NUM_BUCKETS - 1
    return [[[bool((buckets[d, r:r + ATTN_STRIP, c:c + LANES] != last).any())
              for c in range(0, t, LANES)] for r in range(0, t, ATTN_STRIP)] for d in range(2)]


def _bias_tiles_kernel(rb_ref, bucket_ref, o_ref):
    h = pl.program_id(0)
    bucket = bucket_ref[0]
    acc = jnp.zeros(bucket.shape, F32)
    for j in range(NUM_BUCKETS):
        acc = jnp.where(bucket == j, rb_ref[j, h], acc)
    acc = (acc - rb_ref[NUM_BUCKETS - 1, h]) * LOG2E
    o_ref[0, 0] = jnp.where(bucket < 0, MASK_VALUE, acc)


def _bias_tiles(rel_bias, t):
    buckets = jnp.asarray(_bucket_tiles(t))
    return pl.pallas_call(
        _bias_tiles_kernel,
        grid=(N_HEADS, 2),
        in_specs=[
            pl.BlockSpec(memory_space=pltpu.SMEM),
            pl.BlockSpec((1, t, t), lambda h, d: (d, 0, 0)),
        ],
        out_specs=pl.BlockSpec((1, 1, t, t), lambda h, d: (h, d, 0, 0)),
        out_shape=jax.ShapeDtypeStruct((N_HEADS, 2, t, t), F32),
        compiler_params=_params("parallel", "parallel"),
        name="bias_tiles",
    )(rel_bias, buckets)


def _attn_kernel(*refs, t, lambda_init, cast, q_per_step):
    lq1_ref, lk1_ref, lq2_ref, lk2_ref, sg_ref, q_ref, k_ref, v_ref, bias_ref = refs[:9]
    if cast:
        w1_ref, w2_ref, o_ref, w1b_ref, w2b_ref = refs[9:14]
        _cast_slabs(w1_ref, w2_ref, w1b_ref, w2b_ref)
    else:
        o_ref = refs[9]
    m_ref, l_ref, alpha_ref, acc_ref, s_ref, p_ref = refs[-6:]
    chunks = [slice(j * LANES, (j + 1) * LANES) for j in range(t // LANES)]
    bias_needed = _bias_needed(t)

    def start(ki):
        return pl.multiple_of(jnp.maximum(ki, 0) * t, t)

    def qk_logits(jq, ki, slot):
        for c in range(2):
            lanes = slice(c * HEAD_DIM, (c + 1) * HEAD_DIM)
            s_ref[slot, c] = lax.dot_general(q_ref[pl.ds(start(jq), t), lanes], k_ref[pl.ds(start(ki), t), lanes],
                                             (((1,), (1,)), ((), ())), preferred_element_type=F32)

    def accumulate(ki, slot):
        v = v_ref[pl.ds(start(ki), t), :]
        for c in range(2):
            alpha = alpha_ref[slot, c]
            pv = jnp.dot(p_ref[slot, c], v, preferred_element_type=F32)
            acc_ref[c] = jnp.concatenate([alpha, alpha], axis=1) * acc_ref[c] + pv

    strips = [slice(r * ATTN_STRIP, (r + 1) * ATTN_STRIP) for r in range(t // ATTN_STRIP)]

    def block(jq, ki, cur, bias_index, has_next):
        other = 1 - cur

        def logits(c, rows, ch):
            s = s_ref[cur, c, rows, ch]
            if bias_index is not None and bias_needed[bias_index][rows.start // ATTN_STRIP][ch.start // LANES]:
                s = s + bias_ref[0, bias_index, rows, ch]
            return s

        part_max = [[], []]
        for rows in strips:
            for c in range(2):
                part = logits(c, rows, chunks[0])
                for ch in chunks[1:]:
                    part = jnp.maximum(part, logits(c, rows, ch))
                part_max[c].append(part)

        accumulate(ki - 1, other)
        m_new, alpha = [], []
        for c in range(2):
            m_prev = m_ref[c]
            m_c = jnp.maximum(m_prev, jnp.max(jnp.concatenate(part_max[c], axis=0), axis=1, keepdims=True))
            alpha.append(jnp.exp2(m_prev - m_c))
            alpha_ref[cur, c] = alpha[c]
            m_ref[c] = m_c
            m_new.append(m_c)

        part_sum = [[], []]
        for rows in strips:
            for c in range(2):
                m_rows = m_new[c][rows, :]
                part = None
                for ch in chunks:
                    p = jnp.exp2(logits(c, rows, ch) - m_rows)
                    part = p if part is None else part + p
                    p_ref[cur, c, rows, ch] = p.astype(BF16)
                part_sum[c].append(part)

        if has_next:
            qk_logits(jq, ki + 1, other)
        for c in range(2):
            row_sum = jnp.sum(jnp.concatenate(part_sum[c], axis=0), axis=1, keepdims=True)
            l_ref[c] = alpha[c] * l_ref[c] + row_sum

    lam = (jnp.exp(jnp.sum(lq1_ref[...] * lk1_ref[...], keepdims=True))
           - jnp.exp(jnp.sum(lq2_ref[...] * lk2_ref[...], keepdims=True)) + lambda_init)

    def q_block(jq, carry):
        qi = pl.program_id(2) * q_per_step + jq

        m_ref[...] = jnp.full(m_ref.shape, MASK_VALUE, F32)
        l_ref[...] = jnp.zeros(l_ref.shape, F32)
        acc_ref[...] = jnp.zeros(acc_ref.shape, F32)
        alpha_ref[1] = jnp.zeros(alpha_ref.shape[1:], F32)
        p_ref[1] = jnp.zeros(p_ref.shape[1:], BF16)

        n_far = jnp.maximum(qi - 1, 0)

        def far_pair(j, carry):
            block(jq, 2 * j, 0, None, True)
            block(jq, 2 * j + 1, 1, None, True)
            return carry
        lax.fori_loop(0, n_far // 2, far_pair, 0)

        @pl.when(qi % 2 == 1)
        def _():
            block(jq, qi - 1, 0, 1, True)
            block(jq, qi, 1, 0, False)
            accumulate(qi, 1)

        @pl.when(qi % 2 == 0)
        def _():
            @pl.when(qi >= 2)
            def _():
                block(jq, qi - 2, 0, None, True)
                block(jq, qi - 1, 1, 1, True)
            block(jq, qi, 0, 0, False)
            accumulate(qi, 0)

        qk_logits(jnp.minimum(jq + 1, q_per_step - 1), 0, 0)
        inv0 = 1.0 / l_ref[0]
        inv1 = lam / l_ref[1]
        o = (acc_ref[0] * jnp.concatenate([inv0, inv0], axis=1)
             - acc_ref[1] * jnp.concatenate([inv1, inv1], axis=1))
        ms = jnp.mean(o * o, axis=-1, keepdims=True)
        o = o * lax.rsqrt(ms + 1e-5) * sg_ref[...] * (1.0 - lambda_init)
        o_ref[pl.ds(start(jq), t), :] = o.astype(o_ref.dtype)
        return carry

    qk_logits(0, 0, 0)
    lax.fori_loop(0, q_per_step, q_block, 0)


def _diff_attention(qkv, bias_tiles, lq1, lk1, lq2, lk2, subln_g, mlp_weights=None, cast_layer=None,
                    *, batch, seq, lambda_init):
    t = ATTN_T
    qps = ATTN_Q_PER_STEP if cast_layer is not None else seq // t
    groups = seq // (t * qps)
    grid = (batch, N_HEADS, groups)
    lam_spec = pl.BlockSpec((1, HEAD_DIM), lambda b, h, i: (0, 0))
    cast = cast_layer is not None
    cast_in, cast_out, cast_shapes = _mlp_cast_specs(grid, cast_layer) if cast else ([], [], [])
    return pl.pallas_call(
        functools.partial(_attn_kernel, t=t, lambda_init=lambda_init, cast=cast, q_per_step=qps),
        grid=grid,
        in_specs=[
            lam_spec, lam_spec, lam_spec, lam_spec,
            pl.BlockSpec((1, HEAD_W), lambda b, h, i: (0, 0)),
            pl.BlockSpec((qps * t, HEAD_W), lambda b, h, i: (b * groups + i, h)),
            pl.BlockSpec((seq, HEAD_W), lambda b, h, i: (b, N_HEADS + h)),
            pl.BlockSpec((seq, HEAD_W), lambda b, h, i: (b, 2 * N_HEADS + h)),
            pl.BlockSpec((1, 2, t, t), lambda b, h, i: (h, 0, 0, 0)),
        ] + cast_in,
        out_specs=[pl.BlockSpec((qps * t, HEAD_W), lambda b, h, i: (b * groups + i, h))] + cast_out,
        out_shape=[jax.ShapeDtypeStruct((batch * seq, D_MODEL), BF16)] + cast_shapes,
        scratch_shapes=[pltpu.VMEM((2, t, LANES), F32), pltpu.VMEM((2, t, LANES), F32),
                        pltpu.VMEM((2, 2, t, LANES), F32),
                        pltpu.VMEM((2, t, HEAD_W), F32), pltpu.VMEM((2, 2, t, t), F32),
                        pltpu.VMEM((2, 2, t, t), BF16)],
        compiler_params=_params("parallel", "parallel", "arbitrary"),
        name="diff_attn",
    )(lq1, lk1, lq2, lk2, subln_g, qkv, qkv, qkv, bias_tiles, *(mlp_weights if cast else ()))


CONV_T = 256
CONV_HALO = 32
CONV_ROWS = 64
CONV_LANES = 256


def _conv_ln_kernel(halo_ref, u_ref, w_ref, b_ref, g_ref, beta_ref, o_ref, sh_ref, y_ref, *, tiles_per_seq):
    i = pl.program_id(0)
    first = (i % tiles_per_seq) == 0

    @pl.when(first)
    def _():
        sh_ref[0, 0:CONV_HALO, :] = jnp.zeros((CONV_HALO, D_MODEL), F32)

    @pl.when(jnp.logical_not(first))
    def _():
        sh_ref[0, 0:CONV_HALO, :] = halo_ref[...]

    sh_ref[0, CONV_HALO:, :] = u_ref[...]
    window = CONV_HALO + CONV_T
    for s in range(1, SUBLANES):
        for c0 in range(0, D_MODEL, CONV_LANES):
            lanes = slice(c0, c0 + CONV_LANES)
            sh_ref[s, 0:window - SUBLANES, lanes] = sh_ref[0, s:s + window - SUBLANES, lanes]

    base = CONV_HALO - (CONV_WIDTH - 1)
    for r0 in range(0, CONV_T, CONV_ROWS):
        for c0 in range(0, D_MODEL, CONV_LANES):
            lanes = slice(c0, c0 + CONV_LANES)
            acc = jnp.zeros((CONV_ROWS, CONV_LANES), F32)
            for j in range(CONV_WIDTH):
                a, s = divmod(base + j, SUBLANES)
                x0 = r0 + a * SUBLANES
                acc = acc + w_ref[j:j + 1, lanes] * sh_ref[s, x0:x0 + CONV_ROWS, lanes]
            y_ref[r0:r0 + CONV_ROWS, lanes] = acc + b_ref[:, lanes]

    def ln_body(c, carry):
        r0 = pl.multiple_of(c * CONV_ROWS, CONV_ROWS)
        y = y_ref[pl.ds(r0, CONV_ROWS), :]
        mu = jnp.mean(y, axis=-1, keepdims=True)
        yc = y - mu
        var = jnp.mean(yc * yc, axis=-1, keepdims=True)
        z = yc * lax.rsqrt(var + 1e-5) * g_ref[...] + beta_ref[...]
        o_ref[pl.ds(r0, CONV_ROWS), :] = (z * _sigmoid(z)).astype(o_ref.dtype)
        return carry
    lax.fori_loop(0, CONV_T // CONV_ROWS, ln_body, 0)


def _conv_ln(u, dw_w, dw_b, ln_g, ln_b, *, seq):
    m, d = u.shape
    tiles_per_seq = seq // CONV_T
    ratio = CONV_T // CONV_HALO
    row = lambda i: (0, 0)
    return pl.pallas_call(
        functools.partial(_conv_ln_kernel, tiles_per_seq=tiles_per_seq),
        grid=(m // CONV_T,),
        in_specs=[
            pl.BlockSpec((CONV_HALO, d), lambda i: (jnp.maximum(i * ratio - 1, 0), 0)),
            pl.BlockSpec((CONV_T, d), lambda i: (i, 0)),
            pl.BlockSpec((CONV_WIDTH, d), row),
            pl.BlockSpec((1, d), row), pl.BlockSpec((1, d), row), pl.BlockSpec((1, d), row),
        ],
        out_specs=pl.BlockSpec((CONV_T, d), lambda i: (i, 0)),
        out_shape=jax.ShapeDtypeStruct((m, d), BF16),
        scratch_shapes=[pltpu.VMEM((SUBLANES, CONV_HALO + CONV_T, d), F32), pltpu.VMEM((CONV_T, d), F32)],
        compiler_params=_params("parallel"),
        name="conv_ln",
    )(u, u, dw_w, dw_b, ln_g, ln_b)


LRU_T = 1024
LRU_HALO = 8


def _lru_kernel(halo_ref, u_ref, y_ref, cw_ref, cb_ref, wa_ref, ba_ref, wi_ref, bi_ref, lam_ref,
                o_ref, buf_ref, carry_ref):
    ti = pl.program_id(2)
    first = ti == 0
    buf_ref[LRU_HALO:, :] = u_ref[...]

    @pl.when(first)
    def _():
        buf_ref[0:LRU_HALO, :] = jnp.zeros((LRU_HALO, LRU_BLOCK_SIZE), F32)
        carry_ref[...] = jnp.zeros(carry_ref.shape, F32)

    @pl.when(jnp.logical_not(first))
    def _():
        buf_ref[0:LRU_HALO, :] = halo_ref[...]

    base = LRU_HALO - (LRU_CONV_WIDTH - 1)
    uc = jnp.zeros((LRU_T, LRU_BLOCK_SIZE), F32)
    for j in range(LRU_CONV_WIDTH):
        uc = uc + cw_ref[j:j + 1, :] * buf_ref[base + j:base + j + LRU_T, :]
    uc = uc + cb_ref[...]

    ucb = uc.astype(BF16)
    r = _sigmoid(jnp.dot(ucb, wa_ref[0].astype(BF16), preferred_element_type=F32) + ba_ref[...])
    gate_i = _sigmoid(jnp.dot(ucb, wi_ref[0].astype(BF16), preferred_element_type=F32) + bi_ref[...])
    neg_lam = -lam_ref[...]
    softplus = jnp.maximum(neg_lam, 0.0) + jnp.log1p(jnp.exp(-jnp.abs(neg_lam)))
    log_a = -LRU_C * r * softplus
    a = jnp.exp(log_a)
    mult = jnp.sqrt(-jnp.tanh(log_a) * (1.0 + a * a))
    b = mult * (gate_i * uc)

    row = lax.broadcasted_iota(jnp.int32, (LRU_T, LRU_BLOCK_SIZE), 0)
    s = 1
    while s < SUBLANES:
        valid = row >= s
        a_sh = jnp.where(valid, pltpu.roll(a, s, 0), 1.0)
        b_sh = jnp.where(valid, pltpu.roll(b, s, 0), 0.0)
        b = a * b_sh + b
        a = a * a_sh
        s *= 2
    while s < LRU_T:
        b = jnp.concatenate([b[:s], a[s:] * b[:-s] + b[s:]], axis=0)
        a = jnp.concatenate([a[:s], a[s:] * a[:-s]], axis=0)
        s *= 2
    hs = a * carry_ref[0:1, :] + b
    carry_ref[0:1, :] = hs[LRU_T - 1:LRU_T, :]
    o_ref[...] = (hs * y_ref[...]).astype(o_ref.dtype)


def _lru_core(u, y, conv_w, conv_b, w_a, b_a, w_i, b_i, lam, *, batch, seq):
    m, d = u.shape
    nt = seq // LRU_T
    ratio = LRU_T // LRU_HALO
    c = LRU_BLOCK_SIZE
    chan = lambda b, g, t: (0, g)
    tile = lambda b, g, t: (b * nt + t, g)
    gate_w = pl.BlockSpec((1, c, c), lambda b, g, t: (g, 0, 0))
    return pl.pallas_call(
        _lru_kernel,
        grid=(batch, LRU_BLOCKS, nt),
        in_specs=[
            pl.BlockSpec((LRU_HALO, c), lambda b, g, t: (jnp.maximum((b * nt + t) * ratio - 1, 0), g)),
            pl.BlockSpec((LRU_T, c), tile),
            pl.BlockSpec((LRU_T, c), tile),
            pl.BlockSpec((LRU_CONV_WIDTH, c), chan),
            pl.BlockSpec((1, c), chan),
            gate_w, pl.BlockSpec((1, c), chan),
            gate_w, pl.BlockSpec((1, c), chan),
            pl.BlockSpec((1, c), chan),
        ],
        out_specs=pl.BlockSpec((LRU_T, c), tile),
        out_shape=jax.ShapeDtypeStruct((m, d), BF16),
        scratch_shapes=[pltpu.VMEM((LRU_HALO + LRU_T, c), F32), pltpu.VMEM((8, c), F32)],
        compiler_params=_params("parallel", "parallel", "arbitrary"),
        name="lru_core",
    )(u, u, y, conv_w, conv_b, w_a, b_a, w_i, b_i, lam)


def kernel(x, rel_bias, mixer_norm_g, attn_w_qkv, attn_lq1, attn_lk1, attn_lq2, attn_lk2, attn_subln_g, attn_w_o, conv_w_in, conv_b_in, conv_dw_w, conv_dw_b, conv_ln_g, conv_ln_b, conv_w_out, conv_b_out, lru_w_in, lru_conv_w, lru_conv_b, lru_w_a, lru_b_a, lru_w_i, lru_b_i, lru_lambda, lru_w_out, mlp_norm_g, mlp_w1, mlp_w2, final_norm_g):
    batch, seq, d = x.shape
    h = x.reshape(batch * seq, d)
    row = lambda v: v.reshape(1, -1)
    zero_bias = jnp.zeros((1, d), F32)
    q_scale = jnp.concatenate([jnp.full((1, d), HEAD_DIM ** -0.5 * LOG2E, F32), jnp.ones((1, 2 * d), F32)], axis=1)
    bias_tiles = _bias_tiles(rel_bias, ATTN_T)

    ia = ic = il = 0
    for layer in range(DEPTH):
        kind = layer % N_MIXERS
        g = row(mixer_norm_g[layer])
        if kind == 0:
            lambda_init = 0.8 - 0.6 * math.exp(-0.3 * layer)
            qkv = _norm_proj(h, g, attn_w_qkv, ia, q_scale)
            outs = _diff_attention(qkv, bias_tiles, row(attn_lq1[ia]), row(attn_lk1[ia]),
                                   row(attn_lq2[ia]), row(attn_lk2[ia]), row(attn_subln_g[ia]),
                                   (mlp_w1, mlp_w2), 0 if layer == 0 else None,
                                   batch=batch, seq=seq, lambda_init=lambda_init)
            if layer == 0:
                w1, w2 = outs[1:]
            h = _proj_res(outs[0], attn_w_o, ia, zero_bias, h)
            ia += 1
        elif kind == 1:
            u = _norm_glu(h, g, conv_w_in, ic, row(conv_b_in[ic]))
            act = _conv_ln(u, conv_dw_w[ic], row(conv_dw_b[ic]), row(conv_ln_g[ic]), row(conv_ln_b[ic]), seq=seq)
            h = _proj_res(act, conv_w_out, ic, row(conv_b_out[ic]), h)
            ic += 1
        else:
            u, y = _norm_lru_in(h, g, lru_w_in, il)
            hy = _lru_core(u, y, lru_conv_w[il], row(lru_conv_b[il]), lru_w_a[il], row(lru_b_a[il]),
                           lru_w_i[il], row(lru_b_i[il]), row(lru_lambda[il]), batch=batch, seq=seq)
            h = _proj_res(hy, lru_w_out, il, zero_bias, h)
            il += 1
        last = layer == DEPTH - 1
        outs = _mlp(h, row(mlp_norm_g[layer]), w1, w2, row(final_norm_g) if last else None,
                    (mlp_w1, mlp_w2), None if last else layer + 1)
        h = outs[0]
        if not last:
            w1, w2 = outs[1:]
    return h.reshape(batch, seq, d)
```

```python
import functools
import math

import numpy as np
import jax
import jax.numpy as jnp
from jax import lax
from jax.experimental import pallas as pl
from jax.experimental.pallas import tpu as pltpu

F32 = jnp.float32
BF16 = jnp.bfloat16

D_MODEL = 2048
DEPTH = 4
N_MIXERS = 3
HEAD_DIM = 128
HEAD_W = 2 * HEAD_DIM
N_HEADS = D_MODEL // HEAD_W
NUM_BUCKETS = 32
MAX_DISTANCE = 128
MASK_VALUE = -1e30
CONV_WIDTH = 31
LRU_BLOCKS = 8
LRU_BLOCK_SIZE = D_MODEL // LRU_BLOCKS
LRU_CONV_WIDTH = 4
LRU_C = 8.0
D_FF = 4 * D_MODEL

VMEM_LIMIT = 56 * 1024 * 1024
NORM_ROWS = 128
LANES = 128
SUBLANES = 8
BF16_ROWS = 16


def _params(*sem):
    return pltpu.CompilerParams(dimension_semantics=sem, vmem_limit_bytes=VMEM_LIMIT)


def _rms_rows_to(dst_ref, x_ref, g_ref, rows, eps):
    def body(c, carry):
        r0 = pl.multiple_of(c * NORM_ROWS, NORM_ROWS)
        x = x_ref[pl.ds(r0, NORM_ROWS), :]
        ms = jnp.mean(x * x, axis=-1, keepdims=True)
        dst_ref[pl.ds(r0, NORM_ROWS), :] = (x * lax.rsqrt(ms + eps) * g_ref[...]).astype(BF16)
        return carry
    lax.fori_loop(0, rows // NORM_ROWS, body, 0)


def _mlp_cast_specs(grid, layer):
    d, ff = D_MODEL, D_FF
    steps = math.prod(grid)

    def lin(*idx):
        n = 0
        for i, g in zip(idx, grid):
            n = n * g + i
        return n

    split = max(1, BF16_ROWS * steps // d)
    r1, c1, r2 = d * split // steps, ff // split, ff // steps
    in_specs = [pl.BlockSpec((None, r1, c1), lambda *idx: (layer, lin(*idx) // split, lin(*idx) % split)),
                pl.BlockSpec((None, r2, d), lambda *idx: (layer, lin(*idx), 0))]
    out_specs = [pl.BlockSpec((r1, c1), lambda *idx: (lin(*idx) // split, lin(*idx) % split)),
                 pl.BlockSpec((r2, d), lambda *idx: (lin(*idx), 0))]
    out_shapes = [jax.ShapeDtypeStruct((d, ff), BF16), jax.ShapeDtypeStruct((ff, d), BF16)]
    return in_specs, out_specs, out_shapes


def _cast_slabs(w1_ref, w2_ref, w1b_ref, w2b_ref):
    w1b_ref[...] = w1_ref[...].astype(BF16)
    w2b_ref[...] = w2_ref[...].astype(BF16)


def _sigmoid(x):
    return 0.5 * (jnp.tanh(0.5 * x) + 1.0)


def _gelu_tanh(x):
    c = math.sqrt(2.0 / math.pi)
    return 0.5 * x * (1.0 + jnp.tanh(c * (x + 0.044715 * (x * x * x))))


def _norm_proj_kernel(x_ref, g_ref, w_ref, s_ref, o_ref, xn_ref, *, tm):
    @pl.when(pl.program_id(1) == 0)
    def _():
        _rms_rows_to(xn_ref, x_ref, g_ref, tm, 1e-6)
    acc = jnp.dot(xn_ref[...], w_ref[...].astype(BF16), preferred_element_type=F32)
    o_ref[...] = (acc * s_ref[...]).astype(o_ref.dtype)


def _norm_proj(x, g, w, layer, col_scale, *, tm=1024, tn=1024):
    m, k = x.shape
    n = w.shape[2]
    return pl.pallas_call(
        functools.partial(_norm_proj_kernel, tm=tm),
        grid=(m // tm, n // tn),
        in_specs=[
            pl.BlockSpec((tm, k), lambda i, j: (i, 0)),
            pl.BlockSpec((1, k), lambda i, j: (0, 0)),
            pl.BlockSpec((None, k, tn), lambda i, j: (layer, 0, j)),
            pl.BlockSpec((1, tn), lambda i, j: (0, j)),
        ],
        out_specs=pl.BlockSpec((tm, tn), lambda i, j: (i, j)),
        out_shape=jax.ShapeDtypeStruct((m, n), BF16),
        scratch_shapes=[pltpu.VMEM((tm, k), BF16)],
        compiler_params=_params("parallel", "arbitrary"),
        name="norm_proj",
    )(x, g, w, col_scale)


def _norm_glu_kernel(x_ref, g_ref, wa_ref, wg_ref, ba_ref, bg_ref, o_ref, xn_ref, *, tm):
    @pl.when(pl.program_id(1) == 0)
    def _():
        _rms_rows_to(xn_ref, x_ref, g_ref, tm, 1e-6)
    xn = xn_ref[...]
    a = jnp.dot(xn, wa_ref[...].astype(BF16), preferred_element_type=F32) + ba_ref[...]
    gate = jnp.dot(xn, wg_ref[...].astype(BF16), preferred_element_type=F32) + bg_ref[...]
    o_ref[...] = a * _sigmoid(gate)


def _norm_glu(x, g, w, layer, b, *, tm=1024, tn=512):
    m, k = x.shape
    half = w.shape[2] // 2
    nj = half // tn
    return pl.pallas_call(
        functools.partial(_norm_glu_kernel, tm=tm),
        grid=(m // tm, nj),
        in_specs=[
            pl.BlockSpec((tm, k), lambda i, j: (i, 0)),
            pl.BlockSpec((1, k), lambda i, j: (0, 0)),
            pl.BlockSpec((None, k, tn), lambda i, j: (layer, 0, j)),
            pl.BlockSpec((None, k, tn), lambda i, j: (layer, 0, j + nj)),
            pl.BlockSpec((1, tn), lambda i, j: (0, j)),
            pl.BlockSpec((1, tn), lambda i, j: (0, j + nj)),
        ],
        out_specs=pl.BlockSpec((tm, tn), lambda i, j: (i, j)),
        out_shape=jax.ShapeDtypeStruct((m, half), F32),
        scratch_shapes=[pltpu.VMEM((tm, k), BF16)],
        compiler_params=_params("parallel", "arbitrary"),
        name="norm_glu",
    )(x, g, w, w, b, b)


def _norm_lru_in_kernel(x_ref, g_ref, wu_ref, wy_ref, u_ref, y_ref, xn_ref, *, tm):
    @pl.when(pl.program_id(1) == 0)
    def _():
        _rms_rows_to(xn_ref, x_ref, g_ref, tm, 1e-6)
    xn = xn_ref[...]
    u_ref[...] = jnp.dot(xn, wu_ref[...].astype(BF16), preferred_element_type=F32)
    y_ref[...] = _gelu_tanh(jnp.dot(xn, wy_ref[...].astype(BF16), preferred_element_type=F32))


def _norm_lru_in(x, g, w, layer, *, tm=1024, tn=512):
    m, k = x.shape
    half = w.shape[2] // 2
    nj = half // tn
    return pl.pallas_call(
        functools.partial(_norm_lru_in_kernel, tm=tm),
        grid=(m // tm, nj),
        in_specs=[
            pl.BlockSpec((tm, k), lambda i, j: (i, 0)),
            pl.BlockSpec((1, k), lambda i, j: (0, 0)),
            pl.BlockSpec((None, k, tn), lambda i, j: (layer, 0, j)),
            pl.BlockSpec((None, k, tn), lambda i, j: (layer, 0, j + nj)),
        ],
        out_specs=[pl.BlockSpec((tm, tn), lambda i, j: (i, j)),
                   pl.BlockSpec((tm, tn), lambda i, j: (i, j))],
        out_shape=[jax.ShapeDtypeStruct((m, half), F32), jax.ShapeDtypeStruct((m, half), F32)],
        scratch_shapes=[pltpu.VMEM((tm, k), BF16)],
        compiler_params=_params("parallel", "arbitrary"),
        name="norm_lru_in",
    )(x, g, w, w)


def _proj_res_kernel(a_ref, w_ref, b_ref, r_ref, o_ref, wb_ref):
    @pl.when(pl.program_id(1) == 0)
    def _():
        wb_ref[...] = w_ref[...].astype(BF16)
    acc = jnp.dot(a_ref[...], wb_ref[...], preferred_element_type=F32)
    o_ref[...] = r_ref[...] + (acc + b_ref[...])


def _proj_res(a, w, layer, b, res, *, tm=512, tn=2048):
    m, k = a.shape
    n = w.shape[2]
    return pl.pallas_call(
        _proj_res_kernel,
        grid=(n // tn, m // tm),
        in_specs=[
            pl.BlockSpec((tm, k), lambda j, i: (i, 0)),
            pl.BlockSpec((None, k, tn), lambda j, i: (layer, 0, j), pipeline_mode=pl.Buffered(1)),
            pl.BlockSpec((1, tn), lambda j, i: (0, j)),
            pl.BlockSpec((tm, tn), lambda j, i: (i, j)),
        ],
        out_specs=pl.BlockSpec((tm, tn), lambda j, i: (i, j)),
        out_shape=jax.ShapeDtypeStruct((m, n), F32),
        scratch_shapes=[pltpu.VMEM((k, tn), BF16)],
        compiler_params=_params("parallel", "arbitrary"),
        name="proj_res",
    )(a, w, b, res)


def _mlp_kernel(*refs, tm, final_norm, cast):
    refs = list(refs)
    x_ref, g_ref, w1_ref, w2_ref = refs[:4]
    del refs[:4]
    if final_norm:
        fg_ref = refs.pop(0)
    if cast:
        _cast_slabs(refs[0], refs[1], refs[3], refs[4])
        o_ref = refs[2]
    else:
        o_ref = refs[0]
    xn_ref = refs[-1]
    f = pl.program_id(1)

    @pl.when(f == 0)
    def _():
        _rms_rows_to(xn_ref, x_ref, g_ref, tm, 1e-6)
        o_ref[...] = x_ref[...]

    z = jnp.maximum(jnp.dot(xn_ref[...], w1_ref[...], preferred_element_type=F32), 0.0)
    o_ref[...] += jnp.dot((z * z).astype(BF16), w2_ref[...], preferred_element_type=F32)

    if final_norm:
        @pl.when(f == pl.num_programs(1) - 1)
        def _():
            def body(c, carry):
                rows = pl.ds(pl.multiple_of(c * NORM_ROWS, NORM_ROWS), NORM_ROWS)
                y = o_ref[rows, :]
                ms = jnp.mean(y * y, axis=-1, keepdims=True)
                o_ref[rows, :] = y * lax.rsqrt(ms + 1e-6) * fg_ref[...]
                return carry
            lax.fori_loop(0, tm // NORM_ROWS, body, 0)


def _mlp(x, g, w1, w2, final_g=None, mlp_weights=None, cast_layer=None, *, tm=512, tf=1024):
    m, d = x.shape
    ff = w1.shape[1]
    final_norm = final_g is not None
    cast = cast_layer is not None
    grid = (m // tm, ff // tf)
    row_spec = pl.BlockSpec((1, d), lambda i, f: (0, 0))
    cast_in, cast_out, cast_shapes = _mlp_cast_specs(grid, cast_layer) if cast else ([], [], [])
    return pl.pallas_call(
        functools.partial(_mlp_kernel, tm=tm, final_norm=final_norm, cast=cast),
        grid=grid,
        in_specs=[
            pl.BlockSpec((tm, d), lambda i, f: (i, 0)),
            row_spec,
            pl.BlockSpec((d, tf), lambda i, f: (0, f)),
            pl.BlockSpec((tf, d), lambda i, f: (f, 0)),
        ] + ([row_spec] if final_norm else []) + cast_in,
        out_specs=[pl.BlockSpec((tm, d), lambda i, f: (i, 0))] + cast_out,
        out_shape=[jax.ShapeDtypeStruct((m, d), F32)] + cast_shapes,
        scratch_shapes=[pltpu.VMEM((tm, d), BF16)],
        compiler_params=_params("parallel", "arbitrary"),
        name="mlp",
    )(x, g, w1, w2, *((final_g,) if final_norm else ()), *(mlp_weights if cast else ()))


ATTN_T = 512
ATTN_STRIP = 32
LOG2E = math.log2(math.e)


def _bucket_tiles(t):
    r = np.arange(t, dtype=np.int32)[:, None]
    c = np.arange(t, dtype=np.int32)[None, :]
    tiles = []
    for d in range(2):
        n = np.maximum(d * t + r - c, 0)
        max_exact = NUM_BUCKETS // 2
        nf = np.maximum(n, 1).astype(np.float32)
        large = max_exact + (np.log(nf / np.float32(max_exact)) / np.float32(math.log(MAX_DISTANCE / max_exact))
                             * np.float32(NUM_BUCKETS - max_exact)).astype(np.int32)
        large = np.minimum(large, NUM_BUCKETS - 1)
        bucket = np.where(n < max_exact, n, large).astype(np.int32)
        tiles.append(np.where(d * t + r - c >= 0, bucket, -1))
    return np.stack(tiles)


def _bias_needed(t):
    buckets = _bucket_tiles(t)
    last = NUM_BUCKETS - 1
    return [[[bool((buckets[d, r:r + ATTN_STRIP, c:c + LANES] != last).any())
              for c in range(0, t, LANES)] for r in range(0, t, ATTN_STRIP)] for d in range(2)]


def _bias_tiles_kernel(rb_ref, bucket_ref, o_ref):
    h = pl.program_id(0)
    bucket = bucket_ref[0]
    acc = jnp.zeros(bucket.shape, F32)
    for j in range(NUM_BUCKETS):
        acc = jnp.where(bucket == j, rb_ref[j, h], acc)
    acc = (acc - rb_ref[NUM_BUCKETS - 1, h]) * LOG2E
    o_ref[0, 0] = jnp.where(bucket < 0, MASK_VALUE, acc)


def _bias_tiles(rel_bias, t):
    buckets = jnp.asarray(_bucket_tiles(t))
    return pl.pallas_call(
        _bias_tiles_kernel,
        grid=(N_HEADS, 2),
        in_specs=[
            pl.BlockSpec(memory_space=pltpu.SMEM),
            pl.BlockSpec((1, t, t), lambda h, d: (d, 0, 0)),
        ],
        out_specs=pl.BlockSpec((1, 1, t, t), lambda h, d: (h, d, 0, 0)),
        out_shape=jax.ShapeDtypeStruct((N_HEADS, 2, t, t), F32),
        compiler_params=_params("parallel", "parallel"),
        name="bias_tiles",
    )(rel_bias, buckets)


def _attn_kernel(*refs, t, lambda_init, cast, q_per_step):
    lq1_ref, lk1_ref, lq2_ref, lk2_ref, sg_ref, q_ref, k_ref, v_ref, bias_ref = refs[:9]
    if cast:
        w1_ref, w2_ref, o_ref, w1b_ref, w2b_ref = refs[9:14]
        _cast_slabs(w1_ref, w2_ref, w1b_ref, w2b_ref)
    else:
        o_ref = refs[9]
    m_ref, l_ref, alpha_ref, acc_ref, s_ref, p_ref = refs[-6:]
    chunks = [slice(j * LANES, (j + 1) * LANES) for j in range(t // LANES)]
    bias_needed = _bias_needed(t)

    def start(ki):
        return pl.multiple_of(jnp.maximum(ki, 0) * t, t)

    def qk_logits(jq, ki, slot):
        for c in range(2):
            lanes = slice(c * HEAD_DIM, (c + 1) * HEAD_DIM)
            s_ref[slot, c] = lax.dot_general(q_ref[pl.ds(start(jq), t), lanes], k_ref[pl.ds(start(ki), t), lanes],
                                             (((1,), (1,)), ((), ())), preferred_element_type=F32)

    def accumulate(ki, slot):
        v = v_ref[pl.ds(start(ki), t), :]
        for c in range(2):
            alpha = alpha_ref[slot, c]
            pv = jnp.dot(p_ref[slot, c], v, preferred_element_type=F32)
            acc_ref[c] = jnp.concatenate([alpha, alpha], axis=1) * acc_ref[c] + pv

    strips = [slice(r * ATTN_STRIP, (r + 1) * ATTN_STRIP) for r in range(t // ATTN_STRIP)]

    def block(jq, ki, cur, bias_index, has_next):
        other = 1 - cur

        def logits(c, rows, ch):
            s = s_ref[cur, c, rows, ch]
            if bias_index is not None and bias_needed[bias_index][rows.start // ATTN_STRIP][ch.start // LANES]:
                s = s + bias_ref[0, bias_index, rows, ch]
            return s

        part_max = [[], []]
        for rows in strips:
            for c in range(2):
                part = logits(c, rows, chunks[0])
                for ch in chunks[1:]:
                    part = jnp.maximum(part, logits(c, rows, ch))
                part_max[c].append(part)

        accumulate(ki - 1, other)
        m_new, alpha = [], []
        for c in range(2):
            m_prev = m_ref[c]
            m_c = jnp.maximum(m_prev, jnp.max(jnp.concatenate(part_max[c], axis=0), axis=1, keepdims=True))
            alpha.append(jnp.exp2(m_prev - m_c))
            alpha_ref[cur, c] = alpha[c]
            m_ref[c] = m_c
            m_new.append(m_c)

        part_sum = [[], []]
        for rows in strips:
            for c in range(2):
                m_rows = m_new[c][rows, :]
                part = None
                for ch in chunks:
                    p = jnp.exp2(logits(c, rows, ch) - m_rows)
                    part = p if part is None else part + p
                    p_ref[cur, c, rows, ch] = p.astype(BF16)
                part_sum[c].append(part)

        if has_next:
            qk_logits(jq, ki + 1, other)
        for c in range(2):
            row_sum = jnp.sum(jnp.concatenate(part_sum[c], axis=0), axis=1, keepdims=True)
            l_ref[c] = alpha[c] * l_ref[c] + row_sum

    lam = (jnp.exp(jnp.sum(lq1_ref[...] * lk1_ref[...], keepdims=True))
           - jnp.exp(jnp.sum(lq2_ref[...] * lk2_ref[...], keepdims=True)) + lambda_init)

    def q_block(jq, carry):
        qi = pl.program_id(2) * q_per_step + jq

        m_ref[...] = jnp.full(m_ref.shape, MASK_VALUE, F32)
        l_ref[...] = jnp.zeros(l_ref.shape, F32)
        acc_ref[...] = jnp.zeros(acc_ref.shape, F32)
        alpha_ref[1] = jnp.zeros(alpha_ref.shape[1:], F32)
        p_ref[1] = jnp.zeros(p_ref.shape[1:], BF16)

        n_far = jnp.maximum(qi - 1, 0)

        def far_pair(j, carry):
            block(jq, 2 * j, 0, None, True)
            block(jq, 2 * j + 1, 1, None, True)
            return carry
        lax.fori_loop(0, n_far // 2, far_pair, 0)

        @pl.when(qi % 2 == 1)
        def _():
            block(jq, qi - 1, 0, 1, True)
            block(jq, qi, 1, 0, False)
            accumulate(qi, 1)

        @pl.when(qi % 2 == 0)
        def _():
            @pl.when(qi >= 2)
            def _():
                block(jq, qi - 2, 0, None, True)
                block(jq, qi - 1, 1, 1, True)
            block(jq, qi, 0, 0, False)
            accumulate(qi, 0)

        qk_logits(jnp.minimum(jq + 1, q_per_step - 1), 0, 0)
        inv0 = 1.0 / l_ref[0]
        inv1 = lam / l_ref[1]
        o = (acc_ref[0] * jnp.concatenate([inv0, inv0], axis=1)
             - acc_ref[1] * jnp.concatenate([inv1, inv1], axis=1))
        ms = jnp.mean(o * o, axis=-1, keepdims=True)
        o = o * lax.rsqrt(ms + 1e-5) * sg_ref[...] * (1.0 - lambda_init)
        o_ref[pl.ds(start(jq), t), :] = o.astype(o_ref.dtype)
        return carry

    qk_logits(0, 0, 0)
    lax.fori_loop(0, q_per_step, q_block, 0)


def _diff_attention(qkv, bias_tiles, lq1, lk1, lq2, lk2, subln_g, mlp_weights=None, cast_layer=None,
                    *, batch, seq, lambda_init):
    t = ATTN_T
    qps = seq // t
    groups = seq // (t * qps)
    grid = (batch, N_HEADS, groups)
    lam_spec = pl.BlockSpec((1, HEAD_DIM), lambda b, h, i: (0, 0))
    cast = cast_layer is not None
    cast_in, cast_out, cast_shapes = _mlp_cast_specs(grid, cast_layer) if cast else ([], [], [])
    return pl.pallas_call(
        functools.partial(_attn_kernel, t=t, lambda_init=lambda_init, cast=cast, q_per_step=qps),
        grid=grid,
        in_specs=[
            lam_spec, lam_spec, lam_spec, lam_spec,
            pl.BlockSpec((1, HEAD_W), lambda b, h, i: (0, 0)),
            pl.BlockSpec((qps * t, HEAD_W), lambda b, h, i: (b * groups + i, h)),
            pl.BlockSpec((seq, HEAD_W), lambda b, h, i: (b, N_HEADS + h)),
            pl.BlockSpec((seq, HEAD_W), lambda b, h, i: (b, 2 * N_HEADS + h)),
            pl.BlockSpec((1, 2, t, t), lambda b, h, i: (h, 0, 0, 0)),
        ] + cast_in,
        out_specs=[pl.BlockSpec((qps * t, HEAD_W), lambda b, h, i: (b * groups + i, h))] + cast_out,
        out_shape=[jax.ShapeDtypeStruct((batch * seq, D_MODEL), BF16)] + cast_shapes,
        scratch_shapes=[pltpu.VMEM((2, t, LANES), F32), pltpu.VMEM((2, t, LANES), F32),
                        pltpu.VMEM((2, 2, t, LANES), F32),
                        pltpu.VMEM((2, t, HEAD_W), F32), pltpu.VMEM((2, 2, t, t), F32),
                        pltpu.VMEM((2, 2, t, t), BF16)],
        compiler_params=_params("parallel", "parallel", "arbitrary"),
        name="diff_attn",
    )(lq1, lk1, lq2, lk2, subln_g, qkv, qkv, qkv, bias_tiles, *(mlp_weights if cast else ()))


CONV_T = 256
CONV_HALO = 32
CONV_ROWS = 64
CONV_LANES = 256


def _conv_ln_kernel(halo_ref, u_ref, w_ref, b_ref, g_ref, beta_ref, o_ref, sh_ref, y_ref, *, tiles_per_seq):
    i = pl.program_id(0)
    first = (i % tiles_per_seq) == 0

    @pl.when(first)
    def _():
        sh_ref[0, 0:CONV_HALO, :] = jnp.zeros((CONV_HALO, D_MODEL), F32)

    @pl.when(jnp.logical_not(first))
    def _():
        sh_ref[0, 0:CONV_HALO, :] = halo_ref[...]

    sh_ref[0, CONV_HALO:, :] = u_ref[...]
    window = CONV_HALO + CONV_T
    for s in range(1, SUBLANES):
        for c0 in range(0, D_MODEL, CONV_LANES):
            lanes = slice(c0, c0 + CONV_LANES)
            sh_ref[s, 0:window - SUBLANES, lanes] = sh_ref[0, s:s + window - SUBLANES, lanes]

    base = CONV_HALO - (CONV_WIDTH - 1)
    for r0 in range(0, CONV_T, CONV_ROWS):
        for c0 in range(0, D_MODEL, CONV_LANES):
            lanes = slice(c0, c0 + CONV_LANES)
            acc = jnp.zeros((CONV_ROWS, CONV_LANES), F32)
            for j in range(CONV_WIDTH):
                a, s = divmod(base + j, SUBLANES)
                x0 = r0 + a * SUBLANES
                acc = acc + w_ref[j:j + 1, lanes] * sh_ref[s, x0:x0 + CONV_ROWS, lanes]
            y_ref[r0:r0 + CONV_ROWS, lanes] = acc + b_ref[:, lanes]

    def ln_body(c, carry):
        r0 = pl.multiple_of(c * CONV_ROWS, CONV_ROWS)
        y = y_ref[pl.ds(r0, CONV_ROWS), :]
        mu = jnp.mean(y, axis=-1, keepdims=True)
        yc = y - mu
        var = jnp.mean(yc * yc, axis=-1, keepdims=True)
        z = yc * lax.rsqrt(var + 1e-5) * g_ref[...] + beta_ref[...]
        o_ref[pl.ds(r0, CONV_ROWS), :] = (z * _sigmoid(z)).astype(o_ref.dtype)
        return carry
    lax.fori_loop(0, CONV_T // CONV_ROWS, ln_body, 0)


def _conv_ln(u, dw_w, dw_b, ln_g, ln_b, *, seq):
    m, d = u.shape
    tiles_per_seq = seq // CONV_T
    ratio = CONV_T // CONV_HALO
    row = lambda i: (0, 0)
    return pl.pallas_call(
        functools.partial(_conv_ln_kernel, tiles_per_seq=tiles_per_seq),
        grid=(m // CONV_T,),
        in_specs=[
            pl.BlockSpec((CONV_HALO, d), lambda i: (jnp.maximum(i * ratio - 1, 0), 0)),
            pl.BlockSpec((CONV_T, d), lambda i: (i, 0)),
            pl.BlockSpec((CONV_WIDTH, d), row),
            pl.BlockSpec((1, d), row), pl.BlockSpec((1, d), row), pl.BlockSpec((1, d), row),
        ],
        out_specs=pl.BlockSpec((CONV_T, d), lambda i: (i, 0)),
        out_shape=jax.ShapeDtypeStruct((m, d), BF16),
        scratch_shapes=[pltpu.VMEM((SUBLANES, CONV_HALO + CONV_T, d), F32), pltpu.VMEM((CONV_T, d), F32)],
        compiler_params=_params("parallel"),
        name="conv_ln",
    )(u, u, dw_w, dw_b, ln_g, ln_b)


LRU_T = 1024
LRU_HALO = 8


def _lru_kernel(halo_ref, u_ref, y_ref, cw_ref, cb_ref, wa_ref, ba_ref, wi_ref, bi_ref, lam_ref,
                o_ref, buf_ref, carry_ref):
    ti = pl.program_id(2)
    first = ti == 0
    buf_ref[LRU_HALO:, :] = u_ref[...]

    @pl.when(first)
    def _():
        buf_ref[0:LRU_HALO, :] = jnp.zeros((LRU_HALO, LRU_BLOCK_SIZE), F32)
        carry_ref[...] = jnp.zeros(carry_ref.shape, F32)

    @pl.when(jnp.logical_not(first))
    def _():
        buf_ref[0:LRU_HALO, :] = halo_ref[...]

    base = LRU_HALO - (LRU_CONV_WIDTH - 1)
    uc = jnp.zeros((LRU_T, LRU_BLOCK_SIZE), F32)
    for j in range(LRU_CONV_WIDTH):
        uc = uc + cw_ref[j:j + 1, :] * buf_ref[base + j:base + j + LRU_T, :]
    uc = uc + cb_ref[...]

    ucb = uc.astype(BF16)
    r = _sigmoid(jnp.dot(ucb, wa_ref[0].astype(BF16), preferred_element_type=F32) + ba_ref[...])
    gate_i = _sigmoid(jnp.dot(ucb, wi_ref[0].astype(BF16), preferred_element_type=F32) + bi_ref[...])
    neg_lam = -lam_ref[...]
    softplus = jnp.maximum(neg_lam, 0.0) + jnp.log1p(jnp.exp(-jnp.abs(neg_lam)))
    log_a = -LRU_C * r * softplus
    a = jnp.exp(log_a)
    mult = jnp.sqrt(-jnp.tanh(log_a) * (1.0 + a * a))
    b = mult * (gate_i * uc)

    row = lax.broadcasted_iota(jnp.int32, (LRU_T, LRU_BLOCK_SIZE), 0)
    s = 1
    while s < SUBLANES:
        valid = row >= s
        a_sh = jnp.where(valid, pltpu.roll(a, s, 0), 1.0)
        b_sh = jnp.where(valid, pltpu.roll(b, s, 0), 0.0)
        b = a * b_sh + b
        a = a * a_sh
        s *= 2
    while s < LRU_T:
        b = jnp.concatenate([b[:s], a[s:] * b[:-s] + b[s:]], axis=0)
        a = jnp.concatenate([a[:s], a[s:] * a[:-s]], axis=0)
        s *= 2
    hs = a * carry_ref[0:1, :] + b
    carry_ref[0:1, :] = hs[LRU_T - 1:LRU_T, :]
    o_ref[...] = (hs * y_ref[...]).astype(o_ref.dtype)


def _lru_core(u, y, conv_w, conv_b, w_a, b_a, w_i, b_i, lam, *, batch, seq):
    m, d = u.shape
    nt = seq // LRU_T
    ratio = LRU_T // LRU_HALO
    c = LRU_BLOCK_SIZE
    chan = lambda b, g, t: (0, g)
    tile = lambda b, g, t: (b * nt + t, g)
    gate_w = pl.BlockSpec((1, c, c), lambda b, g, t: (g, 0, 0))
    return pl.pallas_call(
        _lru_kernel,
        grid=(batch, LRU_BLOCKS, nt),
        in_specs=[
            pl.BlockSpec((LRU_HALO, c), lambda b, g, t: (jnp.maximum((b * nt + t) * ratio - 1, 0), g)),
            pl.BlockSpec((LRU_T, c), tile),
            pl.BlockSpec((LRU_T, c), tile),
            pl.BlockSpec((LRU_CONV_WIDTH, c), chan),
            pl.BlockSpec((1, c), chan),
            gate_w, pl.BlockSpec((1, c), chan),
            gate_w, pl.BlockSpec((1, c), chan),
            pl.BlockSpec((1, c), chan),
        ],
        out_specs=pl.BlockSpec((LRU_T, c), tile),
        out_shape=jax.ShapeDtypeStruct((m, d), BF16),
        scratch_shapes=[pltpu.VMEM((LRU_HALO + LRU_T, c), F32), pltpu.VMEM((8, c), F32)],
        compiler_params=_params("parallel", "parallel", "arbitrary"),
        name="lru_core",
    )(u, u, y, conv_w, conv_b, w_a, b_a, w_i, b_i, lam)


def kernel(x, rel_bias, mixer_norm_g, attn_w_qkv, attn_lq1, attn_lk1, attn_lq2, attn_lk2, attn_subln_g, attn_w_o, conv_w_in, conv_b_in, conv_dw_w, conv_dw_b, conv_ln_g, conv_ln_b, conv_w_out, conv_b_out, lru_w_in, lru_conv_w, lru_conv_b, lru_w_a, lru_b_a, lru_w_i, lru_b_i, lru_lambda, lru_w_out, mlp_norm_g, mlp_w1, mlp_w2, final_norm_g):
    batch, seq, d = x.shape
    h = x.reshape(batch * seq, d)
    row = lambda v: v.reshape(1, -1)
    zero_bias = jnp.zeros((1, d), F32)
    q_scale = jnp.concatenate([jnp.full((1, d), HEAD_DIM ** -0.5 * LOG2E, F32), jnp.ones((1, 2 * d), F32)], axis=1)
    bias_tiles = _bias_tiles(rel_bias, ATTN_T)

    ia = ic = il = 0
    for layer in range(DEPTH):
        kind = layer % N_MIXERS
        g = row(mixer_norm_g[layer])
        if kind == 0:
            lambda_init = 0.8 - 0.6 * math.exp(-0.3 * layer)
            qkv = _norm_proj(h, g, attn_w_qkv, ia, q_scale)
            outs = _diff_attention(qkv, bias_tiles, row(attn_lq1[ia]), row(attn_lk1[ia]),
                                   row(attn_lq2[ia]), row(attn_lk2[ia]), row(attn_subln_g[ia]),
                                   (mlp_w1, mlp_w2), 0 if layer == 0 else None,
                                   batch=batch, seq=seq, lambda_init=lambda_init)
            if layer == 0:
                w1, w2 = outs[1:]
            h = _proj_res(outs[0], attn_w_o, ia, zero_bias, h)
            ia += 1
        elif kind == 1:
            u = _norm_glu(h, g, conv_w_in, ic, row(conv_b_in[ic]))
            act = _conv_ln(u, conv_dw_w[ic], row(conv_dw_b[ic]), row(conv_ln_g[ic]), row(conv_ln_b[ic]), seq=seq)
            h = _proj_res(act, conv_w_out, ic, row(conv_b_out[ic]), h)
            ic += 1
        else:
            u, y = _norm_lru_in(h, g, lru_w_in, il)
            hy = _lru_core(u, y, lru_conv_w[il], row(lru_conv_b[il]), lru_w_a[il], row(lru_b_a[il]),
                           lru_w_i[il], row(lru_b_i[il]), row(lru_lambda[il]), batch=batch, seq=seq)
            h = _proj_res(hy, lru_w_out, il, zero_bias, h)
            il += 1
        last = layer == DEPTH - 1
        outs = _mlp(h, row(mlp_norm_g[layer]), w1, w2, row(final_norm_g) if last else None,
                    (mlp_w1, mlp_w2), None if last else layer + 1)
        h = outs[0]
        if not last:
            w1, w2 = outs[1:]
    return h.reshape(batch, seq, d)
```
